```python
import jax, jax.numpy as jnp
from jax import lax
import numpy as np

D_MODEL = 1024
BATCH = 8
SEQ = 4096
DEPTH = 4

GRID_W = 64
CTX_LEN = 256
EPS = 1e-6

HEAD_DIM = 64
ATT_HEADS = 8
ATT_KV_HEADS = 2
ATT_GROUP = ATT_HEADS // ATT_KV_HEADS
WINDOW = 128
ATT_BLOCK = 128
ROPE_BASE = 10000.0
ATT_DIM = ATT_HEADS * HEAD_DIM
KV_DIM = ATT_KV_HEADS * HEAD_DIM

POOL_WINDOWS = (2, 4, 8, 16)
POOL_GROUPS = len(POOL_WINDOWS)
POOL_GROUP_DIM = 64
POOL_DIM = POOL_GROUPS * POOL_GROUP_DIM

GLA_HEADS = 4
GLA_DK = 32
GLA_DV = 64
GLA_GATE_RANK = 16
GLA_TAU = 16.0
GLA_CHUNK = 64
GLA_QK_DIM = GLA_HEADS * GLA_DK
GLA_V_DIM = GLA_HEADS * GLA_DV

N_BRANCH = 3

SPLITS = (ATT_DIM, KV_DIM, KV_DIM, POOL_DIM, GLA_QK_DIM, GLA_QK_DIM, GLA_V_DIM,
          GLA_V_DIM, 2 * GLA_GATE_RANK, N_BRANCH * D_MODEL)
SPLIT_IDX = tuple(int(s) for s in np.cumsum(SPLITS)[:-1])
IN_DIM = int(sum(SPLITS))

D_FF = 2816
CONV_W = 3

kernel_name = "hybrid_gated_branch_flow_block"

F32 = jnp.float32


def rmsnorm(x, g):
    xf = x.astype(F32)
    y = xf * lax.rsqrt(jnp.mean(xf * xf, axis=-1, keepdims=True) + EPS)
    return (y * g.astype(F32)).astype(x.dtype)


def axial_rope_tables(L):
    rows = L // GRID_W
    row = jnp.repeat(jnp.arange(rows, dtype=F32), GRID_W)
    col = jnp.tile(jnp.arange(GRID_W, dtype=F32), rows)
    n = HEAD_DIM // 4
    inv = ROPE_BASE ** (-jnp.arange(n, dtype=F32) / n)
    ang = jnp.concatenate([row[:, None] * inv, col[:, None] * inv], axis=-1)
    return jnp.cos(ang), jnp.sin(ang)


def apply_rope(x, cos, sin):
    xf = x.astype(F32)
    half = HEAD_DIM // 2
    x1, x2 = xf[..., :half], xf[..., half:]
    c = cos[None, :, None, :]
    s = sin[None, :, None, :]
    return jnp.concatenate([x1 * c - x2 * s, x1 * s + x2 * c], axis=-1).astype(x.dtype)


def window_attention(q, k, v, k_ctx, v_ctx, sink):
    B, L = q.shape[:2]
    C = k_ctx.shape[1]
    nb = L // ATT_BLOCK
    scale = HEAD_DIM ** -0.5
    qb = q.reshape(B, nb, ATT_BLOCK, ATT_KV_HEADS, ATT_GROUP, HEAD_DIM)
    pad = ((0, 0), (ATT_BLOCK, ATT_BLOCK), (0, 0), (0, 0))
    kp = jnp.pad(k, pad).reshape(B, nb + 2, ATT_BLOCK, ATT_KV_HEADS, HEAD_DIM)
    vp = jnp.pad(v, pad).reshape(B, nb + 2, ATT_BLOCK, ATT_KV_HEADS, HEAD_DIM)
    kw = jnp.concatenate([kp[:, :-2], kp[:, 1:-1], kp[:, 2:]], axis=2)
    vw = jnp.concatenate([vp[:, :-2], vp[:, 1:-1], vp[:, 2:]], axis=2)
    s_loc = jnp.einsum('bnqkgd,bnskd->bnkgqs', qb, kw, preferred_element_type=F32) * scale
    s_ctx = jnp.einsum('bnqkgd,bckd->bnkgqc', qb, k_ctx, preferred_element_type=F32) * scale
    qi = jnp.arange(ATT_BLOCK)[:, None]
    kj = jnp.arange(3 * ATT_BLOCK)[None, :]
    band = jnp.abs(kj - ATT_BLOCK - qi) <= WINDOW
    kpos = jnp.arange(nb)[:, None] * ATT_BLOCK - ATT_BLOCK + jnp.arange(3 * ATT_BLOCK)[None, :]
    valid = (kpos >= 0) & (kpos < L)
    mask = band[None, :, :] & valid[:, None, :]
    s_loc = jnp.where(mask[None, :, None, None], s_loc, -jnp.inf)
    sk = jnp.broadcast_to(sink.astype(F32).reshape(ATT_KV_HEADS, ATT_GROUP)[:, :, None, None],
                          s_loc.shape[:-1] + (1,))
    p = jax.nn.softmax(jnp.concatenate([s_loc, s_ctx, sk], axis=-1), axis=-1)
    nl = 3 * ATT_BLOCK
    p_loc = p[..., :nl].astype(v.dtype)
    p_ctx = p[..., nl:nl + C].astype(v.dtype)
    o = (jnp.einsum('bnkgqs,bnskd->bnqkgd', p_loc, vw)
         + jnp.einsum('bnkgqc,bckd->bnqkgd', p_ctx, v_ctx))
    return o.reshape(B, L, ATT_DIM)


def ctx_attention(q, k, v, sink):
    B, C = q.shape[:2]
    qg = q.reshape(B, C, ATT_KV_HEADS, ATT_GROUP, HEAD_DIM)
    s = jnp.einsum('bqkgd,bskd->bkgqs', qg, k, preferred_element_type=F32) * (HEAD_DIM ** -0.5)
    sk = jnp.broadcast_to(sink.astype(F32).reshape(ATT_KV_HEADS, ATT_GROUP)[:, :, None, None],
                          s.shape[:-1] + (1,))
    p = jax.nn.softmax(jnp.concatenate([s, sk], axis=-1), axis=-1)[..., :C].astype(v.dtype)
    return jnp.einsum('bkgqs,bskd->bqkgd', p, v).reshape(B, C, ATT_DIM)


def multiscale_pool(u):
    T = u.shape[1]
    uf = u.astype(F32)
    cs = jnp.pad(jnp.cumsum(uf, axis=1), ((0, 0), (1, 0), (0, 0)))
    t = jnp.arange(T)
    outs = []
    for g, w in enumerate(POOL_WINDOWS):
        lo = w // 2
        hi = w - lo - 1
        start = jnp.clip(t - lo, 0, T)
        end = jnp.clip(t + hi + 1, 0, T)
        sl = slice(g * POOL_GROUP_DIM, (g + 1) * POOL_GROUP_DIM)
        csg = cs[..., sl]
        cnt = (end - start).astype(F32)
        outs.append((csg[:, end] - csg[:, start]) / cnt[None, :, None] - uf[..., sl])
    return jnp.concatenate(outs, axis=-1)


def pool_branch(u, pool_w, pool_scale):
    B, T, _ = u.shape
    d = multiscale_pool(u).astype(u.dtype).reshape(B, T, POOL_GROUPS, POOL_GROUP_DIM)
    y = jnp.einsum('btgi,gio->btgo', d, pool_w).reshape(B, T, POOL_DIM)
    return y * pool_scale


def gla_inputs(gq, gk, gv, glr, gla_wa2, gla_ba):
    B, T = gq.shape[:2]
    q = gq.reshape(B, T, GLA_HEADS, GLA_DK) * (GLA_DK ** -0.5)
    k = gk.reshape(B, T, GLA_HEADS, GLA_DK)
    v = gv.reshape(B, T, GLA_HEADS, GLA_DV)
    lr = glr.astype(F32).reshape(B, T, 2, GLA_GATE_RANK)
    z = jnp.einsum('btnr,nrd->btnd', lr, gla_wa2.astype(F32)) + gla_ba.astype(F32)
    log_a = jax.nn.log_sigmoid(z) / GLA_TAU
    la_f = log_a[:, :, 0].reshape(B, T, GLA_HEADS, GLA_DK)
    la_b = log_a[:, :, 1].reshape(B, T, GLA_HEADS, GLA_DK)
    return q, k, v, la_f, la_b


def gla_scan(q, k, v, log_a, state0):
    B, T = q.shape[:2]
    n = T // GLA_CHUNK

    def to_chunks(t):
        return t.reshape(B, n, GLA_CHUNK, GLA_HEADS, -1).transpose(1, 0, 3, 2, 4).astype(F32)

    tri = jnp.tril(jnp.ones((GLA_CHUNK, GLA_CHUNK), dtype=bool))

    def step(S, inp):
        qi, ki, vi, ai = inp
        b = jnp.cumsum(ai, axis=2)
        b_last = b[:, :, -1:, :]
        o_inter = jnp.einsum('bhcd,bhde->bhce', qi * jnp.exp(b), S)
        diff = b[:, :, :, None, :] - b[:, :, None, :, :]
        decay = jnp.exp(jnp.where(tri[:, :, None], diff, -jnp.inf))
        att = jnp.einsum('bhid,bhjd,bhijd->bhij', qi, ki, decay)
        o = o_inter + jnp.einsum('bhij,bhje->bhie', att, vi)
        S_new = (S * jnp.exp(b_last[:, :, 0, :])[..., None]
                 + jnp.einsum('bhcd,bhce->bhde', ki * jnp.exp(b_last - b), vi))
        return S_new, o

    S_fin, o = lax.scan(step, state0, (to_chunks(q), to_chunks(k), to_chunks(v), to_chunks(log_a)))
    o = o.transpose(1, 0, 3, 2, 4).reshape(B, T, GLA_HEADS, GLA_DV)
    return o, S_fin


def flip(t):
    return t[:, ::-1]


def gla_output(o, r, gla_norm):
    B, T = o.shape[:2]
    on = o * lax.rsqrt(jnp.mean(o * o, axis=-1, keepdims=True) + EPS) * gla_norm.astype(F32)
    return on.reshape(B, T, GLA_V_DIM).astype(r.dtype) * jax.nn.silu(r)


def merge_branches(y_att, y_pool, y_gla, mg, w_br_att, w_br_pool, w_br_gla, w_o):
    B, T = y_att.shape[:2]
    g = jax.nn.sigmoid(mg).reshape(B, T, N_BRANCH, D_MODEL)
    m = (g[:, :, 0] * (y_att @ w_br_att) + g[:, :, 1] * (y_pool @ w_br_pool)
         + g[:, :, 2] * (y_gla @ w_br_gla))
    return m @ w_o


def token_mixer(h, hc, need_ctx, w_in, att_sink, pool_w, pool_scale, gla_wa2, gla_ba, gla_norm,
                w_br_att, w_br_pool, w_br_gla, w_o):
    B, L, _ = h.shape
    C = hc.shape[1]
    aq, ak, av, pu, gq, gk, gv, gr, glr, mg = jnp.split(h @ w_in, SPLIT_IDX, axis=-1)
    caq, cak, cav, cpu, cgq, cgk, cgv, cgr, cglr, cmg = jnp.split(hc @ w_in, SPLIT_IDX, axis=-1)

    k_ctx = cak.reshape(B, C, ATT_KV_HEADS, HEAD_DIM)
    v_ctx = cav.reshape(B, C, ATT_KV_HEADS, HEAD_DIM)
    cos, sin = axial_rope_tables(L)
    q = apply_rope(aq.reshape(B, L, ATT_HEADS, HEAD_DIM), cos, sin)
    k = apply_rope(ak.reshape(B, L, ATT_KV_HEADS, HEAD_DIM), cos, sin)
    v = av.reshape(B, L, ATT_KV_HEADS, HEAD_DIM)
    y_att = window_attention(q, k, v, k_ctx, v_ctx, att_sink)

    y_pool = pool_branch(pu, pool_w, pool_scale)

    lq, lk, lv, la_f, la_b = gla_inputs(gq, gk, gv, glr, gla_wa2, gla_ba)
    xq, xk, xv, xa_f, xa_b = gla_inputs(cgq, cgk, cgv, cglr, gla_wa2, gla_ba)
    S0 = jnp.zeros((B, GLA_HEADS, GLA_DK, GLA_DV), F32)
    co_f, S_f = gla_scan(xq, xk, xv, xa_f, S0)
    co_b, S_b = gla_scan(flip(xq), flip(xk), flip(xv), flip(xa_b), S0)
    o_f, _ = gla_scan(lq, lk, lv, la_f, S_f)
    o_b, _ = gla_scan(flip(lq), flip(lk), flip(lv), flip(la_b), S_b)
    y_gla = gla_output(o_f + flip(o_b), gr, gla_norm)

    y = merge_branches(y_att, y_pool, y_gla, mg, w_br_att, w_br_pool, w_br_gla, w_o)
    if not need_ctx:
        return y, None
    yc_att = ctx_attention(caq.reshape(B, C, ATT_HEADS, HEAD_DIM), k_ctx, v_ctx, att_sink)
    yc_pool = pool_branch(cpu, pool_w, pool_scale)
    yc_gla = gla_output(co_f + flip(co_b), cgr, gla_norm)
    yc = merge_branches(yc_att, yc_pool, yc_gla, cmg, w_br_att, w_br_pool, w_br_gla, w_o)
    return y, yc


def conv_ffn(h, w_up, conv_w, conv_b, w_down):
    u = h @ w_up
    up = jnp.pad(u, ((0, 0), (1, 1), (0, 0)))
    u = up[:, :-2] * conv_w[0] + up[:, 1:-1] * conv_w[1] + up[:, 2:] * conv_w[2] + conv_b
    a, g = jnp.split(u, 2, axis=-1)
    return (a * jax.nn.silu(g)) @ w_down


def setup_inputs(seed: int = 0) -> dict:
    key = jax.random.key(seed)
    ks = jax.random.split(key, 28)

    def nrm(k, shape, scale):
        return jax.random.normal(k, shape, F32) * scale

    return {
        "x": nrm(ks[0], (BATCH, SEQ, D_MODEL), 1.0),
        "c": nrm(ks[1], (BATCH, D_MODEL), 1.0),
        "ctx": nrm(ks[2], (BATCH, CTX_LEN, D_MODEL), 1.0),
        "c_ctx": nrm(ks[3], (D_MODEL,), 1.0),
        "w_ada": nrm(ks[4], (DEPTH, D_MODEL, 6 * D_MODEL), 0.5 * D_MODEL ** -0.5),
        "b_ada": nrm(ks[5], (DEPTH, 6 * D_MODEL), 0.01),
        "g_pre_mix": 1.0 + nrm(ks[6], (DEPTH, D_MODEL), 0.02),
        "g_post_mix": 1.0 + nrm(ks[7], (DEPTH, D_MODEL), 0.02),
        "g_pre_ffn": 1.0 + nrm(ks[8], (DEPTH, D_MODEL), 0.02),
        "g_post_ffn": 1.0 + nrm(ks[9], (DEPTH, D_MODEL), 0.02),
        "w_in": nrm(ks[10], (DEPTH, D_MODEL, IN_DIM), D_MODEL ** -0.5),
        "att_sink": nrm(ks[11], (DEPTH, ATT_HEADS), 0.5),
        "pool_w": nrm(ks[12], (DEPTH, POOL_GROUPS, POOL_GROUP_DIM, POOL_GROUP_DIM), POOL_GROUP_DIM ** -0.5),
        "pool_scale": 1.0 + nrm(ks[13], (DEPTH, POOL_DIM), 0.02),
        "gla_wa2": nrm(ks[14], (DEPTH, 2, GLA_GATE_RANK, GLA_QK_DIM), GLA_GATE_RANK ** -0.5),
        "gla_ba": 1.0 + nrm(ks[15], (DEPTH, 2, GLA_QK_DIM), 0.1),
        "gla_norm": 1.0 + nrm(ks[16], (DEPTH, GLA_HEADS, GLA_DV), 0.02),
        "w_br_att": nrm(ks[17], (DEPTH, ATT_DIM, D_MODEL), ATT_DIM ** -0.5),
        "w_br_pool": nrm(ks[18], (DEPTH, POOL_DIM, D_MODEL), POOL_DIM ** -0.5),
        "w_br_gla": nrm(ks[19], (DEPTH, GLA_V_DIM, D_MODEL), GLA_V_DIM ** -0.5),
        "w_o": nrm(ks[20], (DEPTH, D_MODEL, D_MODEL), D_MODEL ** -0.5),
        "w_up": nrm(ks[21], (DEPTH, D_MODEL, 2 * D_FF), D_MODEL ** -0.5),
        "conv_w": nrm(ks[22], (DEPTH, CONV_W, 2 * D_FF), CONV_W ** -0.5),
        "conv_b": nrm(ks[23], (DEPTH, 2 * D_FF), 0.01),
        "w_down": nrm(ks[24], (DEPTH, D_FF, D_MODEL), D_FF ** -0.5),
    }


def reference(x, c, ctx, c_ctx, w_ada, b_ada, g_pre_mix, g_post_mix, g_pre_ffn, g_post_ffn,
              w_in, att_sink, pool_w, pool_scale, gla_wa2, gla_ba, gla_norm,
              w_br_att, w_br_pool, w_br_gla, w_o, w_up, conv_w, conv_b, w_down):
    for l in range(DEPTH):
        last = l == DEPTH - 1
        m = jax.nn.silu(c) @ w_ada[l] + b_ada[l]
        mc = jax.nn.silu(c_ctx) @ w_ada[l] + b_ada[l]
        sh1, sc1, g1, sh2, sc2, g2 = jnp.split(m[:, None, :], 6, axis=-1)
        csh1, csc1, cg1, csh2, csc2, cg2 = jnp.split(mc, 6)

        h = rmsnorm(x, g_pre_mix[l]) * (1.0 + sc1) + sh1
        hc = rmsnorm(ctx, g_pre_mix[l]) * (1.0 + csc1) + csh1
        y, yc = token_mixer(h, hc, not last, w_in[l], att_sink[l], pool_w[l], pool_scale[l],
                            gla_wa2[l], gla_ba[l], gla_norm[l], w_br_att[l], w_br_pool[l],
                            w_br_gla[l], w_o[l])
        x = x + g1 * rmsnorm(y, g_post_mix[l])
        h = rmsnorm(x, g_pre_ffn[l]) * (1.0 + sc2) + sh2
        x = x + g2 * rmsnorm(conv_ffn(h, w_up[l], conv_w[l], conv_b[l], w_down[l]), g_post_ffn[l])

        if not last:
            ctx = ctx + cg1 * rmsnorm(yc, g_post_mix[l])
            hc = rmsnorm(ctx, g_pre_ffn[l]) * (1.0 + csc2) + csh2
            ctx = ctx + cg2 * rmsnorm(conv_ffn(hc, w_up[l], conv_w[l], conv_b[l], w_down[l]), g_post_ffn[l])
    return x
```

```python
import functools

import jax
import jax.numpy as jnp
import numpy as np
from jax import lax
from jax.experimental import pallas as pl
from jax.experimental.pallas import tpu as pltpu

F32 = jnp.float32
BF16 = jnp.bfloat16
HIGHEST = lax.Precision.HIGHEST

D_MODEL = 1024
GRID_W = 64
EPS = 1e-6

HEAD_DIM = 64
ATT_HEADS = 8
ATT_KV_HEADS = 2
ATT_GROUP = ATT_HEADS // ATT_KV_HEADS
WINDOW = 128
ROPE_BASE = 10000.0
ATT_DIM = ATT_HEADS * HEAD_DIM
KV_DIM = ATT_KV_HEADS * HEAD_DIM

POOL_WINDOWS = (2, 4, 8, 16)
POOL_GROUP_DIM = 64
POOL_DIM = len(POOL_WINDOWS) * POOL_GROUP_DIM
POOL_PAD = 16

GLA_HEADS = 4
GLA_DK = 32
GLA_DV = 64
GLA_GATE_RANK = 16
GLA_TAU = 16.0
GLA_QK_DIM = GLA_HEADS * GLA_DK
GLA_V_DIM = GLA_HEADS * GLA_DV
GLA_CHUNK = 64

D_FF = 2816
FF_CHUNK = 256

LANE = 128
SUBLANE = 8
ROW_TILE = 256

_GLR_PAD = LANE - 2 * GLA_GATE_RANK
_O_AQ = 0
_O_AK = _O_AQ + ATT_DIM
_O_AV = _O_AK + KV_DIM
_O_PU = _O_AV + KV_DIM
_O_GQ = _O_PU + POOL_DIM
_O_GK = _O_GQ + GLA_QK_DIM
_O_GV = _O_GK + GLA_QK_DIM
_O_GR = _O_GV + GLA_V_DIM
_O_GLR = _O_GR + GLA_V_DIM
_O_MG = _O_GLR + LANE
IN_DIM_PAD = _O_MG + 3 * D_MODEL

NEG_BIG = -1e30


def _params(vmem_mb, n_axes):
    return pltpu.CompilerParams(
        dimension_semantics=("arbitrary",) * n_axes,
        vmem_limit_bytes=vmem_mb * 1024 * 1024,
    )


def _resident(shape):
    nd = len(shape)
    return pl.BlockSpec(shape, lambda *_: (0,) * nd, pipeline_mode=pl.Buffered(1))


def _sigmoid(v):
    return 1.0 / (1.0 + jnp.exp(-v))


def _iota_div(shape, axis, divisor):
    shift = int(np.log2(divisor))
    assert 1 << shift == divisor
    return lax.shift_right_logical(lax.broadcasted_iota(jnp.int32, shape, axis), shift)


def _shifted_rows(ext, shift, n_rows):
    if shift:
        ext = pltpu.roll(ext, (-shift) % ext.shape[0], 0)
    return ext[SUBLANE:SUBLANE + n_rows]


def _modulated_norm(x, gain, scale, shift):
    ms = jnp.mean(x * x, axis=-1, keepdims=True)
    return (x * lax.rsqrt(ms + EPS) * gain) * (1.0 + scale) + shift


def _ada_kernel(c_ref, w_ref, b_ref, o_ref):
    c = c_ref[...]
    a = c * _sigmoid(c)
    o_ref[0] = jnp.dot(a, w_ref[0], precision=HIGHEST, preferred_element_type=F32) + b_ref[0]


def _ada_table(cc, w_ada, b_ada):
    depth = w_ada.shape[0]
    rows = cc.shape[0]
    n_col = w_ada.shape[2] // D_MODEL
    return pl.pallas_call(
        _ada_kernel,
        grid=(depth, n_col),
        in_specs=[
            pl.BlockSpec((rows, D_MODEL), lambda l, j: (0, 0)),
            pl.BlockSpec((1, D_MODEL, D_MODEL), lambda l, j: (l, 0, j)),
            pl.BlockSpec((1, 1, D_MODEL), lambda l, j: (l, 0, j)),
        ],
        out_specs=pl.BlockSpec((1, rows, D_MODEL), lambda l, j: (l, 0, j)),
        out_shape=jax.ShapeDtypeStruct((depth, rows, w_ada.shape[2]), F32),
        compiler_params=_params(32, 2),
        name="ada_table",
    )(cc, w_ada, b_ada.reshape(depth, 1, -1))


def _in_proj_kernel(x_ref, mod_ref, g_ref, w_ref, rope_ref, wa_ref, ba_ref,
                    q_ref, kv_ref, pu_ref, gqkv_ref, sgr_ref, la_ref, mg_ref):
    D = D_MODEL
    mod = mod_ref[0]
    h = _modulated_norm(x_ref[0], g_ref[...], mod[:, D:2 * D], mod[:, 0:D]).astype(BF16)

    def proj(lo, width):
        return jnp.dot(h, w_ref[:, lo:lo + width], preferred_element_type=F32)

    cosf = rope_ref[:, 0:LANE]
    sin_lo = rope_ref[:, LANE:2 * LANE]
    sin_hi = rope_ref[:, 2 * LANE:3 * LANE]

    def rope(t):
        return (t * cosf + pltpu.roll(t, LANE - HEAD_DIM // 2, 1) * sin_lo
                + pltpu.roll(t, HEAD_DIM // 2, 1) * sin_hi)

    aq = proj(_O_AQ, ATT_DIM)
    for j in range(ATT_DIM // LANE):
        q_ref[0, :, j * LANE:(j + 1) * LANE] = (
            rope(aq[:, j * LANE:(j + 1) * LANE]) * (HEAD_DIM ** -0.5)).astype(BF16)
    kv_ref[0, :, 0:KV_DIM] = rope(proj(_O_AK, KV_DIM)).astype(BF16)
    kv_ref[0, :, KV_DIM:2 * KV_DIM] = proj(_O_AV, KV_DIM).astype(BF16)

    pu_ref[0] = proj(_O_PU, POOL_DIM)

    gqkv_ref[0, :, 0:GLA_QK_DIM] = (proj(_O_GQ, GLA_QK_DIM) * (GLA_DK ** -0.5)).astype(BF16)
    gqkv_ref[0, :, GLA_QK_DIM:] = proj(_O_GK, GLA_QK_DIM + GLA_V_DIM).astype(BF16)
    gr = proj(_O_GR, GLA_V_DIM)
    sgr_ref[0] = (gr * _sigmoid(gr)).astype(BF16)

    z = jnp.dot(proj(_O_GLR, LANE), wa_ref[...], precision=HIGHEST,
                preferred_element_type=F32) + ba_ref[...]
    la_ref[0] = (jnp.minimum(z, 0.0) - jnp.log(1.0 + jnp.exp(-jnp.abs(z)))) * (1.0 / GLA_TAU)

    for j in range(3):
        mg_ref[0, :, j * D:(j + 1) * D] = _sigmoid(proj(_O_MG + j * D, D)).astype(BF16)


def _in_proj(xa, mod, g_pre, w_in, rope_tab, wa, ba, n_ctx_tiles):
    B, T, D = xa.shape
    R = ROW_TILE
    nt = T // R
    ctx_row = mod.shape[0] - 1

    def tile(width):
        return pl.BlockSpec((1, R, width), lambda b, i: (b, i, 0))

    widths = (ATT_DIM, 2 * KV_DIM, POOL_DIM, 2 * GLA_QK_DIM + GLA_V_DIM, GLA_V_DIM,
              2 * GLA_QK_DIM, 3 * D)
    dtypes = (BF16, BF16, F32, BF16, BF16, F32, BF16)
    return pl.pallas_call(
        _in_proj_kernel,
        grid=(B, nt),
        in_specs=[
            tile(D),
            pl.BlockSpec((1, 1, 6 * D), lambda b, i: (jnp.where(i < n_ctx_tiles, ctx_row, b), 0, 0)),
            _resident((1, D)),
            _resident((D, IN_DIM_PAD)),
            pl.BlockSpec((R, 3 * LANE), lambda b, i: (i, 0)),
            _resident((LANE, 2 * GLA_QK_DIM)),
            _resident((1, 2 * GLA_QK_DIM)),
        ],
        out_specs=[tile(w) for w in widths],
        out_shape=[jax.ShapeDtypeStruct((B, T, w), dt) for w, dt in zip(widths, dtypes)],
        compiler_params=_params(48, 2),
        name="in_proj",
    )(xa, mod, g_pre, w_in, rope_tab, wa, ba)


def _attn_kernel(sink_ref, q_ref, kv_ref, o_ref, *, tile_off, n_ctx_tiles, n_ctx, n_lat):
    R = ROW_TILE
    W = WINDOW
    n_loc = R + 2 * W
    i = pl.program_id(1) + tile_off
    n_blk = (n_ctx + n_lat) // W

    qi = lax.broadcasted_iota(jnp.int32, (R, n_loc), 0)
    kj = lax.broadcasted_iota(jnp.int32, (R, n_loc), 1)
    kpos = (i - n_ctx_tiles) * R - W + kj
    ok = (jnp.abs(kj - W - qi) <= W) & (kpos >= 0) & (kpos < n_lat) & (i >= n_ctx_tiles)
    bias_loc = jnp.where(ok, 0.0, NEG_BIG).astype(F32)

    first_blk = i * (R // W) - 1
    parts = []
    for blk in range(n_loc // W):
        idx = jnp.clip(first_blk + blk, 0, n_blk - 1)
        parts.append(kv_ref[0, pl.ds(pl.multiple_of(idx * W, W), W), :])
    parts.append(kv_ref[0, 0:n_ctx, :])
    kv = jnp.concatenate(parts, axis=0)

    nt_dims = (((1,), (1,)), ((), ()))
    q = q_ref[0]
    outs = []
    for g in range(ATT_KV_HEADS):
        k = kv[:, g * HEAD_DIM:(g + 1) * HEAD_DIM]
        v = kv[:, KV_DIM + g * HEAD_DIM:KV_DIM + (g + 1) * HEAD_DIM]
        for hh in range(ATT_GROUP):
            h = g * ATT_GROUP + hh
            qh = q[:, h * HEAD_DIM:(h + 1) * HEAD_DIM]
            s = lax.dot_general(qh, k, nt_dims, preferred_element_type=F32)
            s_loc = s[:, 0:n_loc] + bias_loc
            s_ctx = s[:, n_loc:]
            sink = sink_ref[h]
            m = jnp.maximum(jnp.maximum(jnp.max(s_loc, axis=-1, keepdims=True),
                                        jnp.max(s_ctx, axis=-1, keepdims=True)), sink)
            p_loc = jnp.exp(s_loc - m)
            p_ctx = jnp.exp(s_ctx - m)
            den = (jnp.sum(p_loc, axis=-1, keepdims=True) + jnp.sum(p_ctx, axis=-1, keepdims=True)
                   + jnp.exp(sink - m))
            o = (jnp.dot(p_loc.astype(BF16), v[0:n_loc], preferred_element_type=F32)
                 + jnp.dot(p_ctx.astype(BF16), v[n_loc:], preferred_element_type=F32))
            outs.append(o / den)
    o_ref[0] = jnp.concatenate(outs, axis=-1).astype(BF16)


def _attention(q, kv, sink, n_ctx, tile_off):
    B, T, _ = q.shape
    R = ROW_TILE
    nt = T // R
    kern = functools.partial(_attn_kernel, tile_off=tile_off, n_ctx_tiles=n_ctx // R,
                             n_ctx=n_ctx, n_lat=T - n_ctx)
    return pl.pallas_call(
        kern,
        grid=(B, nt - tile_off),
        in_specs=[
            pl.BlockSpec(memory_space=pltpu.SMEM),
            pl.BlockSpec((1, R, ATT_DIM), lambda b, i: (b, i + tile_off, 0)),
            pl.BlockSpec((1, T, 2 * KV_DIM), lambda b, i: (b, 0, 0)),
        ],
        out_specs=pl.BlockSpec((1, R, ATT_DIM), lambda b, i: (b, i + tile_off, 0)),
        out_shape=jax.ShapeDtypeStruct((B, T, ATT_DIM), BF16),
        compiler_params=_params(48, 2),
        name="attention",
    )(sink, q, kv)


def _pool_kernel(u_ref, o_ref, p0, p2, p4, p8, *, n_ctx, n_lat):
    R = ROW_TILE
    T = n_ctx + n_lat
    RE = R + 2 * SUBLANE
    off_ctx = POOL_PAD
    off_lat = 2 * POOL_PAD
    n_comp = p0.shape[0] // R - 1

    for buf in (p0, p2, p4, p8):
        buf[...] = jnp.zeros(buf.shape, F32)
    p0[off_ctx:off_ctx + n_ctx, :] = u_ref[0, 0:n_ctx, :]
    p0[n_ctx + off_lat:T + off_lat, :] = u_ref[0, n_ctx:T, :]

    def stage(src, dst, back, fwd):
        def body(c, carry):
            r = pl.multiple_of(c * R, R)
            ext = src[pl.ds(r, RE), :]
            dst[pl.ds(r + SUBLANE, R), :] = _shifted_rows(ext, -back, R) + _shifted_rows(ext, fwd, R)
            return carry
        lax.fori_loop(0, n_comp, body, 0)

    stage(p0, p2, 1, 0)
    stage(p2, p4, 1, 1)
    stage(p4, p8, 2, 2)

    lane_group = _iota_div((R, POOL_DIM), 1, POOL_GROUP_DIM)
    row = lax.broadcasted_iota(jnp.int32, (R, POOL_DIM), 0)

    def emit(i, carry):
        is_ctx = i < n_ctx // R
        r0 = pl.multiple_of(i * R, R)
        r = pl.multiple_of(r0 + jnp.where(is_ctx, off_ctx, off_lat), SUBLANE)
        pos = row + jnp.where(is_ctx, r0, r0 - n_ctx)
        seq_len = jnp.where(is_ctx, n_ctx, n_lat)
        ext8 = p8[pl.ds(r - SUBLANE, RE), :]
        sums = (p2[pl.ds(r, R), :], p4[pl.ds(r, R), :], ext8[SUBLANE:SUBLANE + R],
                _shifted_rows(ext8, -4, R) + _shifted_rows(ext8, 4, R))
        tot = sums[0]
        lo = jnp.full((R, POOL_DIM), POOL_WINDOWS[0] // 2, jnp.int32)
        hi = jnp.full((R, POOL_DIM), POOL_WINDOWS[0] - POOL_WINDOWS[0] // 2 - 1, jnp.int32)
        for g in range(1, len(POOL_WINDOWS)):
            w = POOL_WINDOWS[g]
            sel = lane_group == g
            tot = jnp.where(sel, sums[g], tot)
            lo = jnp.where(sel, w // 2, lo)
            hi = jnp.where(sel, w - w // 2 - 1, hi)
        cnt = (jnp.minimum(pos + hi + 1, seq_len) - jnp.maximum(pos - lo, 0)).astype(F32)
        o_ref[0, pl.ds(r0, R), :] = (tot / cnt - u_ref[0, pl.ds(r0, R), :]).astype(BF16)
        return carry

    lax.fori_loop(0, T // R, emit, 0)


def _pool(pu, n_ctx):
    B, T, _ = pu.shape
    R = ROW_TILE
    rows = (pl.cdiv(T + 3 * POOL_PAD, R) + 1) * R
    kern = functools.partial(_pool_kernel, n_ctx=n_ctx, n_lat=T - n_ctx)
    return pl.pallas_call(
        kern,
        grid=(B,),
        in_specs=[pl.BlockSpec((1, T, POOL_DIM), lambda b: (b, 0, 0))],
        out_specs=pl.BlockSpec((1, T, POOL_DIM), lambda b: (b, 0, 0)),
        out_shape=jax.ShapeDtypeStruct((B, T, POOL_DIM), BF16),
        scratch_shapes=[pltpu.VMEM((rows, POOL_DIM), F32)] * 4,
        compiler_params=_params(48, 1),
        name="pool",
    )(pu)


def _gla_kernel(qkv_ref, la_ref, gn_ref, o_ref, of_s, ob_s, stf, stb, *, n_ctx):
    CH = GLA_CHUNK
    R = ROW_TILE
    T = of_s.shape[0]
    n_chunks = T // CH
    n_ctx_chunks = n_ctx // CH
    QK = GLA_QK_DIM
    H = GLA_HEADS

    stf[...] = jnp.zeros(stf.shape, F32)
    stb[...] = jnp.zeros(stb.shape, F32)

    ri = lax.broadcasted_iota(jnp.int32, (CH, CH), 0)
    ci = lax.broadcasted_iota(jnp.int32, (CH, CH), 1)
    tri_lo = (ci <= ri).astype(F32)
    tri_up = (ci >= ri).astype(F32)
    rs = jnp.bitwise_and(lax.broadcasted_iota(jnp.int32, (H * CH, CH), 0), CH - 1)
    cs = lax.broadcasted_iota(jnp.int32, (H * CH, CH), 1)
    keep_lo = cs <= rs
    keep_up = cs >= rs
    q_head = _iota_div((H * CH, QK), 0, CH) == _iota_div((H * CH, QK), 1, GLA_DK)
    o_head = _iota_div((CH, GLA_V_DIM), 1, GLA_DV)
    st_keep = _iota_div((GLA_V_DIM, QK), 0, GLA_DV) == _iota_div((GLA_V_DIM, QK), 1, GLA_DK)
    nt_dims = (((1,), (1,)), ((), ()))
    tn_dims = (((0,), (0,)), ((), ()))

    def direction(c, col0, tri, keep, last, st_ref, o_scr):
        r0 = pl.multiple_of(c * CH, CH)
        a = la_ref[0, pl.ds(r0, CH), col0:col0 + QK]
        cum = jnp.dot(tri, a, precision=HIGHEST, preferred_element_type=F32)
        tot = cum[last:last + 1, :]
        mid = cum[CH // 2:CH // 2 + 1, :]
        q = qkv_ref[0, pl.ds(r0, CH), 0:QK].astype(F32)
        k = qkv_ref[0, pl.ds(r0, CH), QK:2 * QK].astype(F32)
        v = qkv_ref[0, pl.ds(r0, CH), 2 * QK:]
        state = st_ref[...]

        o_inter = lax.dot_general((q * jnp.exp(cum)).astype(BF16), state.astype(BF16), nt_dims,
                                  preferred_element_type=F32)
        qm = q * jnp.exp(cum - mid)
        km = (k * jnp.exp(mid - cum)).astype(BF16)
        q_stack = jnp.where(q_head, jnp.concatenate([qm] * H, axis=0), 0.0).astype(BF16)
        att = lax.dot_general(q_stack, km, nt_dims, preferred_element_type=F32)
        att = jnp.where(keep, att, 0.0).astype(BF16)
        o_all = jnp.dot(att, v, preferred_element_type=F32)
        o_intra = o_all[0:CH]
        for hd in range(1, H):
            o_intra = jnp.where(o_head == hd, o_all[hd * CH:(hd + 1) * CH], o_intra)
        o_scr[pl.ds(r0, CH), :] = o_inter + o_intra

        kl = (k * jnp.exp(tot - cum)).astype(BF16)
        kv_t = lax.dot_general(v, kl, tn_dims, preferred_element_type=F32)
        st_ref[...] = state * jnp.exp(tot) + jnp.where(st_keep, kv_t, 0.0)

    def step(s, carry):
        direction(s, 0, tri_lo, keep_lo, CH - 1, stf, of_s)
        cb = jnp.where(s < n_ctx_chunks, n_ctx_chunks - 1 - s, n_chunks - 1 - (s - n_ctx_chunks))
        direction(cb, QK, tri_up, keep_up, 0, stb, ob_s)
        return carry

    lax.fori_loop(0, n_chunks, step, 0)

    gmean = (_iota_div((GLA_V_DIM, GLA_V_DIM), 0, GLA_DV)
             == _iota_div((GLA_V_DIM, GLA_V_DIM), 1, GLA_DV)).astype(F32) * (1.0 / GLA_DV)
    gn = gn_ref[...]

    def finish(i, carry):
        r0 = pl.multiple_of(i * R, R)
        o = of_s[pl.ds(r0, R), :] + ob_s[pl.ds(r0, R), :]
        ms = jnp.dot(o * o, gmean, precision=HIGHEST, preferred_element_type=F32)
        o_ref[0, pl.ds(r0, R), :] = (o * lax.rsqrt(ms + EPS) * gn).astype(BF16)
        return carry

    lax.fori_loop(0, T // R, finish, 0)


def _gla(gqkv, la, gla_norm, n_ctx):
    B, T, _ = gqkv.shape
    kern = functools.partial(_gla_kernel, n_ctx=n_ctx)
    return pl.pallas_call(
        kern,
        grid=(B,),
        in_specs=[
            pl.BlockSpec((1, T, 2 * GLA_QK_DIM + GLA_V_DIM), lambda b: (b, 0, 0)),
            pl.BlockSpec((1, T, 2 * GLA_QK_DIM), lambda b: (b, 0, 0)),
            _resident((1, GLA_V_DIM)),
        ],
        out_specs=pl.BlockSpec((1, T, GLA_V_DIM), lambda b: (b, 0, 0)),
        out_shape=jax.ShapeDtypeStruct((B, T, GLA_V_DIM), BF16),
        scratch_shapes=[pltpu.VMEM((T, GLA_V_DIM), F32), pltpu.VMEM((T, GLA_V_DIM), F32),
                        pltpu.VMEM((GLA_V_DIM, GLA_QK_DIM), F32),
                        pltpu.VMEM((GLA_V_DIM, GLA_QK_DIM), F32)],
        compiler_params=_params(48, 1),
        name="gla",
    )(gqkv, la, gla_norm)


def _merge_kernel(x_ref, att_ref, dp_ref, on_ref, sgr_ref, mg_ref, mod_ref, gpost_ref,
                  wba_ref, pbd_ref, ps_ref, wbp_ref, wbg_ref, wo_ref, o_ref):
    D = D_MODEL
    ya = jnp.dot(att_ref[0], wba_ref[...], preferred_element_type=F32)
    yp = jnp.dot(dp_ref[0], pbd_ref[...], preferred_element_type=F32) * ps_ref[...]
    yp = jnp.dot(yp.astype(BF16), wbp_ref[...], preferred_element_type=F32)
    yg = jnp.dot(on_ref[0] * sgr_ref[0], wbg_ref[...], preferred_element_type=F32)
    m = (mg_ref[0, :, 0:D].astype(F32) * ya + mg_ref[0, :, D:2 * D].astype(F32) * yp
         + mg_ref[0, :, 2 * D:3 * D].astype(F32) * yg)
    y = jnp.dot(m.astype(BF16), wo_ref[...], preferred_element_type=F32)
    ms = jnp.mean(y * y, axis=-1, keepdims=True)
    gate = mod_ref[0][:, 2 * D:3 * D]
    o_ref[0] = x_ref[0] + gate * (y * lax.rsqrt(ms + EPS) * gpost_ref[...])


def _merge(xa, y_att, d_pool, on, sgr, mg, mod, g_post, wba, pbd, ps, wbp, wbg, wo,
           n_ctx_tiles, tile_off):
    B, T, D = xa.shape
    R = ROW_TILE
    nt = T // R
    ctx_row = mod.shape[0] - 1

    def tile(width):
        return pl.BlockSpec((1, R, width), lambda b, i: (b, i + tile_off, 0))

    return pl.pallas_call(
        _merge_kernel,
        grid=(B, nt - tile_off),
        in_specs=[
            tile(D), tile(ATT_DIM), tile(POOL_DIM), tile(GLA_V_DIM), tile(GLA_V_DIM), tile(3 * D),
            pl.BlockSpec((1, 1, 6 * D),
                         lambda b, i: (jnp.where(i + tile_off < n_ctx_tiles, ctx_row, b), 0, 0)),
            _resident((1, D)),
            _resident((ATT_DIM, D)), _resident((POOL_DIM, POOL_DIM)), _resident((1, POOL_DIM)),
            _resident((POOL_DIM, D)), _resident((GLA_V_DIM, D)), _resident((D, D)),
        ],
        out_specs=pl.BlockSpec((1, R, D), lambda b, i: (b, i, 0)),
        out_shape=jax.ShapeDtypeStruct((B, T - tile_off * R, D), F32),
        compiler_params=_params(48, 2),
        name="merge",
    )(xa, y_att, d_pool, on, sgr, mg, mod, g_post, wba, pbd, ps, wbp, wbg, wo)


def _ffn_kernel(x_ref, xp_ref, xn_ref, mod_ref, gpre_ref, wup_ref, cw_ref, cb_ref, wdn_ref,
                gpost_ref, o_ref, *, n_ctx_tiles, n_tiles):
    D = D_MODEL
    R = ROW_TILE
    i = pl.program_id(1)
    mod = mod_ref[0]
    x = x_ref[0]
    xe = jnp.concatenate([xp_ref[0], x, xn_ref[0]], axis=0)
    he = _modulated_norm(xe, gpre_ref[...], mod[:, 4 * D:5 * D], mod[:, 3 * D:4 * D]).astype(BF16)

    row = lax.broadcasted_iota(jnp.int32, (R, 1), 0)
    seq_start = (i == 0) | (i == n_ctx_tiles)
    seq_end = (i == n_ctx_tiles - 1) | (i == n_tiles - 1)
    keep_prev = jnp.logical_not((row == 0) & seq_start)
    keep_next = jnp.logical_not((row == R - 1) & seq_end)

    def conv(col0):
        u = jnp.dot(he, wup_ref[:, col0:col0 + FF_CHUNK], preferred_element_type=F32)
        prev = jnp.where(keep_prev, _shifted_rows(u, -1, R), 0.0)
        nxt = jnp.where(keep_next, _shifted_rows(u, 1, R), 0.0)
        cw = cw_ref[:, col0:col0 + FF_CHUNK]
        return (prev * cw[0:1] + u[SUBLANE:SUBLANE + R] * cw[1:2] + nxt * cw[2:3]
                + cb_ref[:, col0:col0 + FF_CHUNK])

    acc = jnp.zeros((R, D), F32)
    for j in range(D_FF // FF_CHUNK):
        a = conv(j * FF_CHUNK)
        g = conv(D_FF + j * FF_CHUNK)
        act = (a * (g * _sigmoid(g))).astype(BF16)
        acc = acc + jnp.dot(act, wdn_ref[j * FF_CHUNK:(j + 1) * FF_CHUNK, :],
                            preferred_element_type=F32)
    ms = jnp.mean(acc * acc, axis=-1, keepdims=True)
    gate = mod[:, 5 * D:6 * D]
    o_ref[0] = x + gate * (acc * lax.rsqrt(ms + EPS) * gpost_ref[...])


def _ffn(xa, mod, g_pre, w_up, conv_w, conv_b, w_down, g_post, n_ctx_tiles):
    B, T, D = xa.shape
    R = ROW_TILE
    H = SUBLANE
    nt = T // R
    ctx_row = mod.shape[0] - 1
    rpb = R // H
    kern = functools.partial(_ffn_kernel, n_ctx_tiles=n_ctx_tiles, n_tiles=nt)
    return pl.pallas_call(
        kern,
        grid=(B, nt),
        in_specs=[
            pl.BlockSpec((1, R, D), lambda b, i: (b, i, 0)),
            pl.BlockSpec((1, H, D), lambda b, i: (b, jnp.maximum(i * rpb - 1, 0), 0)),
            pl.BlockSpec((1, H, D), lambda b, i: (b, jnp.minimum((i + 1) * rpb, T // H - 1), 0)),
            pl.BlockSpec((1, 1, 6 * D), lambda b, i: (jnp.where(i < n_ctx_tiles, ctx_row, b), 0, 0)),
            _resident((1, D)),
            _resident((D, 2 * D_FF)), _resident((3, 2 * D_FF)), _resident((1, 2 * D_FF)),
            _resident((D_FF, D)), _resident((1, D)),
        ],
        out_specs=pl.BlockSpec((1, R, D), lambda b, i: (b, i, 0)),
        out_shape=jax.ShapeDtypeStruct((B, T, D), F32),
        compiler_params=_params(56, 2),
        name="ffn",
    )(xa, xa, xa, mod, g_pre, w_up, conv_w, conv_b, w_down, g_post)


def _rope_table(n_ctx, n_lat):
    rows = n_lat // GRID_W
    row = jnp.repeat(jnp.arange(rows, dtype=F32), GRID_W)
    col = jnp.tile(jnp.arange(GRID_W, dtype=F32), rows)
    n = HEAD_DIM // 4
    inv = ROPE_BASE ** (-jnp.arange(n, dtype=F32) / n)
    ang = jnp.concatenate([row[:, None] * inv, col[:, None] * inv], axis=-1)
    cos, sin = jnp.cos(ang), jnp.sin(ang)
    zero = jnp.zeros_like(sin)
    reps = LANE // HEAD_DIM
    cos_t = jnp.tile(jnp.concatenate([cos, cos], axis=-1), (1, reps))
    sin_lo = jnp.tile(jnp.concatenate([-sin, zero], axis=-1), (1, reps))
    sin_hi = jnp.tile(jnp.concatenate([zero, sin], axis=-1), (1, reps))
    lat = jnp.concatenate([cos_t, sin_lo, sin_hi], axis=-1)
    ctx = jnp.concatenate([jnp.ones((n_ctx, LANE), F32), jnp.zeros((n_ctx, 2 * LANE), F32)], axis=-1)
    return jnp.concatenate([ctx, lat], axis=0)


def kernel(x, c, ctx, c_ctx, w_ada, b_ada, g_pre_mix, g_post_mix, g_pre_ffn, g_post_ffn, w_in, att_sink, pool_w, pool_scale, gla_wa2, gla_ba, gla_norm, w_br_att, w_br_pool, w_br_gla, w_o, w_up, conv_w, conv_b, w_down):
    B, L, D = x.shape
    C = ctx.shape[1]
    depth = w_in.shape[0]
    R = ROW_TILE
    assert D == D_MODEL and C % R == 0 and L % R == 0 and L % GRID_W == 0
    n_ctx_tiles = C // R

    mod_rows = -(-(B + 1) // SUBLANE) * SUBLANE
    cc = jnp.zeros((mod_rows, D), F32).at[:B].set(c).at[mod_rows - 1].set(c_ctx)
    mods = _ada_table(cc, w_ada, b_ada).reshape(depth, mod_rows, 1, 6 * D)

    rope_tab = _rope_table(C, L)
    split = _O_GLR + 2 * GLA_GATE_RANK
    w_in_p = jnp.concatenate(
        [w_in[..., :split], jnp.zeros((depth, D, _GLR_PAD), w_in.dtype), w_in[..., split:]],
        axis=-1).astype(BF16)
    wa = jnp.zeros((depth, LANE, 2 * GLA_QK_DIM), F32)
    wa = wa.at[:, 0:GLA_GATE_RANK, 0:GLA_QK_DIM].set(gla_wa2[:, 0])
    wa = wa.at[:, GLA_GATE_RANK:2 * GLA_GATE_RANK, GLA_QK_DIM:].set(gla_wa2[:, 1])
    ba = gla_ba.reshape(depth, 1, 2 * GLA_QK_DIM)
    pbd = jnp.zeros((depth, POOL_DIM, POOL_DIM), F32)
    for g in range(len(POOL_WINDOWS)):
        sl = slice(g * POOL_GROUP_DIM, (g + 1) * POOL_GROUP_DIM)
        pbd = pbd.at[:, sl, sl].set(pool_w[:, g])
    pbd = pbd.astype(BF16)
    wba, wbp, wbg, wo = (w.astype(BF16) for w in (w_br_att, w_br_pool, w_br_gla, w_o))
    wup, wdn = w_up.astype(BF16), w_down.astype(BF16)

    xa = jnp.concatenate([ctx, x], axis=1)
    for l in range(depth):
        last = l == depth - 1
        off = n_ctx_tiles if last else 0
        mod = mods[l]
        q, kv, pu, gqkv, sgr, la, mg = _in_proj(
            xa, mod, g_pre_mix[l][None], w_in_p[l], rope_tab, wa[l], ba[l], n_ctx_tiles)
        y_att = _attention(q, kv, att_sink[l], C, off)
        d_pool = _pool(pu, C)
        on = _gla(gqkv, la, gla_norm[l].reshape(1, GLA_V_DIM), C)
        xa = _merge(xa, y_att, d_pool, on, sgr, mg, mod, g_post_mix[l][None], wba[l], pbd[l],
                    pool_scale[l][None], wbp[l], wbg[l], wo[l], n_ctx_tiles, off)
        xa = _ffn(xa, mod, g_pre_ffn[l][None], wup[l], conv_w[l], conv_b[l][None], wdn[l],
                  g_post_ffn[l][None], n_ctx_tiles - off)
    return xa
```

```python
import functools

import jax
import jax.numpy as jnp
import numpy as np
from jax import lax
from jax.experimental import pallas as pl
from jax.experimental.pallas import tpu as pltpu

F32 = jnp.float32
BF16 = jnp.bfloat16
HIGHEST = lax.Precision.HIGHEST

D_MODEL = 1024
GRID_W = 64
EPS = 1e-6

HEAD_DIM = 64
ATT_HEADS = 8
ATT_KV_HEADS = 2
ATT_GROUP = ATT_HEADS // ATT_KV_HEADS
WINDOW = 128
ROPE_BASE = 10000.0
ATT_DIM = ATT_HEADS * HEAD_DIM
KV_DIM = ATT_KV_HEADS * HEAD_DIM

POOL_WINDOWS = (2, 4, 8, 16)
POOL_GROUP_DIM = 64
POOL_DIM = len(POOL_WINDOWS) * POOL_GROUP_DIM
POOL_PAD = 16

GLA_HEADS = 4
GLA_DK = 32
GLA_DV = 64
GLA_GATE_RANK = 16
GLA_TAU = 16.0
GLA_QK_DIM = GLA_HEADS * GLA_DK
GLA_V_DIM = GLA_HEADS * GLA_DV
GLA_CHUNK = 64

D_FF = 2816
FF_CHUNK = 256
FF_GROUP = 4

LANE = 128
SUBLANE = 8
SUBLANE_LOG2 = SUBLANE.bit_length() - 1
ROW_TILE = 256

_GLR_PAD = LANE - 2 * GLA_GATE_RANK
_O_AQ = 0
_O_AK = _O_AQ + ATT_DIM
_O_AV = _O_AK + KV_DIM
_O_PU = _O_AV + KV_DIM
_O_GQ = _O_PU + POOL_DIM
_O_GK = _O_GQ + GLA_QK_DIM
_O_GV = _O_GK + GLA_QK_DIM
_O_GR = _O_GV + GLA_V_DIM
_O_GLR = _O_GR + GLA_V_DIM
_O_MG = _O_GLR + LANE
IN_DIM_PAD = _O_MG + 3 * D_MODEL

NEG_BIG = -1e30


def _params(vmem_mb, n_axes):
    return pltpu.CompilerParams(
        dimension_semantics=("arbitrary",) * n_axes,
        vmem_limit_bytes=vmem_mb * 1024 * 1024,
    )


def _resident(shape):
    nd = len(shape)
    return pl.BlockSpec(shape, lambda *_: (0,) * nd, pipeline_mode=pl.Buffered(1))


def _sigmoid(v):
    return 1.0 / (1.0 + jnp.exp(-v))


def _iota_div(shape, axis, divisor):
    shift = int(np.log2(divisor))
    assert 1 << shift == divisor
    return lax.shift_right_logical(lax.broadcasted_iota(jnp.int32, shape, axis), shift)


def _shifted_rows(ext, shift, n_rows):
    if shift:
        ext = pltpu.roll(ext, (-shift) % ext.shape[0], 0)
    return ext[SUBLANE:SUBLANE + n_rows]


def _modulated_norm(x, gain, scale, shift):
    ms = jnp.mean(x * x, axis=-1, keepdims=True)
    return (x * lax.rsqrt(ms + EPS) * gain) * (1.0 + scale) + shift


def _ada_kernel(c_ref, w_ref, b_ref, o_ref):
    c = c_ref[...]
    a = c * _sigmoid(c)
    o_ref[0] = jnp.dot(a, w_ref[0], precision=HIGHEST, preferred_element_type=F32) + b_ref[0]


def _ada_table(cc, w_ada, b_ada):
    depth = w_ada.shape[0]
    rows = cc.shape[0]
    n_col = w_ada.shape[2] // D_MODEL
    return pl.pallas_call(
        _ada_kernel,
        grid=(depth, n_col),
        in_specs=[
            pl.BlockSpec((rows, D_MODEL), lambda l, j: (0, 0)),
            pl.BlockSpec((1, D_MODEL, D_MODEL), lambda l, j: (l, 0, j)),
            pl.BlockSpec((1, 1, D_MODEL), lambda l, j: (l, 0, j)),
        ],
        out_specs=pl.BlockSpec((1, rows, D_MODEL), lambda l, j: (l, 0, j)),
        out_shape=jax.ShapeDtypeStruct((depth, rows, w_ada.shape[2]), F32),
        compiler_params=_params(32, 2),
        name="ada_table",
    )(cc, w_ada, b_ada.reshape(depth, 1, -1))


def _in_proj_kernel(x_ref, mod_ref, g_ref, w_ref, rope_ref, wa_ref, ba_ref,
                    q_ref, kv_ref, pu_ref, gqkv_ref, sgr_ref, la_ref, mg_ref):
    D = D_MODEL
    mod = mod_ref[0]
    h = _modulated_norm(x_ref[0], g_ref[...], mod[:, D:2 * D], mod[:, 0:D]).astype(BF16)

    def proj(lo, width):
        return jnp.dot(h, w_ref[:, lo:lo + width], preferred_element_type=F32)

    cosf = rope_ref[:, 0:LANE]
    sin_lo = rope_ref[:, LANE:2 * LANE]
    sin_hi = rope_ref[:, 2 * LANE:3 * LANE]

    def rope(t):
        return (t * cosf + pltpu.roll(t, LANE - HEAD_DIM // 2, 1) * sin_lo
                + pltpu.roll(t, HEAD_DIM // 2, 1) * sin_hi)

    aq = proj(_O_AQ, ATT_DIM)
    for j in range(ATT_DIM // LANE):
        q_ref[0, :, j * LANE:(j + 1) * LANE] = (
            rope(aq[:, j * LANE:(j + 1) * LANE]) * (HEAD_DIM ** -0.5)).astype(BF16)
    kv_ref[0, :, 0:KV_DIM] = rope(proj(_O_AK, KV_DIM)).astype(BF16)
    kv_ref[0, :, KV_DIM:2 * KV_DIM] = proj(_O_AV, KV_DIM).astype(BF16)

    pu_ref[0] = proj(_O_PU, POOL_DIM)

    gqkv_ref[0, :, 0:GLA_QK_DIM] = (proj(_O_GQ, GLA_QK_DIM) * (GLA_DK ** -0.5)).astype(BF16)
    gqkv_ref[0, :, GLA_QK_DIM:] = proj(_O_GK, GLA_QK_DIM + GLA_V_DIM).astype(BF16)
    gr = proj(_O_GR, GLA_V_DIM)
    sgr_ref[0] = (gr * _sigmoid(gr)).astype(BF16)

    z = jnp.dot(proj(_O_GLR, LANE), wa_ref[...], precision=HIGHEST,
                preferred_element_type=F32) + ba_ref[...]
    la_ref[0] = (jnp.minimum(z, 0.0) - jnp.log(1.0 + jnp.exp(-jnp.abs(z)))) * (1.0 / GLA_TAU)

    for j in range(3):
        mg_ref[0, :, j * D:(j + 1) * D] = _sigmoid(proj(_O_MG + j * D, D)).astype(BF16)


def _in_proj(xa, mod, g_pre, w_in, rope_tab, wa, ba, n_ctx_tiles):
    B, T, D = xa.shape
    R = ROW_TILE
    nt = T // R
    ctx_row = mod.shape[0] - 1

    def tile(width):
        return pl.BlockSpec((1, R, width), lambda b, i: (b, i, 0))

    widths = (ATT_DIM, 2 * KV_DIM, POOL_DIM, 2 * GLA_QK_DIM + GLA_V_DIM, GLA_V_DIM,
              2 * GLA_QK_DIM, 3 * D)
    dtypes = (BF16, BF16, F32, BF16, BF16, F32, BF16)
    return pl.pallas_call(
        _in_proj_kernel,
        grid=(B, nt),
        in_specs=[
            tile(D),
            pl.BlockSpec((1, 1, 6 * D), lambda b, i: (jnp.where(i < n_ctx_tiles, ctx_row, b), 0, 0)),
            _resident((1, D)),
            _resident((D, IN_DIM_PAD)),
            pl.BlockSpec((R, 3 * LANE), lambda b, i: (i, 0)),
            _resident((LANE, 2 * GLA_QK_DIM)),
            _resident((1, 2 * GLA_QK_DIM)),
        ],
        out_specs=[tile(w) for w in widths],
        out_shape=[jax.ShapeDtypeStruct((B, T, w), dt) for w, dt in zip(widths, dtypes)],
        compiler_params=_params(48, 2),
        name="in_proj",
    )(xa, mod, g_pre, w_in, rope_tab, wa, ba)


def _attn_kernel(sink_ref, q_ref, kv_ref, o_ref, *, tile_off, n_ctx_tiles, n_ctx, n_lat):
    R = ROW_TILE
    W = WINDOW
    n_loc = R + 2 * W
    i = pl.program_id(1) + tile_off
    n_blk = (n_ctx + n_lat) // W

    qi = lax.broadcasted_iota(jnp.int32, (R, n_loc), 0)
    kj = lax.broadcasted_iota(jnp.int32, (R, n_loc), 1)
    kpos = (i - n_ctx_tiles) * R - W + kj
    ok = (jnp.abs(kj - W - qi) <= W) & (kpos >= 0) & (kpos < n_lat) & (i >= n_ctx_tiles)
    bias_loc = jnp.where(ok, 0.0, NEG_BIG).astype(F32)

    first_blk = i * (R // W) - 1
    parts = []
    for blk in range(n_loc // W):
        idx = jnp.clip(first_blk + blk, 0, n_blk - 1)
        parts.append(kv_ref[0, pl.ds(pl.multiple_of(idx * W, W), W), :])
    parts.append(kv_ref[0, 0:n_ctx, :])
    kv = jnp.concatenate(parts, axis=0)

    nt_dims = (((1,), (1,)), ((), ()))
    q = q_ref[0]
    outs = []
    for g in range(ATT_KV_HEADS):
        k = kv[:, g * HEAD_DIM:(g + 1) * HEAD_DIM]
        v = kv[:, KV_DIM + g * HEAD_DIM:KV_DIM + (g + 1) * HEAD_DIM]
        for hh in range(ATT_GROUP):
            h = g * ATT_GROUP + hh
            qh = q[:, h * HEAD_DIM:(h + 1) * HEAD_DIM]
            s = lax.dot_general(qh, k, nt_dims, preferred_element_type=F32)
            s_loc = s[:, 0:n_loc] + bias_loc
            s_ctx = s[:, n_loc:]
            sink = sink_ref[h]
            m = jnp.maximum(jnp.maximum(jnp.max(s_loc, axis=-1, keepdims=True),
                                        jnp.max(s_ctx, axis=-1, keepdims=True)), sink)
            p_loc = jnp.exp(s_loc - m)
            p_ctx = jnp.exp(s_ctx - m)
            den = (jnp.sum(p_loc, axis=-1, keepdims=True) + jnp.sum(p_ctx, axis=-1, keepdims=True)
                   + jnp.exp(sink - m))
            o = (jnp.dot(p_loc.astype(BF16), v[0:n_loc], preferred_element_type=F32)
                 + jnp.dot(p_ctx.astype(BF16), v[n_loc:], preferred_element_type=F32))
            outs.append(o / den)
    o_ref[0] = jnp.concatenate(outs, axis=-1).astype(BF16)


def _attention(q, kv, sink, n_ctx, tile_off):
    B, T, _ = q.shape
    R = ROW_TILE
    nt = T // R
    kern = functools.partial(_attn_kernel, tile_off=tile_off, n_ctx_tiles=n_ctx // R,
                             n_ctx=n_ctx, n_lat=T - n_ctx)
    return pl.pallas_call(
        kern,
        grid=(B, nt - tile_off),
        in_specs=[
            pl.BlockSpec(memory_space=pltpu.SMEM),
            pl.BlockSpec((1, R, ATT_DIM), lambda b, i: (b, i + tile_off, 0)),
            pl.BlockSpec((1, T, 2 * KV_DIM), lambda b, i: (b, 0, 0)),
        ],
        out_specs=pl.BlockSpec((1, R, ATT_DIM), lambda b, i: (b, i + tile_off, 0)),
        out_shape=jax.ShapeDtypeStruct((B, T, ATT_DIM), BF16),
        compiler_params=_params(48, 2),
        name="attention",
    )(sink, q, kv)


def _pool_kernel(u_ref, o_ref, p0, p2, p4, p8, *, n_ctx, n_lat):
    R = ROW_TILE
    T = n_ctx + n_lat
    RE = R + 2 * SUBLANE
    off_ctx = POOL_PAD
    off_lat = 2 * POOL_PAD
    n_comp = p0.shape[0] // R - 1

    for buf in (p0, p2, p4, p8):
        buf[...] = jnp.zeros(buf.shape, F32)
    p0[off_ctx:off_ctx + n_ctx, :] = u_ref[0, 0:n_ctx, :]
    p0[n_ctx + off_lat:T + off_lat, :] = u_ref[0, n_ctx:T, :]

    def stage(src, dst, back, fwd):
        def body(c, carry):
            r = pl.multiple_of(c * R, R)
            ext = src[pl.ds(r, RE), :]
            dst[pl.ds(r + SUBLANE, R), :] = _shifted_rows(ext, -back, R) + _shifted_rows(ext, fwd, R)
            return carry
        lax.fori_loop(0, n_comp, body, 0)

    stage(p0, p2, 1, 0)
    stage(p2, p4, 1, 1)
    stage(p4, p8, 2, 2)

    lane_group = _iota_div((R, POOL_DIM), 1, POOL_GROUP_DIM)
    row = lax.broadcasted_iota(jnp.int32, (R, POOL_DIM), 0)

    def emit(i, carry):
        is_ctx = i < n_ctx // R
        r0 = pl.multiple_of(i * R, R)
        r = pl.multiple_of(r0 + jnp.where(is_ctx, off_ctx, off_lat), SUBLANE)
        pos = row + jnp.where(is_ctx, r0, r0 - n_ctx)
        seq_len = jnp.where(is_ctx, n_ctx, n_lat)
        ext8 = p8[pl.ds(r - SUBLANE, RE), :]
        sums = (p2[pl.ds(r, R), :], p4[pl.ds(r, R), :], ext8[SUBLANE:SUBLANE + R],
                _shifted_rows(ext8, -4, R) + _shifted_rows(ext8, 4, R))
        tot = sums[0]
        lo = jnp.full((R, POOL_DIM), POOL_WINDOWS[0] // 2, jnp.int32)
        hi = jnp.full((R, POOL_DIM), POOL_WINDOWS[0] - POOL_WINDOWS[0] // 2 - 1, jnp.int32)
        for g in range(1, len(POOL_WINDOWS)):
            w = POOL_WINDOWS[g]
            sel = lane_group == g
            tot = jnp.where(sel, sums[g], tot)
            lo = jnp.where(sel, w // 2, lo)
            hi = jnp.where(sel, w - w // 2 - 1, hi)
        cnt = (jnp.minimum(pos + hi + 1, seq_len) - jnp.maximum(pos - lo, 0)).astype(F32)
        o_ref[0, pl.ds(r0, R), :] = (tot / cnt - u_ref[0, pl.ds(r0, R), :]).astype(BF16)
        return carry

    lax.fori_loop(0, T // R, emit, 0)


def _pool(pu, n_ctx):
    B, T, _ = pu.shape
    R = ROW_TILE
    rows = (pl.cdiv(T + 3 * POOL_PAD, R) + 1) * R
    kern = functools.partial(_pool_kernel, n_ctx=n_ctx, n_lat=T - n_ctx)
    return pl.pallas_call(
        kern,
        grid=(B,),
        in_specs=[pl.BlockSpec((1, T, POOL_DIM), lambda b: (b, 0, 0))],
        out_specs=pl.BlockSpec((1, T, POOL_DIM), lambda b: (b, 0, 0)),
        out_shape=jax.ShapeDtypeStruct((B, T, POOL_DIM), BF16),
        scratch_shapes=[pltpu.VMEM((rows, POOL_DIM), F32)] * 4,
        compiler_params=_params(48, 1),
        name="pool",
    )(pu)


def _gla_kernel(qkv_ref, la_ref, gn_ref, o_ref, of_s, ob_s, stf, stb, *, n_ctx):
    CH = GLA_CHUNK
    R = ROW_TILE
    T = of_s.shape[0]
    n_chunks = T // CH
    n_ctx_chunks = n_ctx // CH
    QK = GLA_QK_DIM
    H = GLA_HEADS

    stf[...] = jnp.zeros(stf.shape, F32)
    stb[...] = jnp.zeros(stb.shape, F32)

    ri = lax.broadcasted_iota(jnp.int32, (CH, CH), 0)
    ci = lax.broadcasted_iota(jnp.int32, (CH, CH), 1)
    tri_lo = (ci <= ri).astype(F32)
    tri_up = (ci >= ri).astype(F32)
    rs = jnp.bitwise_and(lax.broadcasted_iota(jnp.int32, (H * CH, CH), 0), CH - 1)
    cs = lax.broadcasted_iota(jnp.int32, (H * CH, CH), 1)
    keep_lo = cs <= rs
    keep_up = cs >= rs
    q_head = _iota_div((H * CH, QK), 0, CH) == _iota_div((H * CH, QK), 1, GLA_DK)
    o_head = _iota_div((CH, GLA_V_DIM), 1, GLA_DV)
    st_keep = _iota_div((GLA_V_DIM, QK), 0, GLA_DV) == _iota_div((GLA_V_DIM, QK), 1, GLA_DK)
    nt_dims = (((1,), (1,)), ((), ()))
    tn_dims = (((0,), (0,)), ((), ()))

    def direction(c, col0, tri, keep, last, st_ref, o_scr):
        r0 = pl.multiple_of(c * CH, CH)
        a = la_ref[0, pl.ds(r0, CH), col0:col0 + QK]
        cum = jnp.dot(tri, a, precision=HIGHEST, preferred_element_type=F32)
        tot = cum[last:last + 1, :]
        mid = cum[CH // 2:CH // 2 + 1, :]
        q = qkv_ref[0, pl.ds(r0, CH), 0:QK].astype(F32)
        k = qkv_ref[0, pl.ds(r0, CH), QK:2 * QK].astype(F32)
        v = qkv_ref[0, pl.ds(r0, CH), 2 * QK:]
        state = st_ref[...]

        o_inter = lax.dot_general((q * jnp.exp(cum)).astype(BF16), state.astype(BF16), nt_dims,
                                  preferred_element_type=F32)
        qm = q * jnp.exp(cum - mid)
        km = (k * jnp.exp(mid - cum)).astype(BF16)
        q_stack = jnp.where(q_head, jnp.concatenate([qm] * H, axis=0), 0.0).astype(BF16)
        att = lax.dot_general(q_stack, km, nt_dims, preferred_element_type=F32)
        att = jnp.where(keep, att, 0.0).astype(BF16)
        o_all = jnp.dot(att, v, preferred_element_type=F32)
        o_intra = o_all[0:CH]
        for hd in range(1, H):
            o_intra = jnp.where(o_head == hd, o_all[hd * CH:(hd + 1) * CH], o_intra)
        o_scr[pl.ds(r0, CH), :] = o_inter + o_intra

        kl = (k * jnp.exp(tot - cum)).astype(BF16)
        kv_t = lax.dot_general(v, kl, tn_dims, preferred_element_type=F32)
        st_ref[...] = state * jnp.exp(tot) + jnp.where(st_keep, kv_t, 0.0)

    def step(s, carry):
        direction(s, 0, tri_lo, keep_lo, CH - 1, stf, of_s)
        cb = jnp.where(s < n_ctx_chunks, n_ctx_chunks - 1 - s, n_chunks - 1 - (s - n_ctx_chunks))
        direction(cb, QK, tri_up, keep_up, 0, stb, ob_s)
        return carry

    lax.fori_loop(0, n_chunks, step, 0)

    gmean = (_iota_div((GLA_V_DIM, GLA_V_DIM), 0, GLA_DV)
             == _iota_div((GLA_V_DIM, GLA_V_DIM), 1, GLA_DV)).astype(F32) * (1.0 / GLA_DV)
    gn = gn_ref[...]

    def finish(i, carry):
        r0 = pl.multiple_of(i * R, R)
        o = of_s[pl.ds(r0, R), :] + ob_s[pl.ds(r0, R), :]
        ms = jnp.dot(o * o, gmean, precision=HIGHEST, preferred_element_type=F32)
        o_ref[0, pl.ds(r0, R), :] = (o * lax.rsqrt(ms + EPS) * gn).astype(BF16)
        return carry

    lax.fori_loop(0, T // R, finish, 0)


def _gla(gqkv, la, gla_norm, n_ctx):
    B, T, _ = gqkv.shape
    kern = functools.partial(_gla_kernel, n_ctx=n_ctx)
    return pl.pallas_call(
        kern,
        grid=(B,),
        in_specs=[
            pl.BlockSpec((1, T, 2 * GLA_QK_DIM + GLA_V_DIM), lambda b: (b, 0, 0)),
            pl.BlockSpec((1, T, 2 * GLA_QK_DIM), lambda b: (b, 0, 0)),
            _resident((1, GLA_V_DIM)),
        ],
        out_specs=pl.BlockSpec((1, T, GLA_V_DIM), lambda b: (b, 0, 0)),
        out_shape=jax.ShapeDtypeStruct((B, T, GLA_V_DIM), BF16),
        scratch_shapes=[pltpu.VMEM((T, GLA_V_DIM), F32), pltpu.VMEM((T, GLA_V_DIM), F32),
                        pltpu.VMEM((GLA_V_DIM, GLA_QK_DIM), F32),
                        pltpu.VMEM((GLA_V_DIM, GLA_QK_DIM), F32)],
        compiler_params=_params(48, 1),
        name="gla",
    )(gqkv, la, gla_norm)


def _merge_kernel(x_ref, att_ref, dp_ref, on_ref, sgr_ref, mg_ref, mod_ref, gpost_ref,
                  wba_ref, pbd_ref, ps_ref, wbp_ref, wbg_ref, wo_ref, o_ref):
    D = D_MODEL
    ya = jnp.dot(att_ref[0], wba_ref[...], preferred_element_type=F32)
    yp = jnp.dot(dp_ref[0], pbd_ref[...], preferred_element_type=F32) * ps_ref[...]
    yp = jnp.dot(yp.astype(BF16), wbp_ref[...], preferred_element_type=F32)
    yg = jnp.dot(on_ref[0] * sgr_ref[0], wbg_ref[...], preferred_element_type=F32)
    m = (mg_ref[0, :, 0:D].astype(F32) * ya + mg_ref[0, :, D:2 * D].astype(F32) * yp
         + mg_ref[0, :, 2 * D:3 * D].astype(F32) * yg)
    y = jnp.dot(m.astype(BF16), wo_ref[...], preferred_element_type=F32)
    ms = jnp.mean(y * y, axis=-1, keepdims=True)
    gate = mod_ref[0][:, 2 * D:3 * D]
    o_ref[0] = x_ref[0] + gate * (y * lax.rsqrt(ms + EPS) * gpost_ref[...])


def _merge(xa, y_att, d_pool, on, sgr, mg, mod, g_post, wba, pbd, ps, wbp, wbg, wo,
           n_ctx_tiles, tile_off):
    B, T, D = xa.shape
    R = ROW_TILE
    nt = T // R
    ctx_row = mod.shape[0] - 1

    def tile(width):
        return pl.BlockSpec((1, R, width), lambda b, i: (b, i + tile_off, 0))

    return pl.pallas_call(
        _merge_kernel,
        grid=(B, nt - tile_off),
        in_specs=[
            tile(D), tile(ATT_DIM), tile(POOL_DIM), tile(GLA_V_DIM), tile(GLA_V_DIM), tile(3 * D),
            pl.BlockSpec((1, 1, 6 * D),
                         lambda b, i: (jnp.where(i + tile_off < n_ctx_tiles, ctx_row, b), 0, 0)),
            _resident((1, D)),
            _resident((ATT_DIM, D)), _resident((POOL_DIM, POOL_DIM)), _resident((1, POOL_DIM)),
            _resident((POOL_DIM, D)), _resident((GLA_V_DIM, D)), _resident((D, D)),
        ],
        out_specs=pl.BlockSpec((1, R, D), lambda b, i: (b, i, 0)),
        out_shape=jax.ShapeDtypeStruct((B, T - tile_off * R, D), F32),
        compiler_params=_params(48, 2),
        name="merge",
    )(xa, y_att, d_pool, on, sgr, mg, mod, g_post, wba, pbd, ps, wbp, wbg, wo)


def _ffn_kernel(x_ref, xp_ref, xn_ref, mod_ref, gpre_ref, wup_ref, cw_ref, cb_ref, wdn_ref,
                gpost_ref, o_ref, he_s, u0_s, u1_s, act_s, acc_s, *, n_ctx_tiles, n_tiles):
    D = D_MODEL
    R = x_ref.shape[1]
    S = SUBLANE
    G = R // S
    CK = FF_CHUNK
    n_chunks = wup_ref.shape[0]
    i = pl.program_id(1)
    mod = mod_ref[0]
    scale, shift = mod[:, 4 * D:5 * D], mod[:, 3 * D:4 * D]
    p_row = lax.broadcasted_iota(jnp.int32, (R, R), 0)
    seq_of_row = jnp.bitwise_and(p_row, S - 1) * G + lax.shift_right_logical(p_row, SUBLANE_LOG2)
    to_perm = (lax.broadcasted_iota(jnp.int32, (R, R), 1) == seq_of_row).astype(BF16)
    h = _modulated_norm(x_ref[0], gpre_ref[...], scale, shift).astype(BF16)
    he_s[0:R, :] = jnp.dot(to_perm, h, preferred_element_type=F32).astype(BF16)
    halo = jnp.concatenate([xp_ref[0, S - 1:S, :], xn_ref[0, 0:1, :], jnp.zeros((S - 2, D), F32)], axis=0)
    row = lax.broadcasted_iota(jnp.int32, (S, 1), 0)
    seq_start = (i == 0) | (i == n_ctx_tiles)
    seq_end = (i == n_ctx_tiles - 1) | (i == n_tiles - 1)
    outside = ((row == 0) & seq_start) | ((row == 1) & seq_end) | (row >= 2)
    he_s[R:R + S, :] = jnp.where(outside, 0.0, _modulated_norm(halo, gpre_ref[...], scale, shift)).astype(BF16)
    sub = lax.broadcasted_iota(jnp.int32, (S, 2 * CK), 0)

    def up(j, dst):
        dst[...] = jnp.dot(he_s[...], wup_ref[j], preferred_element_type=F32)

    def finish(j, src):
        u = src[0:R, :]
        edge = src[R:R + S, :]
        prev0 = jnp.where(sub == 0, edge[0:1], pltpu.roll(u[R - S:R], 1, 0))
        next_last = jnp.where(sub == S - 1, edge[1:2], pltpu.roll(u[0:S], S - 1, 0))
        prev = jnp.concatenate([prev0, u[0:R - S]], axis=0)
        nxt = jnp.concatenate([u[S:R], next_last], axis=0)
        cw = cw_ref[j]
        c = prev * cw[0:1] + u * cw[1:2] + nxt * cw[2:3] + cb_ref[j]
        g = c[:, CK:]
        act_s[:, j * CK:(j + 1) * CK] = (c[:, :CK] * (g * _sigmoid(g))).astype(BF16)

    def down(j0, j1):
        y = jnp.dot(act_s[:, j0 * CK:j1 * CK], wdn_ref[j0 * CK:j1 * CK, :], preferred_element_type=F32)
        if j0 == 0:
            acc_s[...] = y
        else:
            acc_s[...] += y

    slots = (u0_s, u1_s)
    up(0, slots[0])
    group_start = 0
    for j in range(n_chunks):
        if j + 1 < n_chunks:
            up(j + 1, slots[(j + 1) % 2])
        finish(j, slots[j % 2])
        if j + 1 - group_start == FF_GROUP or j + 1 == n_chunks:
            down(group_start, j + 1)
            group_start = j + 1

    acc = acc_s[...]
    acc_hi = acc.astype(BF16)
    acc_lo = (acc - acc_hi.astype(F32)).astype(BF16)
    t_row = lax.broadcasted_iota(jnp.int32, (R, R), 0)
    p_col = lax.broadcasted_iota(jnp.int32, (R, R), 1)
    to_seq = (t_row == jnp.bitwise_and(p_col, S - 1) * G + lax.shift_right_logical(p_col, SUBLANE_LOG2)).astype(BF16)
    y = (jnp.dot(to_seq, acc_hi, preferred_element_type=F32)
         + jnp.dot(to_seq, acc_lo, preferred_element_type=F32))
    ms = jnp.mean(y * y, axis=-1, keepdims=True)
    gate = mod[:, 5 * D:6 * D]
    o_ref[0] = x_ref[0] + gate * (y * lax.rsqrt(ms + EPS) * gpost_ref[...])


def _ffn(xa, mod, g_pre, w_up, conv_w, conv_b, w_down, g_post, n_ctx_tiles, row_tile):
    B, T, D = xa.shape
    R = row_tile
    H = SUBLANE
    nt = T // R
    ctx_row = mod.shape[0] - 1
    rpb = R // H
    n_chunks = w_up.shape[0]
    RE = R + H
    kern = functools.partial(_ffn_kernel, n_ctx_tiles=n_ctx_tiles, n_tiles=nt)
    return pl.pallas_call(
        kern,
        grid=(B, nt),
        in_specs=[
            pl.BlockSpec((1, R, D), lambda b, i: (b, i, 0)),
            pl.BlockSpec((1, H, D), lambda b, i: (b, jnp.maximum(i * rpb - 1, 0), 0)),
            pl.BlockSpec((1, H, D), lambda b, i: (b, jnp.minimum((i + 1) * rpb, T // H - 1), 0)),
            pl.BlockSpec((1, 1, 6 * D), lambda b, i: (jnp.where(i < n_ctx_tiles, ctx_row, b), 0, 0)),
            _resident((1, D)),
            _resident((n_chunks, D, 2 * FF_CHUNK)), _resident((n_chunks, 3, 2 * FF_CHUNK)),
            _resident((n_chunks, 1, 2 * FF_CHUNK)), _resident((D_FF, D)),
            _resident((1, D)),
        ],
        out_specs=pl.BlockSpec((1, R, D), lambda b, i: (b, i, 0)),
        out_shape=jax.ShapeDtypeStruct((B, T, D), F32),
        scratch_shapes=[pltpu.VMEM((RE, D), BF16), pltpu.VMEM((RE, 2 * FF_CHUNK), F32),
                        pltpu.VMEM((RE, 2 * FF_CHUNK), F32), pltpu.VMEM((R, D_FF), BF16),
                        pltpu.VMEM((R, D), F32)],
        compiler_params=_params(56, 2),
        name="ffn",
    )(xa, xa, xa, mod, g_pre, w_up, conv_w, conv_b, w_down, g_post)


def _rope_table(n_ctx, n_lat):
    rows = n_lat // GRID_W
    row = jnp.repeat(jnp.arange(rows, dtype=F32), GRID_W)
    col = jnp.tile(jnp.arange(GRID_W, dtype=F32), rows)
    n = HEAD_DIM // 4
    inv = ROPE_BASE ** (-jnp.arange(n, dtype=F32) / n)
    ang = jnp.concatenate([row[:, None] * inv, col[:, None] * inv], axis=-1)
    cos, sin = jnp.cos(ang), jnp.sin(ang)
    zero = jnp.zeros_like(sin)
    reps = LANE // HEAD_DIM
    cos_t = jnp.tile(jnp.concatenate([cos, cos], axis=-1), (1, reps))
    sin_lo = jnp.tile(jnp.concatenate([-sin, zero], axis=-1), (1, reps))
    sin_hi = jnp.tile(jnp.concatenate([zero, sin], axis=-1), (1, reps))
    lat = jnp.concatenate([cos_t, sin_lo, sin_hi], axis=-1)
    ctx = jnp.concatenate([jnp.ones((n_ctx, LANE), F32), jnp.zeros((n_ctx, 2 * LANE), F32)], axis=-1)
    return jnp.concatenate([ctx, lat], axis=0)


def kernel(x, c, ctx, c_ctx, w_ada, b_ada, g_pre_mix, g_post_mix, g_pre_ffn, g_post_ffn, w_in, att_sink, pool_w, pool_scale, gla_wa2, gla_ba, gla_norm, w_br_att, w_br_pool, w_br_gla, w_o, w_up, conv_w, conv_b, w_down):
    B, L, D = x.shape
    C = ctx.shape[1]
    depth = w_in.shape[0]
    R = ROW_TILE
    assert D == D_MODEL and C % R == 0 and L % R == 0 and L % GRID_W == 0
    n_ctx_tiles = C // R

    mod_rows = -(-(B + 1) // SUBLANE) * SUBLANE
    cc = jnp.zeros((mod_rows, D), F32).at[:B].set(c).at[mod_rows - 1].set(c_ctx)
    mods = _ada_table(cc, w_ada, b_ada).reshape(depth, mod_rows, 1, 6 * D)

    rope_tab = _rope_table(C, L)
    split = _O_GLR + 2 * GLA_GATE_RANK
    w_in_p = jnp.concatenate(
        [w_in[..., :split], jnp.zeros((depth, D, _GLR_PAD), w_in.dtype), w_in[..., split:]],
        axis=-1).astype(BF16)
    wa = jnp.zeros((depth, LANE, 2 * GLA_QK_DIM), F32)
    wa = wa.at[:, 0:GLA_GATE_RANK, 0:GLA_QK_DIM].set(gla_wa2[:, 0])
    wa = wa.at[:, GLA_GATE_RANK:2 * GLA_GATE_RANK, GLA_QK_DIM:].set(gla_wa2[:, 1])
    ba = gla_ba.reshape(depth, 1, 2 * GLA_QK_DIM)
    pbd = jnp.zeros((depth, POOL_DIM, POOL_DIM), F32)
    for g in range(len(POOL_WINDOWS)):
        sl = slice(g * POOL_GROUP_DIM, (g + 1) * POOL_GROUP_DIM)
        pbd = pbd.at[:, sl, sl].set(pool_w[:, g])
    pbd = pbd.astype(BF16)
    wba, wbp, wbg, wo = (w.astype(BF16) for w in (w_br_att, w_br_pool, w_br_gla, w_o))

    def ff_chunks(t):
        lead = t.shape[:-1]
        t = t.reshape(lead + (2, D_FF // FF_CHUNK, FF_CHUNK))
        t = jnp.moveaxis(t, -2, 1)
        return t.reshape(t.shape[:-2] + (2 * FF_CHUNK,))

    wup = ff_chunks(w_up.astype(BF16))
    cw = ff_chunks(conv_w)
    cb = ff_chunks(conv_b[:, None, :])
    wdn = w_down.astype(BF16)

    xa = jnp.concatenate([ctx, x], axis=1)
    for l in range(depth):
        last = l == depth - 1
        off = n_ctx_tiles if last else 0
        mod = mods[l]
        q, kv, pu, gqkv, sgr, la, mg = _in_proj(
            xa, mod, g_pre_mix[l][None], w_in_p[l], rope_tab, wa[l], ba[l], n_ctx_tiles)
        y_att = _attention(q, kv, att_sink[l], C, off)
        d_pool = _pool(pu, C)
        on = _gla(gqkv, la, gla_norm[l].reshape(1, GLA_V_DIM), C)
        xa = _merge(xa, y_att, d_pool, on, sgr, mg, mod, g_post_mix[l][None], wba[l], pbd[l],
                    pool_scale[l][None], wbp[l], wbg[l], wo[l], n_ctx_tiles, off)
        xa = _ffn(xa, mod, g_pre_ffn[l][None], wup[l], cw[l], cb[l], wdn[l],
                  g_post_ffn[l][None], n_ctx_tiles - off, 2 * R if last else R)
    return xa
```

```python
import functools

import jax
import jax.numpy as jnp
import numpy as np
from jax import lax
from jax.experimental import pallas as pl
from jax.experimental.pallas import tpu as pltpu

F32 = jnp.float32
BF16 = jnp.bfloat16
HIGHEST = lax.Precision.HIGHEST

D_MODEL = 1024
GRID_W = 64
EPS = 1e-6

HEAD_DIM = 64
ATT_HEADS = 8
ATT_KV_HEADS = 2
ATT_GROUP = ATT_HEADS // ATT_KV_HEADS
WINDOW = 128
ROPE_BASE = 10000.0
ATT_DIM = ATT_HEADS * HEAD_DIM
KV_DIM = ATT_KV_HEADS * HEAD_DIM

POOL_WINDOWS = (2, 4, 8, 16)
POOL_GROUP_DIM = 64
POOL_DIM = len(POOL_WINDOWS) * POOL_GROUP_DIM
POOL_PAD = 16

GLA_HEADS = 4
GLA_DK = 32
GLA_DV = 64
GLA_GATE_RANK = 16
GLA_TAU = 16.0
GLA_QK_DIM = GLA_HEADS * GLA_DK
GLA_V_DIM = GLA_HEADS * GLA_DV
GLA_CHUNK = 64

D_FF = 2816
FF_CHUNK = 256
FF_GROUP = 4

LANE = 128
SUBLANE = 8
SUBLANE_LOG2 = SUBLANE.bit_length() - 1
ROW_TILE = 256

_GLR_PAD = LANE - 2 * GLA_GATE_RANK
_O_AQ = 0
_O_AK = _O_AQ + ATT_DIM
_O_AV = _O_AK + KV_DIM
_O_PU = _O_AV + KV_DIM
_O_GQ = _O_PU + POOL_DIM
_O_GK = _O_GQ + GLA_QK_DIM
_O_GV = _O_GK + GLA_QK_DIM
_O_GR = _O_GV + GLA_V_DIM
_O_GLR = _O_GR + GLA_V_DIM
_O_MG = _O_GLR + LANE
IN_DIM_PAD = _O_MG + 3 * D_MODEL

NEG_BIG = -1e30
LOG2E = float(np.log2(np.e))


def _params(vmem_mb, n_axes):
    return pltpu.CompilerParams(
        dimension_semantics=("arbitrary",) * n_axes,
        vmem_limit_bytes=vmem_mb * 1024 * 1024,
    )


def _resident(shape):
    nd = len(shape)
    return pl.BlockSpec(shape, lambda *_: (0,) * nd, pipeline_mode=pl.Buffered(1))


def _sigmoid(v):
    return 1.0 / (1.0 + jnp.exp(-v))


def _iota_div(shape, axis, divisor):
    shift = int(np.log2(divisor))
    assert 1 << shift == divisor
    return lax.shift_right_logical(lax.broadcasted_iota(jnp.int32, shape, axis), shift)


def _shifted_rows(ext, shift, n_rows):
    if shift:
        ext = pltpu.roll(ext, (-shift) % ext.shape[0], 0)
    return ext[SUBLANE:SUBLANE + n_rows]


def _modulated_norm(x, gain, scale, shift):
    ms = jnp.mean(x * x, axis=-1, keepdims=True)
    return (x * lax.rsqrt(ms + EPS) * gain) * (1.0 + scale) + shift


def _ada_kernel(c_ref, w_ref, b_ref, o_ref):
    c = c_ref[...]
    a = c * _sigmoid(c)
    o_ref[0] = jnp.dot(a, w_ref[0], precision=HIGHEST, preferred_element_type=F32) + b_ref[0]


def _ada_table(cc, w_ada, b_ada):
    depth = w_ada.shape[0]
    rows = cc.shape[0]
    n_col = w_ada.shape[2] // D_MODEL
    return pl.pallas_call(
        _ada_kernel,
        grid=(depth, n_col),
        in_specs=[
            pl.BlockSpec((rows, D_MODEL), lambda l, j: (0, 0)),
            pl.BlockSpec((1, D_MODEL, D_MODEL), lambda l, j: (l, 0, j)),
            pl.BlockSpec((1, 1, D_MODEL), lambda l, j: (l, 0, j)),
        ],
        out_specs=pl.BlockSpec((1, rows, D_MODEL), lambda l, j: (l, 0, j)),
        out_shape=jax.ShapeDtypeStruct((depth, rows, w_ada.shape[2]), F32),
        compiler_params=_params(32, 2),
        name="ada_table",
    )(cc, w_ada, b_ada.reshape(depth, 1, -1))


def _in_proj_kernel(x_ref, mod_ref, g_ref, w_ref, rope_ref, wa_ref, ba_ref,
                    q_ref, kv_ref, pu_ref, gqkv_ref, sgr_ref, la_ref, mg_ref):
    D = D_MODEL
    mod = mod_ref[0]
    h = _modulated_norm(x_ref[0], g_ref[...], mod[:, D:2 * D], mod[:, 0:D]).astype(BF16)

    def proj(lo, width):
        return jnp.dot(h, w_ref[:, lo:lo + width], preferred_element_type=F32)

    cosf = rope_ref[:, 0:LANE]
    sin_lo = rope_ref[:, LANE:2 * LANE]
    sin_hi = rope_ref[:, 2 * LANE:3 * LANE]

    def rope(t):
        return (t * cosf + pltpu.roll(t, LANE - HEAD_DIM // 2, 1) * sin_lo
                + pltpu.roll(t, HEAD_DIM // 2, 1) * sin_hi)

    aq = proj(_O_AQ, ATT_DIM)
    for j in range(ATT_DIM // LANE):
        q_ref[0, :, j * LANE:(j + 1) * LANE] = (
            rope(aq[:, j * LANE:(j + 1) * LANE]) * (HEAD_DIM ** -0.5 * LOG2E)).astype(BF16)
    kv_ref[0, :, 0:KV_DIM] = rope(proj(_O_AK, KV_DIM)).astype(BF16)
    kv_ref[0, :, KV_DIM:2 * KV_DIM] = proj(_O_AV, KV_DIM).astype(BF16)

    pu_ref[0] = proj(_O_PU, POOL_DIM)

    gqkv_ref[0, :, 0:GLA_QK_DIM] = (proj(_O_GQ, GLA_QK_DIM) * (GLA_DK ** -0.5)).astype(BF16)
    gqkv_ref[0, :, GLA_QK_DIM:] = proj(_O_GK, GLA_QK_DIM + GLA_V_DIM).astype(BF16)
    gr = proj(_O_GR, GLA_V_DIM)
    sgr_ref[0] = (gr * _sigmoid(gr)).astype(BF16)

    z = jnp.dot(proj(_O_GLR, LANE), wa_ref[...], precision=HIGHEST,
                preferred_element_type=F32) + ba_ref[...]
    la_ref[0] = (jnp.minimum(z, 0.0) - jnp.log(1.0 + jnp.exp(-jnp.abs(z)))) * (1.0 / GLA_TAU)

    for j in range(3):
        mg_ref[0, :, j * D:(j + 1) * D] = _sigmoid(proj(_O_MG + j * D, D)).astype(BF16)


def _in_proj(xa, mod, g_pre, w_in, rope_tab, wa, ba, n_ctx_tiles):
    B, T, D = xa.shape
    R = ROW_TILE
    nt = T // R
    ctx_row = mod.shape[0] - 1

    def tile(width):
        return pl.BlockSpec((1, R, width), lambda b, i: (b, i, 0))

    widths = (ATT_DIM, 2 * KV_DIM, POOL_DIM, 2 * GLA_QK_DIM + GLA_V_DIM, GLA_V_DIM,
              2 * GLA_QK_DIM, 3 * D)
    dtypes = (BF16, BF16, F32, BF16, BF16, F32, BF16)
    return pl.pallas_call(
        _in_proj_kernel,
        grid=(B, nt),
        in_specs=[
            tile(D),
            pl.BlockSpec((1, 1, 6 * D), lambda b, i: (jnp.where(i < n_ctx_tiles, ctx_row, b), 0, 0)),
            _resident((1, D)),
            _resident((D, IN_DIM_PAD)),
            pl.BlockSpec((R, 3 * LANE), lambda b, i: (i, 0)),
            _resident((LANE, 2 * GLA_QK_DIM)),
            _resident((1, 2 * GLA_QK_DIM)),
        ],
        out_specs=[tile(w) for w in widths],
        out_shape=[jax.ShapeDtypeStruct((B, T, w), dt) for w, dt in zip(widths, dtypes)],
        compiler_params=_params(48, 2),
        name="in_proj",
    )(xa, mod, g_pre, w_in, rope_tab, wa, ba)


def _attn_kernel(sink_ref, q_ref, kv_ref, o_ref, s_s, *, tile_off, n_ctx_tiles, n_ctx, n_lat):
    R = ROW_TILE
    W = WINDOW
    assert R == 2 * W == 2 * LANE
    n_win = R // W + 2
    i = pl.program_id(1) + tile_off
    n_blk = (n_ctx + n_lat) // W
    nt_dims = (((1,), (1,)), ((), ()))
    q = q_ref[0]

    def heads(kv, split, merge):
        n_keys = kv.shape[0]
        ks, v_ext_ts = [], []
        for g in range(ATT_KV_HEADS):
            ks.append(kv[:, g * HEAD_DIM:(g + 1) * HEAD_DIM])
            v = kv[:, KV_DIM + g * HEAD_DIM:KV_DIM + (g + 1) * HEAD_DIM].astype(F32)
            v_ext_ts.append(
                jnp.concatenate([v, jnp.ones((n_keys, LANE - HEAD_DIM), F32)], axis=1).T.astype(BF16))

        def scores(h):
            qh = q[:, h * HEAD_DIM:(h + 1) * HEAD_DIM]
            s_s[h % 2, 0:n_keys, :] = lax.dot_general(ks[h // ATT_GROUP], qh, nt_dims,
                                                      preferred_element_type=F32)

        scores(0)
        out_t = []
        for h in range(ATT_HEADS):
            if h + 1 < ATT_HEADS:
                scores(h + 1)
            sink = sink_ref[h] * LOG2E
            ms, ps = [], []
            for s_half in split(s_s.at[h % 2]):
                m = jnp.maximum(jnp.max(s_half, axis=0, keepdims=True), sink)
                ms.append(m)
                ps.append(jnp.exp2(s_half - m).astype(BF16))
            o_ext = jnp.dot(v_ext_ts[h // ATT_GROUP], merge(ps), preferred_element_type=F32)
            den = o_ext[HEAD_DIM:HEAD_DIM + 1] + jnp.exp2(sink - jnp.concatenate(ms, axis=1))
            out_t.append(o_ext[0:HEAD_DIM] / den)
            if h % 2 == 1:
                pair = jnp.concatenate(out_t, axis=0).T
                o_ref[0, :, (h - 1) * HEAD_DIM:(h + 1) * HEAD_DIM] = pair.astype(BF16)
                out_t = []

    @pl.when(i < n_ctx_tiles)
    def _context_queries():
        heads(kv_ref[0, 0:n_ctx, :],
              lambda ref: [ref[0:n_ctx, a * LANE:(a + 1) * LANE] for a in range(R // LANE)],
              lambda ps: jnp.concatenate(ps, axis=1))

    @pl.when(i >= n_ctx_tiles)
    def _latent_queries():
        first_blk = i * (R // W) - 1
        parts = []
        for blk in range(n_win):
            idx = jnp.clip(first_blk + blk, 0, n_blk - 1)
            parts.append(kv_ref[0, pl.ds(pl.multiple_of(idx * W, W), W), :])
        parts.append(kv_ref[0, 0:n_ctx, :])
        n_loc = n_win * W
        kj = lax.broadcasted_iota(jnp.int32, (W, LANE), 0)
        qi = lax.broadcasted_iota(jnp.int32, (W, LANE), 1)
        below = jnp.where(kj >= qi, 0.0, NEG_BIG).astype(F32)
        above = jnp.where(kj <= qi, 0.0, NEG_BIG).astype(F32)
        below_first = below + jnp.where(i == n_ctx_tiles, NEG_BIG, 0.0)
        above_last = above + jnp.where(i == n_blk // (R // W) - 1, NEG_BIG, 0.0)

        def split(ref):
            def blk(b, a):
                return ref[b * W:(b + 1) * W, a * LANE:(a + 1) * LANE]
            return [jnp.concatenate([blk(0, 0) + below_first, blk(1, 0), blk(2, 0) + above,
                                     ref[n_loc:n_loc + n_ctx, 0:LANE]], axis=0),
                    jnp.concatenate([blk(1, 1) + below, blk(2, 1), blk(3, 1) + above_last,
                                     ref[n_loc:n_loc + n_ctx, LANE:2 * LANE]], axis=0)]

        def merge(ps):
            zero = jnp.zeros((W, LANE), BF16)
            live = (n_win - 1) * W
            return jnp.concatenate(
                [jnp.concatenate([ps[0][0:live], zero, ps[0][live:]], axis=0),
                 jnp.concatenate([zero, ps[1]], axis=0)], axis=1)

        heads(jnp.concatenate(parts, axis=0), split, merge)


def _attention(q, kv, sink, n_ctx, tile_off):
    B, T, _ = q.shape
    R = ROW_TILE
    nt = T // R
    kern = functools.partial(_attn_kernel, tile_off=tile_off, n_ctx_tiles=n_ctx // R,
                             n_ctx=n_ctx, n_lat=T - n_ctx)
    return pl.pallas_call(
        kern,
        grid=(B, nt - tile_off),
        in_specs=[
            pl.BlockSpec(memory_space=pltpu.SMEM),
            pl.BlockSpec((1, R, ATT_DIM), lambda b, i: (b, i + tile_off, 0)),
            pl.BlockSpec((1, T, 2 * KV_DIM), lambda b, i: (b, 0, 0)),
        ],
        out_specs=pl.BlockSpec((1, R, ATT_DIM), lambda b, i: (b, i + tile_off, 0)),
        out_shape=jax.ShapeDtypeStruct((B, T, ATT_DIM), BF16),
        scratch_shapes=[pltpu.VMEM((2, R + 2 * WINDOW + n_ctx, R), F32)],
        compiler_params=_params(48, 2),
        name="attention",
    )(sink, q, kv)


def _pool_kernel(u_ref, o_ref, p0, p2, p4, p8, *, n_ctx, n_lat):
    R = ROW_TILE
    T = n_ctx + n_lat
    RE = R + 2 * SUBLANE
    off_ctx = POOL_PAD
    off_lat = 2 * POOL_PAD
    n_comp = p0.shape[0] // R - 1

    for buf in (p0, p2, p4, p8):
        buf[...] = jnp.zeros(buf.shape, F32)
    p0[off_ctx:off_ctx + n_ctx, :] = u_ref[0, 0:n_ctx, :]
    p0[n_ctx + off_lat:T + off_lat, :] = u_ref[0, n_ctx:T, :]

    def stage(src, dst, back, fwd):
        def body(c, carry):
            r = pl.multiple_of(c * R, R)
            ext = src[pl.ds(r, RE), :]
            dst[pl.ds(r + SUBLANE, R), :] = _shifted_rows(ext, -back, R) + _shifted_rows(ext, fwd, R)
            return carry
        lax.fori_loop(0, n_comp, body, 0)

    stage(p0, p2, 1, 0)
    stage(p2, p4, 1, 1)
    stage(p4, p8, 2, 2)

    lane_group = _iota_div((R, POOL_DIM), 1, POOL_GROUP_DIM)
    row = lax.broadcasted_iota(jnp.int32, (R, POOL_DIM), 0)

    def emit(i, carry):
        is_ctx = i < n_ctx // R
        r0 = pl.multiple_of(i * R, R)
        r = pl.multiple_of(r0 + jnp.where(is_ctx, off_ctx, off_lat), SUBLANE)
        pos = row + jnp.where(is_ctx, r0, r0 - n_ctx)
        seq_len = jnp.where(is_ctx, n_ctx, n_lat)
        ext8 = p8[pl.ds(r - SUBLANE, RE), :]
        sums = (p2[pl.ds(r, R), :], p4[pl.ds(r, R), :], ext8[SUBLANE:SUBLANE + R],
                _shifted_rows(ext8, -4, R) + _shifted_rows(ext8, 4, R))
        tot = sums[0]
        lo = jnp.full((R, POOL_DIM), POOL_WINDOWS[0] // 2, jnp.int32)
        hi = jnp.full((R, POOL_DIM), POOL_WINDOWS[0] - POOL_WINDOWS[0] // 2 - 1, jnp.int32)
        for g in range(1, len(POOL_WINDOWS)):
            w = POOL_WINDOWS[g]
            sel = lane_group == g
            tot = jnp.where(sel, sums[g], tot)
            lo = jnp.where(sel, w // 2, lo)
            hi = jnp.where(sel, w - w // 2 - 1, hi)
        cnt = (jnp.minimum(pos + hi + 1, seq_len) - jnp.maximum(pos - lo, 0)).astype(F32)
        o_ref[0, pl.ds(r0, R), :] = (tot / cnt - u_ref[0, pl.ds(r0, R), :]).astype(BF16)
        return carry

    lax.fori_loop(0, T // R, emit, 0)


def _pool(pu, n_ctx):
    B, T, _ = pu.shape
    R = ROW_TILE
    rows = (pl.cdiv(T + 3 * POOL_PAD, R) + 1) * R
    kern = functools.partial(_pool_kernel, n_ctx=n_ctx, n_lat=T - n_ctx)
    return pl.pallas_call(
        kern,
        grid=(B,),
        in_specs=[pl.BlockSpec((1, T, POOL_DIM), lambda b: (b, 0, 0))],
        out_specs=pl.BlockSpec((1, T, POOL_DIM), lambda b: (b, 0, 0)),
        out_shape=jax.ShapeDtypeStruct((B, T, POOL_DIM), BF16),
        scratch_shapes=[pltpu.VMEM((rows, POOL_DIM), F32)] * 4,
        compiler_params=_params(48, 1),
        name="pool",
    )(pu)


def _gla_kernel(qkv_ref, la_ref, gn_ref, o_ref, of_s, ob_s, stf, stb, *, n_ctx):
    CH = GLA_CHUNK
    R = ROW_TILE
    NC = R // CH
    T = of_s.shape[0]
    n_tiles = T // R
    n_ctx_tiles = n_ctx // R
    QK = GLA_QK_DIM
    DV = GLA_V_DIM
    H = GLA_HEADS

    stf[...] = jnp.zeros(stf.shape, F32)
    stb[...] = jnp.zeros(stb.shape, F32)

    same_chunk = _iota_div((R, R), 0, CH) == _iota_div((R, R), 1, CH)
    ri = lax.broadcasted_iota(jnp.int32, (R, R), 0)
    ci = lax.broadcasted_iota(jnp.int32, (R, R), 1)
    tri_lo = (same_chunk & (ci <= ri)).astype(BF16)
    tri_up = (same_chunk & (ci >= ri)).astype(BF16)
    rs = jnp.bitwise_and(lax.broadcasted_iota(jnp.int32, (H * R, R), 0), R - 1)
    cs = lax.broadcasted_iota(jnp.int32, (H * R, R), 1)
    same_chunk_h = (lax.shift_right_logical(rs, CH.bit_length() - 1)
                    == lax.shift_right_logical(cs, CH.bit_length() - 1))
    keep_lo = same_chunk_h & (cs <= rs)
    keep_up = same_chunk_h & (cs >= rs)
    q_head = _iota_div((H * R, QK), 0, R) == _iota_div((H * R, QK), 1, GLA_DK)
    o_head = _iota_div((R, DV), 1, GLA_DV)
    st_keep = _iota_div((DV, QK), 0, GLA_DV) == _iota_div((DV, QK), 1, GLA_DK)
    nt_dims = (((1,), (1,)), ((), ()))
    tn_dims = (((0,), (0,)), ((), ()))

    def split3(a):
        a1 = a.astype(BF16)
        r1 = a - a1.astype(F32)
        a2 = r1.astype(BF16)
        a3 = (r1 - a2.astype(F32)).astype(BF16)
        return jnp.concatenate([a1, a2, a3], axis=1)

    def rows_of_chunks(cum, offset):
        picks = [cum[c * CH + offset:c * CH + offset + 1, :] for c in range(NC)]
        full = jnp.concatenate([jnp.broadcast_to(p, (CH, QK)) for p in picks], axis=0)
        return picks, full

    def load(t, col0, tri):
        r0 = pl.multiple_of(t * R, R)
        a = la_ref[0, pl.ds(r0, R), col0:col0 + QK]
        c3 = jnp.dot(tri, split3(a), preferred_element_type=F32)
        return dict(r0=r0, cum=c3[:, 0:QK] + c3[:, QK:2 * QK] + c3[:, 2 * QK:3 * QK],
                    q=qkv_ref[0, pl.ds(r0, R), 0:QK].astype(F32),
                    k=qkv_ref[0, pl.ds(r0, R), QK:2 * QK].astype(F32),
                    v=qkv_ref[0, pl.ds(r0, R), 2 * QK:])

    def scale(d, last):
        cum, q, k = d["cum"], d["q"], d["k"]
        tots, tot_b = rows_of_chunks(cum, last)
        _, mid_b = rows_of_chunks(cum, CH // 2)
        d["dec"] = [jnp.exp(t) for t in tots]
        d["qe"] = (q * jnp.exp(cum)).astype(BF16)
        qm = q * jnp.exp(cum - mid_b)
        d["qs"] = jnp.where(q_head, jnp.concatenate([qm] * H, axis=0), 0.0).astype(BF16)
        d["km"] = (k * jnp.exp(mid_b - cum)).astype(BF16)
        d["kl"] = (k * jnp.exp(tot_b - cum)).astype(BF16)

    def scores(d, keep):
        att = lax.dot_general(d["qs"], d["km"], nt_dims, preferred_element_type=F32)
        d["att"] = jnp.where(keep, att, 0.0).astype(BF16)
        d["kv"] = [lax.dot_general(d["v"][c * CH:(c + 1) * CH], d["kl"][c * CH:(c + 1) * CH], tn_dims,
                                   preferred_element_type=F32) for c in range(NC)]

    def intra(d):
        o_all = jnp.dot(d["att"], d["v"], preferred_element_type=F32)
        o = o_all[0:R]
        for hd in range(1, H):
            o = jnp.where(o_head == hd, o_all[hd * R:(hd + 1) * R], o)
        d["o"] = o

    def inter(d, order, st_ref, o_scr):
        state = st_ref[...]
        parts = [None] * NC
        for c in order:
            parts[c] = lax.dot_general(d["qe"][c * CH:(c + 1) * CH], state.astype(BF16), nt_dims,
                                       preferred_element_type=F32)
            state = state * d["dec"][c] + jnp.where(st_keep, d["kv"][c], 0.0)
        st_ref[...] = state
        o_scr[pl.ds(d["r0"], R), :] = d["o"] + jnp.concatenate(parts, axis=0)

    def step(s, carry):
        tb = jnp.where(s < n_ctx_tiles, n_ctx_tiles - 1 - s, n_tiles - 1 - (s - n_ctx_tiles))
        f = load(s, 0, tri_lo)
        b = load(tb, QK, tri_up)
        scale(f, CH - 1)
        scale(b, 0)
        scores(f, keep_lo)
        scores(b, keep_up)
        intra(f)
        intra(b)
        inter(f, range(NC), stf, of_s)
        inter(b, range(NC - 1, -1, -1), stb, ob_s)
        return carry

    lax.fori_loop(0, n_tiles, step, 0)

    gmean = (_iota_div((DV, DV), 0, GLA_DV) == _iota_div((DV, DV), 1, GLA_DV)).astype(BF16) * (1.0 / GLA_DV)
    gn = gn_ref[...]

    def finish(i, carry):
        r0 = pl.multiple_of(i * R, R)
        o = of_s[pl.ds(r0, R), :] + ob_s[pl.ds(r0, R), :]
        oo = o * o
        hi = oo.astype(BF16)
        lo = (oo - hi.astype(F32)).astype(BF16)
        ms = jnp.dot(hi, gmean, preferred_element_type=F32) + jnp.dot(lo, gmean, preferred_element_type=F32)
        o_ref[0, pl.ds(r0, R), :] = (o * lax.rsqrt(ms + EPS) * gn).astype(BF16)
        return carry

    lax.fori_loop(0, n_tiles, finish, 0)


def _gla(gqkv, la, gla_norm, n_ctx):
    B, T, _ = gqkv.shape
    kern = functools.partial(_gla_kernel, n_ctx=n_ctx)
    return pl.pallas_call(
        kern,
        grid=(B,),
        in_specs=[
            pl.BlockSpec((1, T, 2 * GLA_QK_DIM + GLA_V_DIM), lambda b: (b, 0, 0)),
            pl.BlockSpec((1, T, 2 * GLA_QK_DIM), lambda b: (b, 0, 0)),
            _resident((1, GLA_V_DIM)),
        ],
        out_specs=pl.BlockSpec((1, T, GLA_V_DIM), lambda b: (b, 0, 0)),
        out_shape=jax.ShapeDtypeStruct((B, T, GLA_V_DIM), BF16),
        scratch_shapes=[pltpu.VMEM((T, GLA_V_DIM), F32), pltpu.VMEM((T, GLA_V_DIM), F32),
                        pltpu.VMEM((GLA_V_DIM, GLA_QK_DIM), F32),
                        pltpu.VMEM((GLA_V_DIM, GLA_QK_DIM), F32)],
        compiler_params=_params(48, 1),
        name="gla",
    )(gqkv, la, gla_norm)


def _merge_kernel(x_ref, att_ref, dp_ref, on_ref, sgr_ref, mg_ref, mod_ref, gpost_ref,
                  wba_ref, pbd_ref, ps_ref, wbp_ref, wbg_ref, wo_ref, o_ref):
    D = D_MODEL
    ya = jnp.dot(att_ref[0], wba_ref[...], preferred_element_type=F32)
    yp = jnp.dot(dp_ref[0], pbd_ref[...], preferred_element_type=F32) * ps_ref[...]
    yp = jnp.dot(yp.astype(BF16), wbp_ref[...], preferred_element_type=F32)
    yg = jnp.dot(on_ref[0] * sgr_ref[0], wbg_ref[...], preferred_element_type=F32)
    m = (mg_ref[0, :, 0:D].astype(F32) * ya + mg_ref[0, :, D:2 * D].astype(F32) * yp
         + mg_ref[0, :, 2 * D:3 * D].astype(F32) * yg)
    y = jnp.dot(m.astype(BF16), wo_ref[...], preferred_element_type=F32)
    ms = jnp.mean(y * y, axis=-1, keepdims=True)
    gate = mod_ref[0][:, 2 * D:3 * D]
    o_ref[0] = x_ref[0] + gate * (y * lax.rsqrt(ms + EPS) * gpost_ref[...])


def _merge(xa, y_att, d_pool, on, sgr, mg, mod, g_post, wba, pbd, ps, wbp, wbg, wo,
           n_ctx_tiles, tile_off):
    B, T, D = xa.shape
    R = ROW_TILE
    nt = T // R
    ctx_row = mod.shape[0] - 1

    def tile(width):
        return pl.BlockSpec((1, R, width), lambda b, i: (b, i + tile_off, 0))

    return pl.pallas_call(
        _merge_kernel,
        grid=(B, nt - tile_off),
        in_specs=[
            tile(D), tile(ATT_DIM), tile(POOL_DIM), tile(GLA_V_DIM), tile(GLA_V_DIM), tile(3 * D),
            pl.BlockSpec((1, 1, 6 * D),
                         lambda b, i: (jnp.where(i + tile_off < n_ctx_tiles, ctx_row, b), 0, 0)),
            _resident((1, D)),
            _resident((ATT_DIM, D)), _resident((POOL_DIM, POOL_DIM)), _resident((1, POOL_DIM)),
            _resident((POOL_DIM, D)), _resident((GLA_V_DIM, D)), _resident((D, D)),
        ],
        out_specs=pl.BlockSpec((1, R, D), lambda b, i: (b, i, 0)),
        out_shape=jax.ShapeDtypeStruct((B, T - tile_off * R, D), F32),
        compiler_params=_params(48, 2),
        name="merge",
    )(xa, y_att, d_pool, on, sgr, mg, mod, g_post, wba, pbd, ps, wbp, wbg, wo)


def _ffn_kernel(x_ref, xp_ref, xn_ref, mod_ref, gpre_ref, wup_ref, cw_ref, cb_ref, wdn_ref,
                gpost_ref, o_ref, he_s, u0_s, u1_s, act_s, acc_s, *, n_ctx_tiles, n_tiles):
    D = D_MODEL
    R = x_ref.shape[1]
    S = SUBLANE
    G = R // S
    CK = FF_CHUNK
    n_chunks = wup_ref.shape[0]
    i = pl.program_id(1)
    mod = mod_ref[0]
    scale, shift = mod[:, 4 * D:5 * D], mod[:, 3 * D:4 * D]
    p_row = lax.broadcasted_iota(jnp.int32, (R, R), 0)
    seq_of_row = jnp.bitwise_and(p_row, S - 1) * G + lax.shift_right_logical(p_row, SUBLANE_LOG2)
    to_perm = (lax.broadcasted_iota(jnp.int32, (R, R), 1) == seq_of_row).astype(BF16)
    h = _modulated_norm(x_ref[0], gpre_ref[...], scale, shift).astype(BF16)
    he_s[0:R, :] = jnp.dot(to_perm, h, preferred_element_type=F32).astype(BF16)
    halo = jnp.concatenate([xp_ref[0, S - 1:S, :], xn_ref[0, 0:1, :], jnp.zeros((S - 2, D), F32)], axis=0)
    row = lax.broadcasted_iota(jnp.int32, (S, 1), 0)
    seq_start = (i == 0) | (i == n_ctx_tiles)
    seq_end = (i == n_ctx_tiles - 1) | (i == n_tiles - 1)
    outside = ((row == 0) & seq_start) | ((row == 1) & seq_end) | (row >= 2)
    he_s[R:R + S, :] = jnp.where(outside, 0.0, _modulated_norm(halo, gpre_ref[...], scale, shift)).astype(BF16)
    sub = lax.broadcasted_iota(jnp.int32, (S, 2 * CK), 0)

    def up(j, dst):
        dst[...] = jnp.dot(he_s[...], wup_ref[j], preferred_element_type=F32)

    def finish(j, src):
        u = src[0:R, :]
        edge = src[R:R + S, :]
        prev0 = jnp.where(sub == 0, edge[0:1], pltpu.roll(u[R - S:R], 1, 0))
        next_last = jnp.where(sub == S - 1, edge[1:2], pltpu.roll(u[0:S], S - 1, 0))
        prev = jnp.concatenate([prev0, u[0:R - S]], axis=0)
        nxt = jnp.concatenate([u[S:R], next_last], axis=0)
        cw = cw_ref[j]
        c = prev * cw[0:1] + u * cw[1:2] + nxt * cw[2:3] + cb_ref[j]
        g = c[:, CK:]
        act_s[:, j * CK:(j + 1) * CK] = (c[:, :CK] * (g * _sigmoid(g))).astype(BF16)

    def down(j0, j1):
        y = jnp.dot(act_s[:, j0 * CK:j1 * CK], wdn_ref[j0 * CK:j1 * CK, :], preferred_element_type=F32)
        if j0 == 0:
            acc_s[...] = y
        else:
            acc_s[...] += y

    slots = (u0_s, u1_s)
    up(0, slots[0])
    group_start = 0
    for j in range(n_chunks):
        if j + 1 < n_chunks:
            up(j + 1, slots[(j + 1) % 2])
        finish(j, slots[j % 2])
        if j + 1 - group_start == FF_GROUP or j + 1 == n_chunks:
            down(group_start, j + 1)
            group_start = j + 1

    acc = acc_s[...]
    acc_hi = acc.astype(BF16)
    acc_lo = (acc - acc_hi.astype(F32)).astype(BF16)
    t_row = lax.broadcasted_iota(jnp.int32, (R, R), 0)
    p_col = lax.broadcasted_iota(jnp.int32, (R, R), 1)
    to_seq = (t_row == jnp.bitwise_and(p_col, S - 1) * G + lax.shift_right_logical(p_col, SUBLANE_LOG2)).astype(BF16)
    y = (jnp.dot(to_seq, acc_hi, preferred_element_type=F32)
         + jnp.dot(to_seq, acc_lo, preferred_element_type=F32))
    ms = jnp.mean(y * y, axis=-1, keepdims=True)
    gate = mod[:, 5 * D:6 * D]
    o_ref[0] = x_ref[0] + gate * (y * lax.rsqrt(ms + EPS) * gpost_ref[...])


def _ffn(xa, mod, g_pre, w_up, conv_w, conv_b, w_down, g_post, n_ctx_tiles, row_tile):
    B, T, D = xa.shape
    R = row_tile
    H = SUBLANE
    nt = T // R
    ctx_row = mod.shape[0] - 1
    rpb = R // H
    n_chunks = w_up.shape[0]
    RE = R + H
    kern = functools.partial(_ffn_kernel, n_ctx_tiles=n_ctx_tiles, n_tiles=nt)
    return pl.pallas_call(
        kern,
        grid=(B, nt),
        in_specs=[
            pl.BlockSpec((1, R, D), lambda b, i: (b, i, 0)),
            pl.BlockSpec((1, H, D), lambda b, i: (b, jnp.maximum(i * rpb - 1, 0), 0)),
            pl.BlockSpec((1, H, D), lambda b, i: (b, jnp.minimum((i + 1) * rpb, T // H - 1), 0)),
            pl.BlockSpec((1, 1, 6 * D), lambda b, i: (jnp.where(i < n_ctx_tiles, ctx_row, b), 0, 0)),
            _resident((1, D)),
            _resident((n_chunks, D, 2 * FF_CHUNK)), _resident((n_chunks, 3, 2 * FF_CHUNK)),
            _resident((n_chunks, 1, 2 * FF_CHUNK)), _resident((D_FF, D)),
            _resident((1, D)),
        ],
        out_specs=pl.BlockSpec((1, R, D), lambda b, i: (b, i, 0)),
        out_shape=jax.ShapeDtypeStruct((B, T, D), F32),
        scratch_shapes=[pltpu.VMEM((RE, D), BF16), pltpu.VMEM((RE, 2 * FF_CHUNK), F32),
                        pltpu.VMEM((RE, 2 * FF_CHUNK), F32), pltpu.VMEM((R, D_FF), BF16),
                        pltpu.VMEM((R, D), F32)],
        compiler_params=_params(56, 2),
        name="ffn",
    )(xa, xa, xa, mod, g_pre, w_up, conv_w, conv_b, w_down, g_post)


def _rope_table(n_ctx, n_lat):
    rows = n_lat // GRID_W
    row = jnp.repeat(jnp.arange(rows, dtype=F32), GRID_W)
    col = jnp.tile(jnp.arange(GRID_W, dtype=F32), rows)
    n = HEAD_DIM // 4
    inv = ROPE_BASE ** (-jnp.arange(n, dtype=F32) / n)
    ang = jnp.concatenate([row[:, None] * inv, col[:, None] * inv], axis=-1)
    cos, sin = jnp.cos(ang), jnp.sin(ang)
    zero = jnp.zeros_like(sin)
    reps = LANE // HEAD_DIM
    cos_t = jnp.tile(jnp.concatenate([cos, cos], axis=-1), (1, reps))
    sin_lo = jnp.tile(jnp.concatenate([-sin, zero], axis=-1), (1, reps))
    sin_hi = jnp.tile(jnp.concatenate([zero, sin], axis=-1), (1, reps))
    lat = jnp.concatenate([cos_t, sin_lo, sin_hi], axis=-1)
    ctx = jnp.concatenate([jnp.ones((n_ctx, LANE), F32), jnp.zeros((n_ctx, 2 * LANE), F32)], axis=-1)
    return jnp.concatenate([ctx, lat], axis=0)


def kernel(x, c, ctx, c_ctx, w_ada, b_ada, g_pre_mix, g_post_mix, g_pre_ffn, g_post_ffn, w_in, att_sink, pool_w, pool_scale, gla_wa2, gla_ba, gla_norm, w_br_att, w_br_pool, w_br_gla, w_o, w_up, conv_w, conv_b, w_down):
    B, L, D = x.shape
    C = ctx.shape[1]
    depth = w_in.shape[0]
    R = ROW_TILE
    assert D == D_MODEL and C % R == 0 and L % R == 0 and L % GRID_W == 0
    n_ctx_tiles = C // R

    mod_rows = -(-(B + 1) // SUBLANE) * SUBLANE
    cc = jnp.zeros((mod_rows, D), F32).at[:B].set(c).at[mod_rows - 1].set(c_ctx)
    mods = _ada_table(cc, w_ada, b_ada).reshape(depth, mod_rows, 1, 6 * D)

    rope_tab = _rope_table(C, L)
    split = _O_GLR + 2 * GLA_GATE_RANK
    w_in_p = jnp.concatenate(
        [w_in[..., :split], jnp.zeros((depth, D, _GLR_PAD), w_in.dtype), w_in[..., split:]],
        axis=-1).astype(BF16)
    wa = jnp.zeros((depth, LANE, 2 * GLA_QK_DIM), F32)
    wa = wa.at[:, 0:GLA_GATE_RANK, 0:GLA_QK_DIM].set(gla_wa2[:, 0])
    wa = wa.at[:, GLA_GATE_RANK:2 * GLA_GATE_RANK, GLA_QK_DIM:].set(gla_wa2[:, 1])
    ba = gla_ba.reshape(depth, 1, 2 * GLA_QK_DIM)
    pbd = jnp.zeros((depth, POOL_DIM, POOL_DIM), F32)
    for g in range(len(POOL_WINDOWS)):
        sl = slice(g * POOL_GROUP_DIM, (g + 1) * POOL_GROUP_DIM)
        pbd = pbd.at[:, sl, sl].set(pool_w[:, g])
    pbd = pbd.astype(BF16)
    wba, wbp, wbg, wo = (w.astype(BF16) for w in (w_br_att, w_br_pool, w_br_gla, w_o))

    def ff_chunks(t):
        lead = t.shape[:-1]
        t = t.reshape(lead + (2, D_FF // FF_CHUNK, FF_CHUNK))
        t = jnp.moveaxis(t, -2, 1)
        return t.reshape(t.shape[:-2] + (2 * FF_CHUNK,))

    wup = ff_chunks(w_up.astype(BF16))
    cw = ff_chunks(conv_w)
    cb = ff_chunks(conv_b[:, None, :])
    wdn = w_down.astype(BF16)

    xa = jnp.concatenate([ctx, x], axis=1)
    for l in range(depth):
        last = l == depth - 1
        off = n_ctx_tiles if last else 0
        mod = mods[l]
        q, kv, pu, gqkv, sgr, la, mg = _in_proj(
            xa, mod, g_pre_mix[l][None], w_in_p[l], rope_tab, wa[l], ba[l], n_ctx_tiles)
        y_att = _attention(q, kv, att_sink[l], C, off)
        d_pool = _pool(pu, C)
        on = _gla(gqkv, la, gla_norm[l].reshape(1, GLA_V_DIM), C)
        xa = _merge(xa, y_att, d_pool, on, sgr, mg, mod, g_post_mix[l][None], wba[l], pbd[l],
                    pool_scale[l][None], wbp[l], wbg[l], wo[l], n_ctx_tiles, off)
        xa = _ffn(xa, mod, g_pre_ffn[l][None], wup[l], cw[l], cb[l], wdn[l],
                  g_post_ffn[l][None], n_ctx_tiles - off, 2 * R if last else R)
    return xa
```

```python
import functools

import jax
import jax.numpy as jnp
import numpy as np
from jax import lax
from jax.experimental import pallas as pl
from jax.experimental.pallas import tpu as pltpu

F32 = jnp.float32
BF16 = jnp.bfloat16
HIGHEST = lax.Precision.HIGHEST

D_MODEL = 1024
GRID_W = 64
EPS = 1e-6

HEAD_DIM = 64
ATT_HEADS = 8
ATT_KV_HEADS = 2
ATT_GROUP = ATT_HEADS // ATT_KV_HEADS
WINDOW = 128
ROPE_BASE = 10000.0
ATT_DIM = ATT_HEADS * HEAD_DIM
KV_DIM = ATT_KV_HEADS * HEAD_DIM

POOL_WINDOWS = (2, 4, 8, 16)
POOL_GROUP_DIM = 64
POOL_DIM = len(POOL_WINDOWS) * POOL_GROUP_DIM
POOL_PAD = 16

GLA_HEADS = 4
GLA_DK = 32
GLA_DV = 64
GLA_GATE_RANK = 16
GLA_TAU = 16.0
GLA_QK_DIM = GLA_HEADS * GLA_DK
GLA_V_DIM = GLA_HEADS * GLA_DV
GLA_CHUNK = 64

D_FF = 2816
FF_CHUNK = 256
FF_GROUP = 4

LANE = 128
SUBLANE = 8
SUBLANE_LOG2 = SUBLANE.bit_length() - 1
ROW_TILE = 256

_GLR_PAD = LANE - 2 * GLA_GATE_RANK
_O_AQ = 0
_O_AK = _O_AQ + ATT_DIM
_O_AV = _O_AK + KV_DIM
_O_PU = _O_AV + KV_DIM
_O_GQ = _O_PU + POOL_DIM
_O_GK = _O_GQ + GLA_QK_DIM
_O_GV = _O_GK + GLA_QK_DIM
_O_GR = _O_GV + GLA_V_DIM
_O_GLR = _O_GR + GLA_V_DIM
IN_DIM_PAD = _O_GLR + LANE

NEG_BIG = -1e30
LOG2E = float(np.log2(np.e))


def _params(vmem_mb, n_axes):
    return pltpu.CompilerParams(
        dimension_semantics=("arbitrary",) * n_axes,
        vmem_limit_bytes=vmem_mb * 1024 * 1024,
    )


def _resident(shape):
    nd = len(shape)
    return pl.BlockSpec(shape, lambda *_: (0,) * nd, pipeline_mode=pl.Buffered(1))


def _layer_spec(shape, layer):
    nd = len(shape)
    return pl.BlockSpec((None,) + tuple(shape), lambda *_: (layer,) + (0,) * nd,
                        pipeline_mode=pl.Buffered(1))


def _mod_spec(layer, ctx_row, is_ctx_tile):
    return pl.BlockSpec((None, 1, 1, 6 * D_MODEL),
                        lambda b, i: (layer, jnp.where(is_ctx_tile(i), ctx_row, b), 0, 0))


def _sigmoid(v):
    return 1.0 / (1.0 + jnp.exp(-v))


def _iota_div(shape, axis, divisor):
    shift = int(np.log2(divisor))
    assert 1 << shift == divisor
    return lax.shift_right_logical(lax.broadcasted_iota(jnp.int32, shape, axis), shift)


def _shifted_rows(ext, shift, n_rows):
    if shift:
        ext = pltpu.roll(ext, (-shift) % ext.shape[0], 0)
    return ext[SUBLANE:SUBLANE + n_rows]


def _modulated_norm(x, gain, scale, shift):
    ms = jnp.mean(x * x, axis=-1, keepdims=True)
    return (x * lax.rsqrt(ms + EPS) * gain) * (1.0 + scale) + shift


def _ada_kernel(c_ref, w_ref, b_ref, o_ref):
    c = c_ref[...]
    a = c * _sigmoid(c)
    o_ref[0] = jnp.dot(a, w_ref[0], precision=HIGHEST, preferred_element_type=F32) + b_ref[0]


def _ada_table(cc, w_ada, b_ada):
    depth = w_ada.shape[0]
    rows = cc.shape[0]
    n_col = w_ada.shape[2] // D_MODEL
    return pl.pallas_call(
        _ada_kernel,
        grid=(depth, n_col),
        in_specs=[
            pl.BlockSpec((rows, D_MODEL), lambda l, j: (0, 0)),
            pl.BlockSpec((1, D_MODEL, D_MODEL), lambda l, j: (l, 0, j)),
            pl.BlockSpec((1, 1, D_MODEL), lambda l, j: (l, 0, j)),
        ],
        out_specs=pl.BlockSpec((1, rows, D_MODEL), lambda l, j: (l, 0, j)),
        out_shape=jax.ShapeDtypeStruct((depth, rows, w_ada.shape[2]), F32),
        compiler_params=_params(32, 2),
        name="ada_table",
    )(cc, w_ada, b_ada.reshape(depth, 1, -1))


def _in_proj_kernel(x_ref, mod_ref, g_ref, w_ref, rope_ref, wa_ref, ba_ref,
                    q_ref, kv_ref, pu_ref, gqkv_ref, sgr_ref, la_ref, h_ref):
    D = D_MODEL
    mod = mod_ref[0]
    h = _modulated_norm(x_ref[0], g_ref[...], mod[:, D:2 * D], mod[:, 0:D]).astype(BF16)
    h_ref[0] = h

    def proj(lo, width):
        return jnp.dot(h, w_ref[:, lo:lo + width], preferred_element_type=F32)

    cosf = rope_ref[:, 0:LANE]
    sin_lo = rope_ref[:, LANE:2 * LANE]
    sin_hi = rope_ref[:, 2 * LANE:3 * LANE]

    def rope(t):
        return (t * cosf + pltpu.roll(t, LANE - HEAD_DIM // 2, 1) * sin_lo
                + pltpu.roll(t, HEAD_DIM // 2, 1) * sin_hi)

    aq = proj(_O_AQ, ATT_DIM)
    for j in range(ATT_DIM // LANE):
        q_ref[0, :, j * LANE:(j + 1) * LANE] = (
            rope(aq[:, j * LANE:(j + 1) * LANE]) * (HEAD_DIM ** -0.5 * LOG2E)).astype(BF16)
    kv_ref[0, :, 0:KV_DIM] = rope(proj(_O_AK, KV_DIM)).astype(BF16)
    kv_ref[0, :, KV_DIM:2 * KV_DIM] = proj(_O_AV, KV_DIM).astype(BF16)

    pu_ref[0] = proj(_O_PU, POOL_DIM)

    gqkv_ref[0, :, 0:GLA_QK_DIM] = (proj(_O_GQ, GLA_QK_DIM) * (GLA_DK ** -0.5)).astype(BF16)
    gqkv_ref[0, :, GLA_QK_DIM:] = proj(_O_GK, GLA_QK_DIM + GLA_V_DIM).astype(BF16)
    gr = proj(_O_GR, GLA_V_DIM)
    sgr_ref[0] = (gr * _sigmoid(gr)).astype(BF16)

    glr = proj(_O_GLR, LANE)
    glr_hi = glr.astype(BF16)
    glr_lo = (glr - glr_hi.astype(F32)).astype(BF16)
    z = jnp.dot(jnp.concatenate([glr_hi, glr_lo, glr_hi], axis=1), wa_ref[...],
                preferred_element_type=F32) + ba_ref[...]
    la_ref[0] = (jnp.minimum(z, 0.0) - jnp.log(1.0 + jnp.exp(-jnp.abs(z)))) * (1.0 / GLA_TAU)


def _in_proj(xa, mods, g_pre, w_in, rope_tab, wa, ba, n_ctx_tiles, layer):
    B, T, D = xa.shape
    R = ROW_TILE
    nt = T // R
    ctx_row = mods.shape[1] - 1

    def tile(width):
        return pl.BlockSpec((1, R, width), lambda b, i: (b, i, 0))

    widths = (ATT_DIM, 2 * KV_DIM, POOL_DIM, 2 * GLA_QK_DIM + GLA_V_DIM, GLA_V_DIM,
              2 * GLA_QK_DIM, D)
    dtypes = (BF16, BF16, F32, BF16, BF16, F32, BF16)
    return pl.pallas_call(
        _in_proj_kernel,
        grid=(B, nt),
        in_specs=[
            tile(D),
            _mod_spec(layer, ctx_row, lambda i: i < n_ctx_tiles),
            _layer_spec((1, D), layer),
            _layer_spec((D, IN_DIM_PAD), layer),
            pl.BlockSpec((R, 3 * LANE), lambda b, i: (i, 0)),
            _layer_spec((3 * LANE, 2 * GLA_QK_DIM), layer),
            _layer_spec((1, 2 * GLA_QK_DIM), layer),
        ],
        out_specs=[tile(w) for w in widths],
        out_shape=[jax.ShapeDtypeStruct((B, T, w), dt) for w, dt in zip(widths, dtypes)],
        compiler_params=_params(48, 2),
        name="in_proj",
    )(xa, mods, g_pre, w_in, rope_tab, wa, ba)


def _attn_kernel(sink_ref, q_ref, kv_ref, o_ref, s_s, *, layer, tile_off, n_ctx_tiles, n_ctx, n_lat):
    R = ROW_TILE
    W = WINDOW
    assert R == 2 * W == 2 * LANE
    n_win = R // W + 2
    i = pl.program_id(1) + tile_off
    n_blk = (n_ctx + n_lat) // W
    nt_dims = (((1,), (1,)), ((), ()))
    q = q_ref[0]

    def heads(kv, split, merge):
        n_keys = kv.shape[0]
        ks, v_ext_ts = [], []
        for g in range(ATT_KV_HEADS):
            ks.append(kv[:, g * HEAD_DIM:(g + 1) * HEAD_DIM])
            v = kv[:, KV_DIM + g * HEAD_DIM:KV_DIM + (g + 1) * HEAD_DIM].astype(F32)
            v_ext_ts.append(
                jnp.concatenate([v, jnp.ones((n_keys, LANE - HEAD_DIM), F32)], axis=1).T.astype(BF16))

        def scores(h):
            qh = q[:, h * HEAD_DIM:(h + 1) * HEAD_DIM]
            s_s[h % 2, 0:n_keys, :] = lax.dot_general(ks[h // ATT_GROUP], qh, nt_dims,
                                                      preferred_element_type=F32)

        scores(0)
        out_t = []
        for h in range(ATT_HEADS):
            if h + 1 < ATT_HEADS:
                scores(h + 1)
            sink = sink_ref[layer, h] * LOG2E
            ms, ps = [], []
            for s_half in split(s_s.at[h % 2]):
                m = jnp.maximum(jnp.max(s_half, axis=0, keepdims=True), sink)
                ms.append(m)
                ps.append(jnp.exp2(s_half - m).astype(BF16))
            o_ext = jnp.dot(v_ext_ts[h // ATT_GROUP], merge(ps), preferred_element_type=F32)
            den = o_ext[HEAD_DIM:HEAD_DIM + 1] + jnp.exp2(sink - jnp.concatenate(ms, axis=1))
            out_t.append(o_ext[0:HEAD_DIM] / den)
            if h % 2 == 1:
                pair = jnp.concatenate(out_t, axis=0).T
                o_ref[0, :, (h - 1) * HEAD_DIM:(h + 1) * HEAD_DIM] = pair.astype(BF16)
                out_t = []

    @pl.when(i < n_ctx_tiles)
    def _context_queries():
        heads(kv_ref[0, 0:n_ctx, :],
              lambda ref: [ref[0:n_ctx, a * LANE:(a + 1) * LANE] for a in range(R // LANE)],
              lambda ps: jnp.concatenate(ps, axis=1))

    @pl.when(i >= n_ctx_tiles)
    def _latent_queries():
        first_blk = i * (R // W) - 1
        parts = []
        for blk in range(n_win):
            idx = jnp.clip(first_blk + blk, 0, n_blk - 1)
            parts.append(kv_ref[0, pl.ds(pl.multiple_of(idx * W, W), W), :])
        parts.append(kv_ref[0, 0:n_ctx, :])
        n_loc = n_win * W
        kj = lax.broadcasted_iota(jnp.int32, (W, LANE), 0)
        qi = lax.broadcasted_iota(jnp.int32, (W, LANE), 1)
        below = jnp.where(kj >= qi, 0.0, NEG_BIG).astype(F32)
        above = jnp.where(kj <= qi, 0.0, NEG_BIG).astype(F32)
        below_first = below + jnp.where(i == n_ctx_tiles, NEG_BIG, 0.0)
        above_last = above + jnp.where(i == n_blk // (R // W) - 1, NEG_BIG, 0.0)

        def split(ref):
            def blk(b, a):
                return ref[b * W:(b + 1) * W, a * LANE:(a + 1) * LANE]
            return [jnp.concatenate([blk(0, 0) + below_first, blk(1, 0), blk(2, 0) + above,
                                     ref[n_loc:n_loc + n_ctx, 0:LANE]], axis=0),
                    jnp.concatenate([blk(1, 1) + below, blk(2, 1), blk(3, 1) + above_last,
                                     ref[n_loc:n_loc + n_ctx, LANE:2 * LANE]], axis=0)]

        def merge(ps):
            zero = jnp.zeros((W, LANE), BF16)
            live = (n_win - 1) * W
            return jnp.concatenate(
                [jnp.concatenate([ps[0][0:live], zero, ps[0][live:]], axis=0),
                 jnp.concatenate([zero, ps[1]], axis=0)], axis=1)

        heads(jnp.concatenate(parts, axis=0), split, merge)


def _attention(q, kv, sink, n_ctx, tile_off, layer):
    B, T, _ = q.shape
    R = ROW_TILE
    nt = T // R
    kern = functools.partial(_attn_kernel, layer=layer, tile_off=tile_off, n_ctx_tiles=n_ctx // R,
                             n_ctx=n_ctx, n_lat=T - n_ctx)
    return pl.pallas_call(
        kern,
        grid=(B, nt - tile_off),
        in_specs=[
            pl.BlockSpec(memory_space=pltpu.SMEM),
            pl.BlockSpec((1, R, ATT_DIM), lambda b, i: (b, i + tile_off, 0)),
            pl.BlockSpec((1, T, 2 * KV_DIM), lambda b, i: (b, 0, 0)),
        ],
        out_specs=pl.BlockSpec((1, R, ATT_DIM), lambda b, i: (b, i + tile_off, 0)),
        out_shape=jax.ShapeDtypeStruct((B, T, ATT_DIM), BF16),
        scratch_shapes=[pltpu.VMEM((2, R + 2 * WINDOW + n_ctx, R), F32)],
        compiler_params=_params(48, 2),
        name="attention",
    )(sink, q, kv)


def _pool_kernel(u_ref, o_ref, p0, p2, p4, p8, *, n_ctx, n_lat):
    R = ROW_TILE
    T = n_ctx + n_lat
    RE = R + 2 * SUBLANE
    off_ctx = POOL_PAD
    off_lat = 2 * POOL_PAD
    n_comp = p0.shape[0] // R - 1

    for buf in (p0, p2, p4, p8):
        buf[...] = jnp.zeros(buf.shape, F32)
    p0[off_ctx:off_ctx + n_ctx, :] = u_ref[0, 0:n_ctx, :]
    p0[n_ctx + off_lat:T + off_lat, :] = u_ref[0, n_ctx:T, :]

    def stage(src, dst, back, fwd):
        def body(c, carry):
            r = pl.multiple_of(c * R, R)
            ext = src[pl.ds(r, RE), :]
            dst[pl.ds(r + SUBLANE, R), :] = _shifted_rows(ext, -back, R) + _shifted_rows(ext, fwd, R)
            return carry
        lax.fori_loop(0, n_comp, body, 0)

    stage(p0, p2, 1, 0)
    stage(p2, p4, 1, 1)
    stage(p4, p8, 2, 2)

    lane_group = _iota_div((R, POOL_DIM), 1, POOL_GROUP_DIM)
    row = lax.broadcasted_iota(jnp.int32, (R, POOL_DIM), 0)

    def emit(i, carry):
        is_ctx = i < n_ctx // R
        r0 = pl.multiple_of(i * R, R)
        r = pl.multiple_of(r0 + jnp.where(is_ctx, off_ctx, off_lat), SUBLANE)
        pos = row + jnp.where(is_ctx, r0, r0 - n_ctx)
        seq_len = jnp.where(is_ctx, n_ctx, n_lat)
        ext8 = p8[pl.ds(r - SUBLANE, RE), :]
        sums = (p2[pl.ds(r, R), :], p4[pl.ds(r, R), :], ext8[SUBLANE:SUBLANE + R],
                _shifted_rows(ext8, -4, R) + _shifted_rows(ext8, 4, R))
        tot = sums[0]
        lo = jnp.full((R, POOL_DIM), POOL_WINDOWS[0] // 2, jnp.int32)
        hi = jnp.full((R, POOL_DIM), POOL_WINDOWS[0] - POOL_WINDOWS[0] // 2 - 1, jnp.int32)
        for g in range(1, len(POOL_WINDOWS)):
            w = POOL_WINDOWS[g]
            sel = lane_group == g
            tot = jnp.where(sel, sums[g], tot)
            lo = jnp.where(sel, w // 2, lo)
            hi = jnp.where(sel, w - w // 2 - 1, hi)
        cnt = (jnp.minimum(pos + hi + 1, seq_len) - jnp.maximum(pos - lo, 0)).astype(F32)
        o_ref[0, pl.ds(r0, R), :] = (tot / cnt - u_ref[0, pl.ds(r0, R), :]).astype(BF16)
        return carry

    lax.fori_loop(0, T // R, emit, 0)


def _pool(pu, n_ctx):
    B, T, _ = pu.shape
    R = ROW_TILE
    rows = (pl.cdiv(T + 3 * POOL_PAD, R) + 1) * R
    kern = functools.partial(_pool_kernel, n_ctx=n_ctx, n_lat=T - n_ctx)
    return pl.pallas_call(
        kern,
        grid=(B,),
        in_specs=[pl.BlockSpec((1, T, POOL_DIM), lambda b: (b, 0, 0))],
        out_specs=pl.BlockSpec((1, T, POOL_DIM), lambda b: (b, 0, 0)),
        out_shape=jax.ShapeDtypeStruct((B, T, POOL_DIM), BF16),
        scratch_shapes=[pltpu.VMEM((rows, POOL_DIM), F32)] * 4,
        compiler_params=_params(48, 1),
        name="pool",
    )(pu)


def _gla_kernel(qkv_ref, la_ref, gn_ref, o_ref, of_s, ob_s, stf, stb, *, n_ctx):
    CH = GLA_CHUNK
    R = ROW_TILE
    NC = R // CH
    T = of_s.shape[0]
    n_tiles = T // R
    n_ctx_tiles = n_ctx // R
    QK = GLA_QK_DIM
    DV = GLA_V_DIM
    H = GLA_HEADS

    stf[...] = jnp.zeros(stf.shape, F32)
    stb[...] = jnp.zeros(stb.shape, F32)

    same_chunk = _iota_div((R, R), 0, CH) == _iota_div((R, R), 1, CH)
    ri = lax.broadcasted_iota(jnp.int32, (R, R), 0)
    ci = lax.broadcasted_iota(jnp.int32, (R, R), 1)
    tri_lo = (same_chunk & (ci <= ri)).astype(BF16)
    tri_up = (same_chunk & (ci >= ri)).astype(BF16)
    rs = jnp.bitwise_and(lax.broadcasted_iota(jnp.int32, (H * R, R), 0), R - 1)
    cs = lax.broadcasted_iota(jnp.int32, (H * R, R), 1)
    same_chunk_h = (lax.shift_right_logical(rs, CH.bit_length() - 1)
                    == lax.shift_right_logical(cs, CH.bit_length() - 1))
    keep_lo = same_chunk_h & (cs <= rs)
    keep_up = same_chunk_h & (cs >= rs)
    q_head = _iota_div((H * R, QK), 0, R) == _iota_div((H * R, QK), 1, GLA_DK)
    o_head = _iota_div((R, DV), 1, GLA_DV)
    st_keep = _iota_div((DV, QK), 0, GLA_DV) == _iota_div((DV, QK), 1, GLA_DK)
    nt_dims = (((1,), (1,)), ((), ()))
    tn_dims = (((0,), (0,)), ((), ()))

    def split3(a):
        a1 = a.astype(BF16)
        r1 = a - a1.astype(F32)
        a2 = r1.astype(BF16)
        a3 = (r1 - a2.astype(F32)).astype(BF16)
        return jnp.concatenate([a1, a2, a3], axis=1)

    def rows_of_chunks(cum, offset):
        picks = [cum[c * CH + offset:c * CH + offset + 1, :] for c in range(NC)]
        full = jnp.concatenate([jnp.broadcast_to(p, (CH, QK)) for p in picks], axis=0)
        return picks, full

    def load(t, col0, tri):
        r0 = pl.multiple_of(t * R, R)
        a = la_ref[0, pl.ds(r0, R), col0:col0 + QK]
        c3 = jnp.dot(tri, split3(a), preferred_element_type=F32)
        return dict(r0=r0, cum=c3[:, 0:QK] + c3[:, QK:2 * QK] + c3[:, 2 * QK:3 * QK],
                    q=qkv_ref[0, pl.ds(r0, R), 0:QK].astype(F32),
                    k=qkv_ref[0, pl.ds(r0, R), QK:2 * QK].astype(F32),
                    v=qkv_ref[0, pl.ds(r0, R), 2 * QK:])

    def scale(d, last):
        cum, q, k = d["cum"], d["q"], d["k"]
        tots, tot_b = rows_of_chunks(cum, last)
        _, mid_b = rows_of_chunks(cum, CH // 2)
        d["dec"] = [jnp.exp(t) for t in tots]
        d["qe"] = (q * jnp.exp(cum)).astype(BF16)
        qm = q * jnp.exp(cum - mid_b)
        d["qs"] = jnp.where(q_head, jnp.concatenate([qm] * H, axis=0), 0.0).astype(BF16)
        d["km"] = (k * jnp.exp(mid_b - cum)).astype(BF16)
        d["kl"] = (k * jnp.exp(tot_b - cum)).astype(BF16)

    def scores(d, keep):
        att = lax.dot_general(d["qs"], d["km"], nt_dims, preferred_element_type=F32)
        d["att"] = jnp.where(keep, att, 0.0).astype(BF16)
        d["kv"] = [lax.dot_general(d["v"][c * CH:(c + 1) * CH], d["kl"][c * CH:(c + 1) * CH], tn_dims,
                                   preferred_element_type=F32) for c in range(NC)]

    def intra(d):
        o_all = jnp.dot(d["att"], d["v"], preferred_element_type=F32)
        o = o_all[0:R]
        for hd in range(1, H):
            o = jnp.where(o_head == hd, o_all[hd * R:(hd + 1) * R], o)
        d["o"] = o

    def inter(d, order, st_ref, o_scr):
        state = st_ref[...]
        parts = [None] * NC
        for c in order:
            parts[c] = lax.dot_general(d["qe"][c * CH:(c + 1) * CH], state.astype(BF16), nt_dims,
                                       preferred_element_type=F32)
            state = state * d["dec"][c] + jnp.where(st_keep, d["kv"][c], 0.0)
        st_ref[...] = state
        o_scr[pl.ds(d["r0"], R), :] = d["o"] + jnp.concatenate(parts, axis=0)

    def step(s, carry):
        tb = jnp.where(s < n_ctx_tiles, n_ctx_tiles - 1 - s, n_tiles - 1 - (s - n_ctx_tiles))
        f = load(s, 0, tri_lo)
        b = load(tb, QK, tri_up)
        scale(f, CH - 1)
        scale(b, 0)
        scores(f, keep_lo)
        scores(b, keep_up)
        intra(f)
        intra(b)
        inter(f, range(NC), stf, of_s)
        inter(b, range(NC - 1, -1, -1), stb, ob_s)
        return carry

    lax.fori_loop(0, n_tiles, step, 0)

    gmean = (_iota_div((DV, DV), 0, GLA_DV) == _iota_div((DV, DV), 1, GLA_DV)).astype(BF16) * (1.0 / GLA_DV)
    gn = gn_ref[...]

    def finish(i, carry):
        r0 = pl.multiple_of(i * R, R)
        o = of_s[pl.ds(r0, R), :] + ob_s[pl.ds(r0, R), :]
        oo = o * o
        hi = oo.astype(BF16)
        lo = (oo - hi.astype(F32)).astype(BF16)
        ms = jnp.dot(hi, gmean, preferred_element_type=F32) + jnp.dot(lo, gmean, preferred_element_type=F32)
        o_ref[0, pl.ds(r0, R), :] = (o * lax.rsqrt(ms + EPS) * gn).astype(BF16)
        return carry

    lax.fori_loop(0, n_tiles, finish, 0)


def _gla(gqkv, la, gla_norm, n_ctx, layer):
    B, T, _ = gqkv.shape
    kern = functools.partial(_gla_kernel, n_ctx=n_ctx)
    return pl.pallas_call(
        kern,
        grid=(B,),
        in_specs=[
            pl.BlockSpec((1, T, 2 * GLA_QK_DIM + GLA_V_DIM), lambda b: (b, 0, 0)),
            pl.BlockSpec((1, T, 2 * GLA_QK_DIM), lambda b: (b, 0, 0)),
            _layer_spec((1, GLA_V_DIM), layer),
        ],
        out_specs=pl.BlockSpec((1, T, GLA_V_DIM), lambda b: (b, 0, 0)),
        out_shape=jax.ShapeDtypeStruct((B, T, GLA_V_DIM), BF16),
        scratch_shapes=[pltpu.VMEM((T, GLA_V_DIM), F32), pltpu.VMEM((T, GLA_V_DIM), F32),
                        pltpu.VMEM((GLA_V_DIM, GLA_QK_DIM), F32),
                        pltpu.VMEM((GLA_V_DIM, GLA_QK_DIM), F32)],
        compiler_params=_params(48, 1),
        name="gla",
    )(gqkv, la, gla_norm)


def _merge_kernel(x_ref, att_ref, dp_ref, on_ref, sgr_ref, h_ref, mod_ref, gpost_ref,
                  wba_ref, pbd_ref, ps_ref, wbp_ref, wbg_ref, wmg_ref, wo_ref, o_ref):
    D = D_MODEL
    h = h_ref[0]

    def gate(j):
        return _sigmoid(jnp.dot(h, wmg_ref[:, j * D:(j + 1) * D], preferred_element_type=F32))

    m = gate(0) * jnp.dot(att_ref[0], wba_ref[...], preferred_element_type=F32)
    yp = jnp.dot(dp_ref[0], pbd_ref[...], preferred_element_type=F32) * ps_ref[...]
    m = m + gate(1) * jnp.dot(yp.astype(BF16), wbp_ref[...], preferred_element_type=F32)
    m = m + gate(2) * jnp.dot(on_ref[0] * sgr_ref[0], wbg_ref[...], preferred_element_type=F32)
    y = jnp.dot(m.astype(BF16), wo_ref[...], preferred_element_type=F32)
    ms = jnp.mean(y * y, axis=-1, keepdims=True)
    o_ref[0] = x_ref[0] + mod_ref[0][:, 2 * D:3 * D] * (y * lax.rsqrt(ms + EPS) * gpost_ref[...])


def _merge(xa, y_att, d_pool, on, sgr, h, mods, g_post, wba, pbd, ps, wbp, wbg, wmg, wo,
           n_ctx_tiles, tile_off, layer):
    B, T, D = xa.shape
    R = ROW_TILE
    nt = T // R
    ctx_row = mods.shape[1] - 1

    def tile(width):
        return pl.BlockSpec((1, R, width), lambda b, i: (b, i + tile_off, 0))

    return pl.pallas_call(
        _merge_kernel,
        grid=(B, nt - tile_off),
        in_specs=[
            tile(D), tile(ATT_DIM), tile(POOL_DIM), tile(GLA_V_DIM), tile(GLA_V_DIM), tile(D),
            _mod_spec(layer, ctx_row, lambda i: i + tile_off < n_ctx_tiles),
            _layer_spec((1, D), layer),
            _layer_spec((ATT_DIM, D), layer), _layer_spec((POOL_DIM, POOL_DIM), layer),
            _layer_spec((1, POOL_DIM), layer), _layer_spec((POOL_DIM, D), layer),
            _layer_spec((GLA_V_DIM, D), layer), _layer_spec((D, 3 * D), layer),
            _layer_spec((D, D), layer),
        ],
        out_specs=pl.BlockSpec((1, R, D), lambda b, i: (b, i, 0)),
        out_shape=jax.ShapeDtypeStruct((B, T - tile_off * R, D), F32),
        compiler_params=_params(48, 2),
        name="merge",
    )(xa, y_att, d_pool, on, sgr, h, mods, g_post, wba, pbd, ps, wbp, wbg, wmg, wo)


def _ffn_kernel(x_ref, xp_ref, xn_ref, mod_ref, gpre_ref, wup_ref, cw_ref, cb_ref, wdn_ref,
                gpost_ref, o_ref, he_s, u0_s, u1_s, act_s, acc_s, *, n_ctx_tiles, n_tiles):
    D = D_MODEL
    R = x_ref.shape[1]
    S = SUBLANE
    G = R // S
    CK = FF_CHUNK
    n_chunks = D_FF // CK
    i = pl.program_id(1)
    mod = mod_ref[0]
    scale, shift = mod[:, 4 * D:5 * D], mod[:, 3 * D:4 * D]
    p_row = lax.broadcasted_iota(jnp.int32, (R, R), 0)
    seq_of_row = jnp.bitwise_and(p_row, S - 1) * G + lax.shift_right_logical(p_row, SUBLANE_LOG2)
    to_perm = (lax.broadcasted_iota(jnp.int32, (R, R), 1) == seq_of_row).astype(BF16)
    h = _modulated_norm(x_ref[0], gpre_ref[...], scale, shift).astype(BF16)
    he_s[0:R, :] = jnp.dot(to_perm, h, preferred_element_type=F32).astype(BF16)
    halo = jnp.concatenate([xp_ref[0, S - 1:S, :], xn_ref[0, 0:1, :], jnp.zeros((S - 2, D), F32)], axis=0)
    row = lax.broadcasted_iota(jnp.int32, (S, 1), 0)
    seq_start = (i == 0) | (i == n_ctx_tiles)
    seq_end = (i == n_ctx_tiles - 1) | (i == n_tiles - 1)
    outside = ((row == 0) & seq_start) | ((row == 1) & seq_end) | (row >= 2)
    he_s[R:R + S, :] = jnp.where(outside, 0.0, _modulated_norm(halo, gpre_ref[...], scale, shift)).astype(BF16)
    sub = lax.broadcasted_iota(jnp.int32, (S, 2 * CK), 0)

    def chunk_cols(ref, j):
        return jnp.concatenate([ref[:, j * CK:(j + 1) * CK], ref[:, D_FF + j * CK:D_FF + (j + 1) * CK]],
                               axis=1)

    def up(j, dst):
        he = he_s[...]
        dst[:, 0:CK] = jnp.dot(he, wup_ref[:, j * CK:(j + 1) * CK], preferred_element_type=F32)
        dst[:, CK:] = jnp.dot(he, wup_ref[:, D_FF + j * CK:D_FF + (j + 1) * CK],
                              preferred_element_type=F32)

    def finish(j, src):
        u = src[0:R, :]
        edge = src[R:R + S, :]
        prev0 = jnp.where(sub == 0, edge[0:1], pltpu.roll(u[R - S:R], 1, 0))
        next_last = jnp.where(sub == S - 1, edge[1:2], pltpu.roll(u[0:S], S - 1, 0))
        prev = jnp.concatenate([prev0, u[0:R - S]], axis=0)
        nxt = jnp.concatenate([u[S:R], next_last], axis=0)
        cw = chunk_cols(cw_ref, j)
        c = prev * cw[0:1] + u * cw[1:2] + nxt * cw[2:3] + chunk_cols(cb_ref, j)
        g = c[:, CK:]
        act_s[:, j * CK:(j + 1) * CK] = (c[:, :CK] * (g * _sigmoid(g))).astype(BF16)

    def down(j0, j1):
        y = jnp.dot(act_s[:, j0 * CK:j1 * CK], wdn_ref[j0 * CK:j1 * CK, :], preferred_element_type=F32)
        if j0 == 0:
            acc_s[...] = y
        else:
            acc_s[...] += y

    slots = (u0_s, u1_s)
    up(0, slots[0])
    group_start = 0
    for j in range(n_chunks):
        if j + 1 < n_chunks:
            up(j + 1, slots[(j + 1) % 2])
        finish(j, slots[j % 2])
        if j + 1 - group_start == FF_GROUP or j + 1 == n_chunks:
            down(group_start, j + 1)
            group_start = j + 1

    acc = acc_s[...]
    acc_hi = acc.astype(BF16)
    acc_lo = (acc - acc_hi.astype(F32)).astype(BF16)
    t_row = lax.broadcasted_iota(jnp.int32, (R, R), 0)
    p_col = lax.broadcasted_iota(jnp.int32, (R, R), 1)
    to_seq = (t_row == jnp.bitwise_and(p_col, S - 1) * G + lax.shift_right_logical(p_col, SUBLANE_LOG2)).astype(BF16)
    y = (jnp.dot(to_seq, acc_hi, preferred_element_type=F32)
         + jnp.dot(to_seq, acc_lo, preferred_element_type=F32))
    ms = jnp.mean(y * y, axis=-1, keepdims=True)
    gate = mod[:, 5 * D:6 * D]
    o_ref[0] = x_ref[0] + gate * (y * lax.rsqrt(ms + EPS) * gpost_ref[...])


def _ffn(xa, mods, g_pre, w_up, conv_w, conv_b, w_down, g_post, n_ctx_tiles, row_tile, layer):
    B, T, D = xa.shape
    R = row_tile
    H = SUBLANE
    nt = T // R
    ctx_row = mods.shape[1] - 1
    rpb = R // H
    RE = R + H
    kern = functools.partial(_ffn_kernel, n_ctx_tiles=n_ctx_tiles, n_tiles=nt)
    return pl.pallas_call(
        kern,
        grid=(B, nt),
        in_specs=[
            pl.BlockSpec((1, R, D), lambda b, i: (b, i, 0)),
            pl.BlockSpec((1, H, D), lambda b, i: (b, jnp.maximum(i * rpb - 1, 0), 0)),
            pl.BlockSpec((1, H, D), lambda b, i: (b, jnp.minimum((i + 1) * rpb, T // H - 1), 0)),
            _mod_spec(layer, ctx_row, lambda i: i < n_ctx_tiles),
            _layer_spec((1, D), layer),
            _layer_spec((D, 2 * D_FF), layer), _layer_spec((3, 2 * D_FF), layer),
            _layer_spec((1, 2 * D_FF), layer), _layer_spec((D_FF, D), layer),
            _layer_spec((1, D), layer),
        ],
        out_specs=pl.BlockSpec((1, R, D), lambda b, i: (b, i, 0)),
        out_shape=jax.ShapeDtypeStruct((B, T, D), F32),
        scratch_shapes=[pltpu.VMEM((RE, D), BF16), pltpu.VMEM((RE, 2 * FF_CHUNK), F32),
                        pltpu.VMEM((RE, 2 * FF_CHUNK), F32), pltpu.VMEM((R, D_FF), BF16),
                        pltpu.VMEM((R, D), F32)],
        compiler_params=_params(56, 2),
        name="ffn",
    )(xa, xa, xa, mods, g_pre, w_up, conv_w, conv_b, w_down, g_post)


def _rope_table(n_ctx, n_lat):
    rows = n_lat // GRID_W
    row = jnp.repeat(jnp.arange(rows, dtype=F32), GRID_W)
    col = jnp.tile(jnp.arange(GRID_W, dtype=F32), rows)
    n = HEAD_DIM // 4
    inv = ROPE_BASE ** (-jnp.arange(n, dtype=F32) / n)
    ang = jnp.concatenate([row[:, None] * inv, col[:, None] * inv], axis=-1)
    cos, sin = jnp.cos(ang), jnp.sin(ang)
    zero = jnp.zeros_like(sin)
    reps = LANE // HEAD_DIM
    cos_t = jnp.tile(jnp.concatenate([cos, cos], axis=-1), (1, reps))
    sin_lo = jnp.tile(jnp.concatenate([-sin, zero], axis=-1), (1, reps))
    sin_hi = jnp.tile(jnp.concatenate([zero, sin], axis=-1), (1, reps))
    lat = jnp.concatenate([cos_t, sin_lo, sin_hi], axis=-1)
    ctx = jnp.concatenate([jnp.ones((n_ctx, LANE), F32), jnp.zeros((n_ctx, 2 * LANE), F32)], axis=-1)
    return jnp.concatenate([ctx, lat], axis=0)


def kernel(x, c, ctx, c_ctx, w_ada, b_ada, g_pre_mix, g_post_mix, g_pre_ffn, g_post_ffn, w_in, att_sink, pool_w, pool_scale, gla_wa2, gla_ba, gla_norm, w_br_att, w_br_pool, w_br_gla, w_o, w_up, conv_w, conv_b, w_down):
    B, L, D = x.shape
    C = ctx.shape[1]
    depth = w_in.shape[0]
    R = ROW_TILE
    assert D == D_MODEL and C % R == 0 and L % R == 0 and L % GRID_W == 0
    n_ctx_tiles = C // R

    mod_rows = -(-(B + 1) // SUBLANE) * SUBLANE
    cc = jnp.zeros((mod_rows, D), F32).at[:B].set(c).at[mod_rows - 1].set(c_ctx)
    mods = _ada_table(cc, w_ada, b_ada).reshape(depth, mod_rows, 1, 6 * D)

    rope_tab = _rope_table(C, L)
    split = _O_GLR + 2 * GLA_GATE_RANK
    w_in_p = jnp.concatenate(
        [w_in[..., :split], jnp.zeros((depth, D, _GLR_PAD), w_in.dtype)], axis=-1).astype(BF16)
    w_mg = w_in[..., split:].astype(BF16)
    wa = jnp.zeros((depth, LANE, 2 * GLA_QK_DIM), F32)
    wa = wa.at[:, 0:GLA_GATE_RANK, 0:GLA_QK_DIM].set(gla_wa2[:, 0])
    wa = wa.at[:, GLA_GATE_RANK:2 * GLA_GATE_RANK, GLA_QK_DIM:].set(gla_wa2[:, 1])
    wa_hi = wa.astype(BF16)
    wa = jnp.concatenate([wa_hi, wa_hi, (wa - wa_hi.astype(F32)).astype(BF16)], axis=1)
    ba = gla_ba.reshape(depth, 1, 2 * GLA_QK_DIM)
    pbd = jnp.zeros((depth, POOL_DIM, POOL_DIM), F32)
    for g in range(len(POOL_WINDOWS)):
        sl = slice(g * POOL_GROUP_DIM, (g + 1) * POOL_GROUP_DIM)
        pbd = pbd.at[:, sl, sl].set(pool_w[:, g])
    pbd = pbd.astype(BF16)
    wba, wbp, wbg, wo = (w.astype(BF16) for w in (w_br_att, w_br_pool, w_br_gla, w_o))
    wup, wdn = w_up.astype(BF16), w_down.astype(BF16)

    def rows(t):
        return t.reshape(depth, 1, -1)

    g_pre_mix, g_post_mix, g_pre_ffn, g_post_ffn, pool_scale, gla_norm, conv_b = (
        rows(t) for t in (g_pre_mix, g_post_mix, g_pre_ffn, g_post_ffn, pool_scale, gla_norm, conv_b))

    xa = jnp.concatenate([ctx, x], axis=1)
    for l in range(depth):
        last = l == depth - 1
        off = n_ctx_tiles if last else 0
        q, kv, pu, gqkv, sgr, la, h = _in_proj(xa, mods, g_pre_mix, w_in_p, rope_tab, wa, ba, n_ctx_tiles, l)
        y_att = _attention(q, kv, att_sink, C, off, l)
        d_pool = _pool(pu, C)
        on = _gla(gqkv, la, gla_norm, C, l)
        xa = _merge(xa, y_att, d_pool, on, sgr, h, mods, g_post_mix, wba, pbd, pool_scale, wbp, wbg, w_mg, wo,
                    n_ctx_tiles, off, l)
        xa = _ffn(xa, mods, g_pre_ffn, wup, conv_w, conv_b, wdn, g_post_ffn,
                  n_ctx_tiles - off, 2 * R if last else R, l)
    return xa
```

```python
import functools

import jax
import jax.numpy as jnp
import numpy as np
from jax import lax
from jax.experimental import pallas as pl
from jax.experimental.pallas import tpu as pltpu

F32 = jnp.float32
BF16 = jnp.bfloat16
HIGHEST = lax.Precision.HIGHEST

D_MODEL = 1024
GRID_W = 64
EPS = 1e-6

HEAD_DIM = 64
ATT_HEADS = 8
ATT_KV_HEADS = 2
ATT_GROUP = ATT_HEADS // ATT_KV_HEADS
WINDOW = 128
ROPE_BASE = 10000.0
ATT_DIM = ATT_HEADS * HEAD_DIM
KV_DIM = ATT_KV_HEADS * HEAD_DIM

POOL_WINDOWS = (2, 4, 8, 16)
POOL_GROUP_DIM = 64
POOL_DIM = len(POOL_WINDOWS) * POOL_GROUP_DIM
POOL_PAD = 16

GLA_HEADS = 4
GLA_DK = 32
GLA_DV = 64
GLA_GATE_RANK = 16
GLA_TAU = 16.0
GLA_QK_DIM = GLA_HEADS * GLA_DK
GLA_V_DIM = GLA_HEADS * GLA_DV
GLA_CHUNK = 64

D_FF = 2816
FF_CHUNK = 256
FF_GROUP = 4
PAIR = 2

LANE = 128
SUBLANE = 8
SUBLANE_LOG2 = SUBLANE.bit_length() - 1
ROW_TILE = 256

_GLR_PAD = LANE - 2 * GLA_GATE_RANK
_O_AQ = 0
_O_AK = _O_AQ + ATT_DIM
_O_AV = _O_AK + KV_DIM
_O_PU = _O_AV + KV_DIM
_O_GQ = _O_PU + POOL_DIM
_O_GK = _O_GQ + GLA_QK_DIM
_O_GV = _O_GK + GLA_QK_DIM
_O_GR = _O_GV + GLA_V_DIM
_O_GLR = _O_GR + GLA_V_DIM
IN_DIM_PAD = _O_GLR + LANE

NEG_BIG = -1e30
LOG2E = float(np.log2(np.e))


def _params(vmem_mb, n_axes):
    return pltpu.CompilerParams(
        dimension_semantics=("arbitrary",) * n_axes,
        vmem_limit_bytes=vmem_mb * 1024 * 1024,
    )


def _resident(shape):
    nd = len(shape)
    return pl.BlockSpec(shape, lambda *_: (0,) * nd, pipeline_mode=pl.Buffered(1))


def _layer_spec(shape, layer):
    nd = len(shape)
    return pl.BlockSpec((None,) + tuple(shape), lambda *_: (layer,) + (0,) * nd,
                        pipeline_mode=pl.Buffered(1))


def _mod_spec(layer, ctx_row, is_ctx_tile):
    return pl.BlockSpec((None, 1, 1, 6 * D_MODEL),
                        lambda b, i: (layer, jnp.where(is_ctx_tile(i), ctx_row, b), 0, 0))


def _sigmoid(v):
    return 1.0 / (1.0 + jnp.exp(-v))


def _iota_div(shape, axis, divisor):
    shift = int(np.log2(divisor))
    assert 1 << shift == divisor
    return lax.shift_right_logical(lax.broadcasted_iota(jnp.int32, shape, axis), shift)


def _shifted_rows(ext, shift, n_rows):
    if shift:
        ext = pltpu.roll(ext, (-shift) % ext.shape[0], 0)
    return ext[SUBLANE:SUBLANE + n_rows]


def _modulated_norm(x, gain, scale, shift):
    ms = jnp.mean(x * x, axis=-1, keepdims=True)
    return (x * lax.rsqrt(ms + EPS) * gain) * (1.0 + scale) + shift


def _ada_kernel(c_ref, w_ref, b_ref, o_ref):
    c = c_ref[...]
    a = c * _sigmoid(c)
    o_ref[0] = jnp.dot(a, w_ref[0], precision=HIGHEST, preferred_element_type=F32) + b_ref[0]


def _ada_table(cc, w_ada, b_ada):
    depth = w_ada.shape[0]
    rows = cc.shape[0]
    n_col = w_ada.shape[2] // D_MODEL
    return pl.pallas_call(
        _ada_kernel,
        grid=(depth, n_col),
        in_specs=[
            pl.BlockSpec((rows, D_MODEL), lambda l, j: (0, 0)),
            pl.BlockSpec((1, D_MODEL, D_MODEL), lambda l, j: (l, 0, j)),
            pl.BlockSpec((1, 1, D_MODEL), lambda l, j: (l, 0, j)),
        ],
        out_specs=pl.BlockSpec((1, rows, D_MODEL), lambda l, j: (l, 0, j)),
        out_shape=jax.ShapeDtypeStruct((depth, rows, w_ada.shape[2]), F32),
        compiler_params=_params(32, 2),
        name="ada_table",
    )(cc, w_ada, b_ada.reshape(depth, 1, -1))


def _in_proj_kernel(x_ref, moda_ref, modb_ref, g_ref, w_ref, ropea_ref, ropeb_ref, wa_ref, ba_ref,
                    q_ref, kv_ref, pu_ref, gqkv_ref, sgr_ref, la_ref, h_ref):
    D = D_MODEL
    R = ROW_TILE
    mods = (moda_ref[0], modb_ref[0])
    ropes = (ropea_ref, ropeb_ref)

    def norm(t):
        rows = pl.ds(t * R, R)
        h_ref[rows, :] = _modulated_norm(x_ref[rows, :], g_ref[...], mods[t][:, D:2 * D],
                                         mods[t][:, 0:D]).astype(BF16)

    def project(t, after_first):
        rows = pl.ds(t * R, R)
        h = h_ref[rows, :]
        rope_ref = ropes[t]

        def proj(lo, width):
            return jnp.dot(h, w_ref[:, lo:lo + width], preferred_element_type=F32)

        cosf = rope_ref[:, 0:LANE]
        sin_lo = rope_ref[:, LANE:2 * LANE]
        sin_hi = rope_ref[:, 2 * LANE:3 * LANE]

        def rope(v):
            return (v * cosf + pltpu.roll(v, LANE - HEAD_DIM // 2, 1) * sin_lo
                    + pltpu.roll(v, HEAD_DIM // 2, 1) * sin_hi)

        aq = proj(_O_AQ, ATT_DIM)
        after_first()
        for j in range(ATT_DIM // LANE):
            q_ref[rows, j * LANE:(j + 1) * LANE] = (
                rope(aq[:, j * LANE:(j + 1) * LANE]) * (HEAD_DIM ** -0.5 * LOG2E)).astype(BF16)
        kv_ref[rows, 0:KV_DIM] = rope(proj(_O_AK, KV_DIM)).astype(BF16)
        kv_ref[rows, KV_DIM:2 * KV_DIM] = proj(_O_AV, KV_DIM).astype(BF16)

        pu_ref[rows, :] = proj(_O_PU, POOL_DIM)

        gqkv_ref[rows, 0:GLA_QK_DIM] = (proj(_O_GQ, GLA_QK_DIM) * (GLA_DK ** -0.5)).astype(BF16)
        gqkv_ref[rows, GLA_QK_DIM:] = proj(_O_GK, GLA_QK_DIM + GLA_V_DIM).astype(BF16)
        gr = proj(_O_GR, GLA_V_DIM)
        sgr_ref[rows, :] = (gr * _sigmoid(gr)).astype(BF16)

        glr = proj(_O_GLR, LANE)
        glr_hi = glr.astype(BF16)
        glr_lo = (glr - glr_hi.astype(F32)).astype(BF16)
        z = jnp.dot(jnp.concatenate([glr_hi, glr_lo, glr_hi], axis=1), wa_ref[...],
                    preferred_element_type=F32) + ba_ref[...]
        la_ref[rows, :] = (jnp.minimum(z, 0.0) - jnp.log(1.0 + jnp.exp(-jnp.abs(z)))) * (1.0 / GLA_TAU)

    norm(0)
    project(0, lambda: norm(1))
    project(1, lambda: None)


def _tile_of_pair(t, tiles_per_seq, n_ctx_tiles, ctx_row, layer):
    def seq_tile(s):
        return lax.rem(s * PAIR + t, tiles_per_seq)

    def mod_index(s):
        tile = s * PAIR + t
        return (layer, jnp.where(lax.rem(tile, tiles_per_seq) < n_ctx_tiles, ctx_row,
                                 lax.div(tile, tiles_per_seq)), 0, 0)

    return seq_tile, pl.BlockSpec((None, 1, 1, 6 * D_MODEL), mod_index)


def _in_proj(xa, mods, g_pre, w_in, rope_tab, wa, ba, n_ctx_tiles, layer):
    B, T, D = xa.shape
    R = ROW_TILE
    nt = T // R
    assert (B * nt) % PAIR == 0
    ctx_row = mods.shape[1] - 1

    def pair(width):
        return pl.BlockSpec((PAIR * R, width), lambda s: (s, 0))

    seq_tiles, mod_specs = zip(*(_tile_of_pair(t, nt, n_ctx_tiles, ctx_row, layer) for t in range(PAIR)))
    rope_specs = [pl.BlockSpec((R, 3 * LANE), lambda s, f=f: (f(s), 0)) for f in seq_tiles]
    widths = (ATT_DIM, 2 * KV_DIM, POOL_DIM, 2 * GLA_QK_DIM + GLA_V_DIM, GLA_V_DIM,
              2 * GLA_QK_DIM, D)
    dtypes = (BF16, BF16, F32, BF16, BF16, F32, BF16)
    outs = pl.pallas_call(
        _in_proj_kernel,
        grid=(B * nt // PAIR,),
        in_specs=[
            pair(D), *mod_specs,
            _layer_spec((1, D), layer),
            _layer_spec((D, IN_DIM_PAD), layer),
            *rope_specs,
            _layer_spec((3 * LANE, 2 * GLA_QK_DIM), layer),
            _layer_spec((1, 2 * GLA_QK_DIM), layer),
        ],
        out_specs=[pair(w) for w in widths],
        out_shape=[jax.ShapeDtypeStruct((B * T, w), dt) for w, dt in zip(widths, dtypes)],
        compiler_params=_params(48, 1),
        name="in_proj",
    )(xa.reshape(B * T, D), mods, mods, g_pre, w_in, rope_tab, rope_tab, wa, ba)
    return [o.reshape(B, T, -1) for o in outs]


def _attn_kernel(sink_ref, q_ref, kv_ref, o_ref, s_s, *, layer, tile_off, n_ctx_tiles, n_ctx, n_lat):
    R = ROW_TILE
    W = WINDOW
    assert R == 2 * W == 2 * LANE
    n_win = R // W + 2
    i = pl.program_id(1) + tile_off
    n_blk = (n_ctx + n_lat) // W
    nt_dims = (((1,), (1,)), ((), ()))
    q = q_ref[0]

    def heads(kv, split, merge):
        n_keys = kv.shape[0]
        ks, v_ext_ts = [], []
        for g in range(ATT_KV_HEADS):
            ks.append(kv[:, g * HEAD_DIM:(g + 1) * HEAD_DIM])
            v = kv[:, KV_DIM + g * HEAD_DIM:KV_DIM + (g + 1) * HEAD_DIM].astype(F32)
            v_ext_ts.append(
                jnp.concatenate([v, jnp.ones((n_keys, LANE - HEAD_DIM), F32)], axis=1).T.astype(BF16))

        def scores(h):
            qh = q[:, h * HEAD_DIM:(h + 1) * HEAD_DIM]
            s_s[h % 2, 0:n_keys, :] = lax.dot_general(ks[h // ATT_GROUP], qh, nt_dims,
                                                      preferred_element_type=F32)

        scores(0)
        out_t = []
        for h in range(ATT_HEADS):
            if h + 1 < ATT_HEADS:
                scores(h + 1)
            sink = sink_ref[layer, h] * LOG2E
            ms, ps = [], []
            for s_half in split(s_s.at[h % 2]):
                m = jnp.maximum(jnp.max(s_half, axis=0, keepdims=True), sink)
                ms.append(m)
                ps.append(jnp.exp2(s_half - m).astype(BF16))
            o_ext = jnp.dot(v_ext_ts[h // ATT_GROUP], merge(ps), preferred_element_type=F32)
            den = o_ext[HEAD_DIM:HEAD_DIM + 1] + jnp.exp2(sink - jnp.concatenate(ms, axis=1))
            out_t.append(o_ext[0:HEAD_DIM] / den)
            if h % 2 == 1:
                pair = jnp.concatenate(out_t, axis=0).T
                o_ref[0, :, (h - 1) * HEAD_DIM:(h + 1) * HEAD_DIM] = pair.astype(BF16)
                out_t = []

    @pl.when(i < n_ctx_tiles)
    def _context_queries():
        heads(kv_ref[0, 0:n_ctx, :],
              lambda ref: [ref[0:n_ctx, a * LANE:(a + 1) * LANE] for a in range(R // LANE)],
              lambda ps: jnp.concatenate(ps, axis=1))

    @pl.when(i >= n_ctx_tiles)
    def _latent_queries():
        first_blk = i * (R // W) - 1
        parts = []
        for blk in range(n_win):
            idx = jnp.clip(first_blk + blk, 0, n_blk - 1)
            parts.append(kv_ref[0, pl.ds(pl.multiple_of(idx * W, W), W), :])
        parts.append(kv_ref[0, 0:n_ctx, :])
        n_loc = n_win * W
        kj = lax.broadcasted_iota(jnp.int32, (W, LANE), 0)
        qi = lax.broadcasted_iota(jnp.int32, (W, LANE), 1)
        below = jnp.where(kj >= qi, 0.0, NEG_BIG).astype(F32)
        above = jnp.where(kj <= qi, 0.0, NEG_BIG).astype(F32)
        below_first = below + jnp.where(i == n_ctx_tiles, NEG_BIG, 0.0)
        above_last = above + jnp.where(i == n_blk // (R // W) - 1, NEG_BIG, 0.0)

        def split(ref):
            def blk(b, a):
                return ref[b * W:(b + 1) * W, a * LANE:(a + 1) * LANE]
            return [jnp.concatenate([blk(0, 0) + below_first, blk(1, 0), blk(2, 0) + above,
                                     ref[n_loc:n_loc + n_ctx, 0:LANE]], axis=0),
                    jnp.concatenate([blk(1, 1) + below, blk(2, 1), blk(3, 1) + above_last,
                                     ref[n_loc:n_loc + n_ctx, LANE:2 * LANE]], axis=0)]

        def merge(ps):
            zero = jnp.zeros((W, LANE), BF16)
            live = (n_win - 1) * W
            return jnp.concatenate(
                [jnp.concatenate([ps[0][0:live], zero, ps[0][live:]], axis=0),
                 jnp.concatenate([zero, ps[1]], axis=0)], axis=1)

        heads(jnp.concatenate(parts, axis=0), split, merge)


def _attention(q, kv, sink, n_ctx, tile_off, layer):
    B, T, _ = q.shape
    R = ROW_TILE
    nt = T // R
    kern = functools.partial(_attn_kernel, layer=layer, tile_off=tile_off, n_ctx_tiles=n_ctx // R,
                             n_ctx=n_ctx, n_lat=T - n_ctx)
    return pl.pallas_call(
        kern,
        grid=(B, nt - tile_off),
        in_specs=[
            pl.BlockSpec(memory_space=pltpu.SMEM),
            pl.BlockSpec((1, R, ATT_DIM), lambda b, i: (b, i + tile_off, 0)),
            pl.BlockSpec((1, T, 2 * KV_DIM), lambda b, i: (b, 0, 0)),
        ],
        out_specs=pl.BlockSpec((1, R, ATT_DIM), lambda b, i: (b, i + tile_off, 0)),
        out_shape=jax.ShapeDtypeStruct((B, T, ATT_DIM), BF16),
        scratch_shapes=[pltpu.VMEM((2, R + 2 * WINDOW + n_ctx, R), F32)],
        compiler_params=_params(48, 2),
        name="attention",
    )(sink, q, kv)


def _pool_kernel(u_ref, o_ref, p0, p2, p4, p8, *, n_ctx, n_lat):
    R = ROW_TILE
    T = n_ctx + n_lat
    RE = R + 2 * SUBLANE
    off_ctx = POOL_PAD
    off_lat = 2 * POOL_PAD
    n_comp = p0.shape[0] // R - 1

    for buf in (p0, p2, p4, p8):
        buf[...] = jnp.zeros(buf.shape, F32)
    p0[off_ctx:off_ctx + n_ctx, :] = u_ref[0, 0:n_ctx, :]
    p0[n_ctx + off_lat:T + off_lat, :] = u_ref[0, n_ctx:T, :]

    def stage(src, dst, back, fwd):
        def body(c, carry):
            r = pl.multiple_of(c * R, R)
            ext = src[pl.ds(r, RE), :]
            dst[pl.ds(r + SUBLANE, R), :] = _shifted_rows(ext, -back, R) + _shifted_rows(ext, fwd, R)
            return carry
        lax.fori_loop(0, n_comp, body, 0)

    stage(p0, p2, 1, 0)
    stage(p2, p4, 1, 1)
    stage(p4, p8, 2, 2)

    lane_group = _iota_div((R, POOL_DIM), 1, POOL_GROUP_DIM)
    row = lax.broadcasted_iota(jnp.int32, (R, POOL_DIM), 0)

    def emit(i, carry):
        is_ctx = i < n_ctx // R
        r0 = pl.multiple_of(i * R, R)
        r = pl.multiple_of(r0 + jnp.where(is_ctx, off_ctx, off_lat), SUBLANE)
        pos = row + jnp.where(is_ctx, r0, r0 - n_ctx)
        seq_len = jnp.where(is_ctx, n_ctx, n_lat)
        ext8 = p8[pl.ds(r - SUBLANE, RE), :]
        sums = (p2[pl.ds(r, R), :], p4[pl.ds(r, R), :], ext8[SUBLANE:SUBLANE + R],
                _shifted_rows(ext8, -4, R) + _shifted_rows(ext8, 4, R))
        tot = sums[0]
        lo = jnp.full((R, POOL_DIM), POOL_WINDOWS[0] // 2, jnp.int32)
        hi = jnp.full((R, POOL_DIM), POOL_WINDOWS[0] - POOL_WINDOWS[0] // 2 - 1, jnp.int32)
        for g in range(1, len(POOL_WINDOWS)):
            w = POOL_WINDOWS[g]
            sel = lane_group == g
            tot = jnp.where(sel, sums[g], tot)
            lo = jnp.where(sel, w // 2, lo)
            hi = jnp.where(sel, w - w // 2 - 1, hi)
        cnt = (jnp.minimum(pos + hi + 1, seq_len) - jnp.maximum(pos - lo, 0)).astype(F32)
        o_ref[0, pl.ds(r0, R), :] = (tot / cnt - u_ref[0, pl.ds(r0, R), :]).astype(BF16)
        return carry

    lax.fori_loop(0, T // R, emit, 0)


def _pool(pu, n_ctx):
    B, T, _ = pu.shape
    R = ROW_TILE
    rows = (pl.cdiv(T + 3 * POOL_PAD, R) + 1) * R
    kern = functools.partial(_pool_kernel, n_ctx=n_ctx, n_lat=T - n_ctx)
    return pl.pallas_call(
        kern,
        grid=(B,),
        in_specs=[pl.BlockSpec((1, T, POOL_DIM), lambda b: (b, 0, 0))],
        out_specs=pl.BlockSpec((1, T, POOL_DIM), lambda b: (b, 0, 0)),
        out_shape=jax.ShapeDtypeStruct((B, T, POOL_DIM), BF16),
        scratch_shapes=[pltpu.VMEM((rows, POOL_DIM), F32)] * 4,
        compiler_params=_params(48, 1),
        name="pool",
    )(pu)


def _gla_kernel(qkv_ref, la_ref, gn_ref, o_ref, of_s, ob_s, stf, stb, *, n_ctx):
    CH = GLA_CHUNK
    R = ROW_TILE
    NC = R // CH
    T = of_s.shape[0]
    n_tiles = T // R
    n_ctx_tiles = n_ctx // R
    QK = GLA_QK_DIM
    DV = GLA_V_DIM
    H = GLA_HEADS

    stf[...] = jnp.zeros(stf.shape, F32)
    stb[...] = jnp.zeros(stb.shape, F32)

    same_chunk = _iota_div((R, R), 0, CH) == _iota_div((R, R), 1, CH)
    ri = lax.broadcasted_iota(jnp.int32, (R, R), 0)
    ci = lax.broadcasted_iota(jnp.int32, (R, R), 1)
    tri_lo = (same_chunk & (ci <= ri)).astype(BF16)
    tri_up = (same_chunk & (ci >= ri)).astype(BF16)
    rs = jnp.bitwise_and(lax.broadcasted_iota(jnp.int32, (H * R, R), 0), R - 1)
    cs = lax.broadcasted_iota(jnp.int32, (H * R, R), 1)
    same_chunk_h = (lax.shift_right_logical(rs, CH.bit_length() - 1)
                    == lax.shift_right_logical(cs, CH.bit_length() - 1))
    keep_lo = same_chunk_h & (cs <= rs)
    keep_up = same_chunk_h & (cs >= rs)
    q_head = _iota_div((H * R, QK), 0, R) == _iota_div((H * R, QK), 1, GLA_DK)
    o_head = _iota_div((R, DV), 1, GLA_DV)
    st_keep = _iota_div((DV, QK), 0, GLA_DV) == _iota_div((DV, QK), 1, GLA_DK)
    nt_dims = (((1,), (1,)), ((), ()))
    tn_dims = (((0,), (0,)), ((), ()))

    def split3(a):
        a1 = a.astype(BF16)
        r1 = a - a1.astype(F32)
        a2 = r1.astype(BF16)
        a3 = (r1 - a2.astype(F32)).astype(BF16)
        return jnp.concatenate([a1, a2, a3], axis=1)

    def rows_of_chunks(cum, offset):
        picks = [cum[c * CH + offset:c * CH + offset + 1, :] for c in range(NC)]
        full = jnp.concatenate([jnp.broadcast_to(p, (CH, QK)) for p in picks], axis=0)
        return picks, full

    def load(t, col0, tri):
        r0 = pl.multiple_of(t * R, R)
        a = la_ref[0, pl.ds(r0, R), col0:col0 + QK]
        c3 = jnp.dot(tri, split3(a), preferred_element_type=F32)
        return dict(r0=r0, cum=c3[:, 0:QK] + c3[:, QK:2 * QK] + c3[:, 2 * QK:3 * QK],
                    q=qkv_ref[0, pl.ds(r0, R), 0:QK].astype(F32),
                    k=qkv_ref[0, pl.ds(r0, R), QK:2 * QK].astype(F32),
                    v=qkv_ref[0, pl.ds(r0, R), 2 * QK:])

    def scale(d, last):
        cum, q, k = d["cum"], d["q"], d["k"]
        tots, tot_b = rows_of_chunks(cum, last)
        _, mid_b = rows_of_chunks(cum, CH // 2)
        d["dec"] = [jnp.exp(t) for t in tots]
        d["qe"] = (q * jnp.exp(cum)).astype(BF16)
        qm = q * jnp.exp(cum - mid_b)
        d["qs"] = jnp.where(q_head, jnp.concatenate([qm] * H, axis=0), 0.0).astype(BF16)
        d["km"] = (k * jnp.exp(mid_b - cum)).astype(BF16)
        d["kl"] = (k * jnp.exp(tot_b - cum)).astype(BF16)

    def scores(d, keep):
        att = lax.dot_general(d["qs"], d["km"], nt_dims, preferred_element_type=F32)
        d["att"] = jnp.where(keep, att, 0.0).astype(BF16)
        d["kv"] = [lax.dot_general(d["v"][c * CH:(c + 1) * CH], d["kl"][c * CH:(c + 1) * CH], tn_dims,
                                   preferred_element_type=F32) for c in range(NC)]

    def intra(d):
        o_all = jnp.dot(d["att"], d["v"], preferred_element_type=F32)
        o = o_all[0:R]
        for hd in range(1, H):
            o = jnp.where(o_head == hd, o_all[hd * R:(hd + 1) * R], o)
        d["o"] = o

    def inter(d, order, st_ref, o_scr):
        state = st_ref[...]
        parts = [None] * NC
        for c in order:
            parts[c] = lax.dot_general(d["qe"][c * CH:(c + 1) * CH], state.astype(BF16), nt_dims,
                                       preferred_element_type=F32)
            state = state * d["dec"][c] + jnp.where(st_keep, d["kv"][c], 0.0)
        st_ref[...] = state
        o_scr[pl.ds(d["r0"], R), :] = d["o"] + jnp.concatenate(parts, axis=0)

    def step(s, carry):
        tb = jnp.where(s < n_ctx_tiles, n_ctx_tiles - 1 - s, n_tiles - 1 - (s - n_ctx_tiles))
        f = load(s, 0, tri_lo)
        b = load(tb, QK, tri_up)
        scale(f, CH - 1)
        scale(b, 0)
        scores(f, keep_lo)
        scores(b, keep_up)
        intra(f)
        intra(b)
        inter(f, range(NC), stf, of_s)
        inter(b, range(NC - 1, -1, -1), stb, ob_s)
        return carry

    lax.fori_loop(0, n_tiles, step, 0)

    gmean = (_iota_div((DV, DV), 0, GLA_DV) == _iota_div((DV, DV), 1, GLA_DV)).astype(BF16) * (1.0 / GLA_DV)
    gn = gn_ref[...]

    def finish(i, carry):
        r0 = pl.multiple_of(i * R, R)
        o = of_s[pl.ds(r0, R), :] + ob_s[pl.ds(r0, R), :]
        oo = o * o
        hi = oo.astype(BF16)
        lo = (oo - hi.astype(F32)).astype(BF16)
        ms = jnp.dot(hi, gmean, preferred_element_type=F32) + jnp.dot(lo, gmean, preferred_element_type=F32)
        o_ref[0, pl.ds(r0, R), :] = (o * lax.rsqrt(ms + EPS) * gn).astype(BF16)
        return carry

    lax.fori_loop(0, n_tiles, finish, 0)


def _gla(gqkv, la, gla_norm, n_ctx, layer):
    B, T, _ = gqkv.shape
    kern = functools.partial(_gla_kernel, n_ctx=n_ctx)
    return pl.pallas_call(
        kern,
        grid=(B,),
        in_specs=[
            pl.BlockSpec((1, T, 2 * GLA_QK_DIM + GLA_V_DIM), lambda b: (b, 0, 0)),
            pl.BlockSpec((1, T, 2 * GLA_QK_DIM), lambda b: (b, 0, 0)),
            _layer_spec((1, GLA_V_DIM), layer),
        ],
        out_specs=pl.BlockSpec((1, T, GLA_V_DIM), lambda b: (b, 0, 0)),
        out_shape=jax.ShapeDtypeStruct((B, T, GLA_V_DIM), BF16),
        scratch_shapes=[pltpu.VMEM((T, GLA_V_DIM), F32), pltpu.VMEM((T, GLA_V_DIM), F32),
                        pltpu.VMEM((GLA_V_DIM, GLA_QK_DIM), F32),
                        pltpu.VMEM((GLA_V_DIM, GLA_QK_DIM), F32)],
        compiler_params=_params(48, 1),
        name="gla",
    )(gqkv, la, gla_norm)


N_MERGE_TILED = 6


def _merge_kernel(*refs):
    D = D_MODEL
    R = ROW_TILE
    tiled = refs[:PAIR * N_MERGE_TILED]
    mod_refs = refs[PAIR * N_MERGE_TILED:PAIR * N_MERGE_TILED + PAIR]
    (gpost_ref, wba_ref, pbd_ref, ps_ref, wbp_ref, wbg_ref, wmg_ref, wo_ref,
     o_ref) = refs[PAIR * N_MERGE_TILED + PAIR:]

    for t in range(PAIR):
        x_ref, att_ref, dp_ref, on_ref, sgr_ref, h_ref = tiled[t::PAIR]
        h = h_ref[...]

        def gate(j):
            return _sigmoid(jnp.dot(h, wmg_ref[:, j * D:(j + 1) * D], preferred_element_type=F32))

        m = gate(0) * jnp.dot(att_ref[...], wba_ref[...], preferred_element_type=F32)
        yp = jnp.dot(dp_ref[...], pbd_ref[...], preferred_element_type=F32) * ps_ref[...]
        m = m + gate(1) * jnp.dot(yp.astype(BF16), wbp_ref[...], preferred_element_type=F32)
        m = m + gate(2) * jnp.dot(on_ref[...] * sgr_ref[...], wbg_ref[...], preferred_element_type=F32)
        y = jnp.dot(m.astype(BF16), wo_ref[...], preferred_element_type=F32)
        ms = jnp.mean(y * y, axis=-1, keepdims=True)
        o_ref[t * R:(t + 1) * R, :] = (
            x_ref[...] + mod_refs[t][0][:, 2 * D:3 * D] * (y * lax.rsqrt(ms + EPS) * gpost_ref[...]))


def _merge(xa, y_att, d_pool, on, sgr, h, mods, g_post, wba, pbd, ps, wbp, wbg, wmg, wo,
           n_ctx_tiles, tile_off, layer):
    B, T, D = xa.shape
    R = ROW_TILE
    nt = T // R
    kept = nt - tile_off
    assert (B * kept) % PAIR == 0
    ctx_row = mods.shape[1] - 1

    def coords(s, t):
        tile = s * PAIR + t
        return lax.div(tile, kept), lax.rem(tile, kept) + tile_off

    def tile_specs(width):
        def spec(t):
            def index(s):
                b, i = coords(s, t)
                return (b * nt + i, 0)
            return pl.BlockSpec((R, width), index)
        return [spec(t) for t in range(PAIR)]

    def mod_spec(t):
        def index(s):
            b, i = coords(s, t)
            return (layer, jnp.where(i < n_ctx_tiles, ctx_row, b), 0, 0)
        return pl.BlockSpec((None, 1, 1, 6 * D), index)

    tiled = (xa, y_att, d_pool, on, sgr, h)
    assert len(tiled) == N_MERGE_TILED
    flat = [a.reshape(B * T, a.shape[-1]) for a in tiled]
    out = pl.pallas_call(
        _merge_kernel,
        grid=(B * kept // PAIR,),
        in_specs=[
            *(spec for a in flat for spec in tile_specs(a.shape[-1])),
            *(mod_spec(t) for t in range(PAIR)),
            _layer_spec((1, D), layer),
            _layer_spec((ATT_DIM, D), layer), _layer_spec((POOL_DIM, POOL_DIM), layer),
            _layer_spec((1, POOL_DIM), layer), _layer_spec((POOL_DIM, D), layer),
            _layer_spec((GLA_V_DIM, D), layer), _layer_spec((D, 3 * D), layer),
            _layer_spec((D, D), layer),
        ],
        out_specs=pl.BlockSpec((PAIR * R, D), lambda s: (s, 0)),
        out_shape=jax.ShapeDtypeStruct((B * kept * R, D), F32),
        compiler_params=_params(48, 1),
        name="merge",
    )(*(a for a in flat for _ in range(PAIR)), *([mods] * PAIR), g_post, wba, pbd, ps, wbp, wbg, wmg, wo)
    return out.reshape(B, kept * R, D)


def _ffn_kernel(x_ref, xpa_ref, xna_ref, xpb_ref, xnb_ref, moda_ref, modb_ref, gpre_ref, wup_ref, cw_ref,
                cb_ref, wdn_ref, gpost_ref, o_ref, he_s, u_s, act_s, acc_s, *, n_ctx_tiles, tiles_per_seq):
    D = D_MODEL
    R = ROW_TILE
    S = SUBLANE
    G = R // S
    CK = FF_CHUNK
    n_chunks = D_FF // CK
    halo_refs = ((xpa_ref, xna_ref), (xpb_ref, xnb_ref))
    mods = (moda_ref[0], modb_ref[0])
    r_idx = lax.broadcasted_iota(jnp.int32, (R, R), 0)
    c_idx = lax.broadcasted_iota(jnp.int32, (R, R), 1)

    def seq_of(p):
        return jnp.bitwise_and(p, S - 1) * G + lax.shift_right_logical(p, SUBLANE_LOG2)

    to_perm = (c_idx == seq_of(r_idx)).astype(BF16)
    to_seq = (r_idx == seq_of(c_idx)).astype(BF16)
    sub = lax.broadcasted_iota(jnp.int32, (S, 2 * CK), 0)
    halo_row = lax.broadcasted_iota(jnp.int32, (S, 1), 0)

    def chunk_cols(ref, j):
        return jnp.concatenate([ref[:, j * CK:(j + 1) * CK], ref[:, D_FF + j * CK:D_FF + (j + 1) * CK]],
                               axis=1)

    def prologue(t):
        i = lax.rem(pl.program_id(0) * PAIR + t, tiles_per_seq)
        scale, shift = mods[t][:, 4 * D:5 * D], mods[t][:, 3 * D:4 * D]
        h = _modulated_norm(x_ref[t * R:(t + 1) * R, :], gpre_ref[...], scale, shift).astype(BF16)
        he_s[t, 0:R, :] = jnp.dot(to_perm, h, preferred_element_type=F32).astype(BF16)
        xp_ref, xn_ref = halo_refs[t]
        halo = jnp.concatenate([xp_ref[S - 1:S, :], xn_ref[0:1, :], jnp.zeros((S - 2, D), F32)], axis=0)
        seq_start = (i == 0) | (i == n_ctx_tiles)
        seq_end = (i == n_ctx_tiles - 1) | (i == tiles_per_seq - 1)
        outside = ((halo_row == 0) & seq_start) | ((halo_row == 1) & seq_end) | (halo_row >= 2)
        he_s[t, R:R + S, :] = jnp.where(
            outside, 0.0, _modulated_norm(halo, gpre_ref[...], scale, shift)).astype(BF16)

    def up(t, j):
        he = he_s[t]
        u_s[t, j % 2, :, 0:CK] = jnp.dot(he, wup_ref[:, j * CK:(j + 1) * CK], preferred_element_type=F32)
        u_s[t, j % 2, :, CK:] = jnp.dot(he, wup_ref[:, D_FF + j * CK:D_FF + (j + 1) * CK],
                                        preferred_element_type=F32)

    def finish(t, j):
        u = u_s[t, j % 2, 0:R, :]
        edge = u_s[t, j % 2, R:R + S, :]
        prev0 = jnp.where(sub == 0, edge[0:1], pltpu.roll(u[R - S:R], 1, 0))
        next_last = jnp.where(sub == S - 1, edge[1:2], pltpu.roll(u[0:S], S - 1, 0))
        prev = jnp.concatenate([prev0, u[0:R - S]], axis=0)
        nxt = jnp.concatenate([u[S:R], next_last], axis=0)
        cw = chunk_cols(cw_ref, j)
        c = prev * cw[0:1] + u * cw[1:2] + nxt * cw[2:3] + chunk_cols(cb_ref, j)
        g = c[:, CK:]
        act_s[t, :, j * CK:(j + 1) * CK] = (c[:, :CK] * (g * _sigmoid(g))).astype(BF16)

    def down(t, j0, j1):
        y = jnp.dot(act_s[t, :, j0 * CK:j1 * CK], wdn_ref[j0 * CK:j1 * CK, :], preferred_element_type=F32)
        if j0 == 0:
            acc_s[t] = y
        else:
            acc_s[t] += y

    def epilogue(t):
        acc = acc_s[t]
        acc_hi = acc.astype(BF16)
        acc_lo = (acc - acc_hi.astype(F32)).astype(BF16)
        y = (jnp.dot(to_seq, acc_hi, preferred_element_type=F32)
             + jnp.dot(to_seq, acc_lo, preferred_element_type=F32))
        ms = jnp.mean(y * y, axis=-1, keepdims=True)
        o_ref[t * R:(t + 1) * R, :] = (x_ref[t * R:(t + 1) * R, :]
                                       + mods[t][:, 5 * D:6 * D] * (y * lax.rsqrt(ms + EPS) * gpost_ref[...]))

    def body(t, after_first_up):
        up(t, 0)
        after_first_up()
        group_start = 0
        for j in range(n_chunks):
            if j + 1 < n_chunks:
                up(t, j + 1)
            finish(t, j)
            if j + 1 - group_start == FF_GROUP or j + 1 == n_chunks:
                down(t, group_start, j + 1)
                group_start = j + 1

    prologue(0)
    body(0, lambda: prologue(1))
    body(1, lambda: epilogue(0))
    epilogue(1)


def _ffn(xa, mods, g_pre, w_up, conv_w, conv_b, w_down, g_post, n_ctx_tiles, layer):
    B, T, D = xa.shape
    R = ROW_TILE
    H = SUBLANE
    nt = T // R
    assert (B * nt) % PAIR == 0 and PAIR == 2
    ctx_row = mods.shape[1] - 1
    rpb = R // H
    RE = R + H
    last_blk = B * T // H - 1

    def halo(t, after):
        def index(s):
            tile = s * PAIR + t
            return ((jnp.minimum((tile + 1) * rpb, last_blk) if after else jnp.maximum(tile * rpb - 1, 0)), 0)
        return pl.BlockSpec((H, D), index)

    def mod_spec(t):
        def index(s):
            tile = s * PAIR + t
            return (layer, jnp.where(lax.rem(tile, nt) < n_ctx_tiles, ctx_row, lax.div(tile, nt)), 0, 0)
        return pl.BlockSpec((None, 1, 1, 6 * D), index)

    x2 = xa.reshape(B * T, D)
    kern = functools.partial(_ffn_kernel, n_ctx_tiles=n_ctx_tiles, tiles_per_seq=nt)
    out = pl.pallas_call(
        kern,
        grid=(B * nt // PAIR,),
        in_specs=[
            pl.BlockSpec((PAIR * R, D), lambda s: (s, 0)),
            halo(0, False), halo(0, True), halo(1, False), halo(1, True),
            mod_spec(0), mod_spec(1),
            _layer_spec((1, D), layer),
            _layer_spec((D, 2 * D_FF), layer), _layer_spec((3, 2 * D_FF), layer),
            _layer_spec((1, 2 * D_FF), layer), _layer_spec((D_FF, D), layer),
            _layer_spec((1, D), layer),
        ],
        out_specs=pl.BlockSpec((PAIR * R, D), lambda s: (s, 0)),
        out_shape=jax.ShapeDtypeStruct((B * T, D), F32),
        scratch_shapes=[pltpu.VMEM((PAIR, RE, D), BF16),
                        pltpu.VMEM((PAIR, 2, RE, 2 * FF_CHUNK), F32),
                        pltpu.VMEM((PAIR, R, D_FF), BF16),
                        pltpu.VMEM((PAIR, R, D), F32)],
        compiler_params=_params(56, 1),
        name="ffn",
    )(x2, x2, x2, x2, x2, mods, mods, g_pre, w_up, conv_w, conv_b, w_down, g_post)
    return out.reshape(B, T, D)


def _rope_table(n_ctx, n_lat):
    rows = n_lat // GRID_W
    row = jnp.repeat(jnp.arange(rows, dtype=F32), GRID_W)
    col = jnp.tile(jnp.arange(GRID_W, dtype=F32), rows)
    n = HEAD_DIM // 4
    inv = ROPE_BASE ** (-jnp.arange(n, dtype=F32) / n)
    ang = jnp.concatenate([row[:, None] * inv, col[:, None] * inv], axis=-1)
    cos, sin = jnp.cos(ang), jnp.sin(ang)
    zero = jnp.zeros_like(sin)
    reps = LANE // HEAD_DIM
    cos_t = jnp.tile(jnp.concatenate([cos, cos], axis=-1), (1, reps))
    sin_lo = jnp.tile(jnp.concatenate([-sin, zero], axis=-1), (1, reps))
    sin_hi = jnp.tile(jnp.concatenate([zero, sin], axis=-1), (1, reps))
    lat = jnp.concatenate([cos_t, sin_lo, sin_hi], axis=-1)
    ctx = jnp.concatenate([jnp.ones((n_ctx, LANE), F32), jnp.zeros((n_ctx, 2 * LANE), F32)], axis=-1)
    return jnp.concatenate([ctx, lat], axis=0)


def kernel(x, c, ctx, c_ctx, w_ada, b_ada, g_pre_mix, g_post_mix, g_pre_ffn, g_post_ffn, w_in, att_sink, pool_w, pool_scale, gla_wa2, gla_ba, gla_norm, w_br_att, w_br_pool, w_br_gla, w_o, w_up, conv_w, conv_b, w_down):
    B, L, D = x.shape
    C = ctx.shape[1]
    depth = w_in.shape[0]
    R = ROW_TILE
    assert D == D_MODEL and C % R == 0 and L % R == 0 and L % GRID_W == 0
    n_ctx_tiles = C // R

    mod_rows = -(-(B + 1) // SUBLANE) * SUBLANE
    cc = jnp.zeros((mod_rows, D), F32).at[:B].set(c).at[mod_rows - 1].set(c_ctx)
    mods = _ada_table(cc, w_ada, b_ada).reshape(depth, mod_rows, 1, 6 * D)

    rope_tab = _rope_table(C, L)
    split = _O_GLR + 2 * GLA_GATE_RANK
    w_in_p = w_in[..., :IN_DIM_PAD].astype(BF16)
    w_mg = w_in[..., split:].astype(BF16)
    wa = jnp.zeros((depth, LANE, 2 * GLA_QK_DIM), F32)
    wa = wa.at[:, 0:GLA_GATE_RANK, 0:GLA_QK_DIM].set(gla_wa2[:, 0])
    wa = wa.at[:, GLA_GATE_RANK:2 * GLA_GATE_RANK, GLA_QK_DIM:].set(gla_wa2[:, 1])
    wa_hi = wa.astype(BF16)
    wa = jnp.concatenate([wa_hi, wa_hi, (wa - wa_hi.astype(F32)).astype(BF16)], axis=1)
    ba = gla_ba.reshape(depth, 1, 2 * GLA_QK_DIM)
    pbd = jnp.zeros((depth, POOL_DIM, POOL_DIM), F32)
    for g in range(len(POOL_WINDOWS)):
        sl = slice(g * POOL_GROUP_DIM, (g + 1) * POOL_GROUP_DIM)
        pbd = pbd.at[:, sl, sl].set(pool_w[:, g])
    pbd = pbd.astype(BF16)
    wba, wbp, wbg, wo = (w.astype(BF16) for w in (w_br_att, w_br_pool, w_br_gla, w_o))
    wup, wdn = w_up.astype(BF16), w_down.astype(BF16)

    def rows(t):
        return t.reshape(depth, 1, -1)

    g_pre_mix, g_post_mix, g_pre_ffn, g_post_ffn, pool_scale, gla_norm, conv_b = (
        rows(t) for t in (g_pre_mix, g_post_mix, g_pre_ffn, g_post_ffn, pool_scale, gla_norm, conv_b))

    xa = jnp.concatenate([ctx, x], axis=1)
    for l in range(depth):
        last = l == depth - 1
        off = n_ctx_tiles if last else 0
        q, kv, pu, gqkv, sgr, la, h = _in_proj(xa, mods, g_pre_mix, w_in_p, rope_tab, wa, ba, n_ctx_tiles, l)
        y_att = _attention(q, kv, att_sink, C, off, l)
        d_pool = _pool(pu, C)
        on = _gla(gqkv, la, gla_norm, C, l)
        xa = _merge(xa, y_att, d_pool, on, sgr, h, mods, g_post_mix, wba, pbd, pool_scale, wbp, wbg, w_mg, wo,
                    n_ctx_tiles, off, l)
        xa = _ffn(xa, mods, g_pre_ffn, wup, conv_w, conv_b, wdn, g_post_ffn, n_ctx_tiles - off, l)
    return xa
```

```python
import functools

import jax
import jax.numpy as jnp
import numpy as np
from jax import lax
from jax.experimental import pallas as pl
from jax.experimental.pallas import tpu as pltpu

F32 = jnp.float32
BF16 = jnp.bfloat16
HIGHEST = lax.Precision.HIGHEST

D_MODEL = 1024
GRID_W = 64
EPS = 1e-6

HEAD_DIM = 64
ATT_HEADS = 8
ATT_KV_HEADS = 2
ATT_GROUP = ATT_HEADS // ATT_KV_HEADS
WINDOW = 128
ROPE_BASE = 10000.0
ATT_DIM = ATT_HEADS * HEAD_DIM
KV_DIM = ATT_KV_HEADS * HEAD_DIM

POOL_WINDOWS = (2, 4, 8, 16)
POOL_GROUP_DIM = 64
POOL_DIM = len(POOL_WINDOWS) * POOL_GROUP_DIM
POOL_PAD = 16

GLA_HEADS = 4
GLA_DK = 32
GLA_DV = 64
GLA_GATE_RANK = 16
GLA_TAU = 16.0
GLA_QK_DIM = GLA_HEADS * GLA_DK
GLA_V_DIM = GLA_HEADS * GLA_DV
GLA_CHUNK = 64

D_FF = 2816
FF_CHUNK = 256
FF_GROUP = 4
PAIR = 2

LANE = 128
SUBLANE = 8
SUBLANE_LOG2 = SUBLANE.bit_length() - 1
ROW_TILE = 256

_GLR_PAD = LANE - 2 * GLA_GATE_RANK
_O_AQ = 0
_O_AK = _O_AQ + ATT_DIM
_O_AV = _O_AK + KV_DIM
_O_PU = _O_AV + KV_DIM
_O_GQ = _O_PU + POOL_DIM
_O_GK = _O_GQ + GLA_QK_DIM
_O_GV = _O_GK + GLA_QK_DIM
_O_GR = _O_GV + GLA_V_DIM
_O_GLR = _O_GR + GLA_V_DIM
IN_DIM_PAD = _O_GLR + LANE

NEG_BIG = -1e30
LOG2E = float(np.log2(np.e))


def _params(vmem_mb, n_axes):
    return pltpu.CompilerParams(
        dimension_semantics=("arbitrary",) * n_axes,
        vmem_limit_bytes=vmem_mb * 1024 * 1024,
    )


def _resident(shape):
    nd = len(shape)
    return pl.BlockSpec(shape, lambda *_: (0,) * nd, pipeline_mode=pl.Buffered(1))


def _layer_spec(shape, layer):
    nd = len(shape)
    return pl.BlockSpec((None,) + tuple(shape), lambda *_: (layer,) + (0,) * nd,
                        pipeline_mode=pl.Buffered(1))


def _mod_spec(layer, ctx_row, is_ctx_tile):
    return pl.BlockSpec((None, 1, 1, 6 * D_MODEL),
                        lambda b, i: (layer, jnp.where(is_ctx_tile(i), ctx_row, b), 0, 0))


def _sigmoid(v):
    return 1.0 / (1.0 + jnp.exp(-v))


def _iota_div(shape, axis, divisor):
    shift = int(np.log2(divisor))
    assert 1 << shift == divisor
    return lax.shift_right_logical(lax.broadcasted_iota(jnp.int32, shape, axis), shift)


def _shifted_rows(ext, shift, n_rows):
    if shift:
        ext = pltpu.roll(ext, (-shift) % ext.shape[0], 0)
    return ext[SUBLANE:SUBLANE + n_rows]


def _modulated_norm(x, gain, scale, shift):
    ms = jnp.mean(x * x, axis=-1, keepdims=True)
    return (x * lax.rsqrt(ms + EPS) * gain) * (1.0 + scale) + shift


def _ada_kernel(c_ref, w_ref, b_ref, o_ref):
    c = c_ref[...]
    a = c * _sigmoid(c)
    o_ref[0] = jnp.dot(a, w_ref[0], precision=HIGHEST, preferred_element_type=F32) + b_ref[0]


def _ada_table(cc, w_ada, b_ada):
    depth = w_ada.shape[0]
    rows = cc.shape[0]
    width = 2 * D_MODEL
    n_col = w_ada.shape[2] // width
    return pl.pallas_call(
        _ada_kernel,
        grid=(depth, n_col),
        in_specs=[
            pl.BlockSpec((rows, D_MODEL), lambda l, j: (0, 0)),
            pl.BlockSpec((1, D_MODEL, width), lambda l, j: (l, 0, j)),
            pl.BlockSpec((1, 1, width), lambda l, j: (l, 0, j)),
        ],
        out_specs=pl.BlockSpec((1, rows, width), lambda l, j: (l, 0, j)),
        out_shape=jax.ShapeDtypeStruct((depth, rows, w_ada.shape[2]), F32),
        compiler_params=_params(32, 2),
        name="ada_table",
    )(cc, w_ada, b_ada.reshape(depth, 1, -1))


def _in_proj_kernel(x_ref, moda_ref, modb_ref, g_ref, w_ref, ropea_ref, ropeb_ref, wa_ref, ba_ref,
                    q_ref, kv_ref, pu_ref, gqkv_ref, sgr_ref, la_ref, h_ref):
    D = D_MODEL
    R = ROW_TILE
    mods = (moda_ref[0], modb_ref[0])
    ropes = (ropea_ref, ropeb_ref)

    def norm(t):
        rows = pl.ds(t * R, R)
        h_ref[rows, :] = _modulated_norm(x_ref[rows, :], g_ref[...], mods[t][:, D:2 * D],
                                         mods[t][:, 0:D]).astype(BF16)

    def project(t, after_first):
        rows = pl.ds(t * R, R)
        h = h_ref[rows, :]
        rope_ref = ropes[t]

        def proj(lo, width):
            return jnp.dot(h, w_ref[:, lo:lo + width], preferred_element_type=F32)

        cosf = rope_ref[:, 0:LANE]
        sin_lo = rope_ref[:, LANE:2 * LANE]
        sin_hi = rope_ref[:, 2 * LANE:3 * LANE]

        def rope(v):
            return (v * cosf + pltpu.roll(v, LANE - HEAD_DIM // 2, 1) * sin_lo
                    + pltpu.roll(v, HEAD_DIM // 2, 1) * sin_hi)

        aq = proj(_O_AQ, ATT_DIM)
        after_first()
        for j in range(ATT_DIM // LANE):
            q_ref[rows, j * LANE:(j + 1) * LANE] = (
                rope(aq[:, j * LANE:(j + 1) * LANE]) * (HEAD_DIM ** -0.5 * LOG2E)).astype(BF16)
        kv_ref[rows, 0:KV_DIM] = rope(proj(_O_AK, KV_DIM)).astype(BF16)
        kv_ref[rows, KV_DIM:2 * KV_DIM] = proj(_O_AV, KV_DIM).astype(BF16)

        pu_ref[rows, :] = proj(_O_PU, POOL_DIM)

        gqkv_ref[rows, 0:GLA_QK_DIM] = (proj(_O_GQ, GLA_QK_DIM) * (GLA_DK ** -0.5)).astype(BF16)
        gqkv_ref[rows, GLA_QK_DIM:] = proj(_O_GK, GLA_QK_DIM + GLA_V_DIM).astype(BF16)
        gr = proj(_O_GR, GLA_V_DIM)
        sgr_ref[rows, :] = (gr * _sigmoid(gr)).astype(BF16)

        glr = proj(_O_GLR, LANE)
        glr_hi = glr.astype(BF16)
        glr_lo = (glr - glr_hi.astype(F32)).astype(BF16)
        z = jnp.dot(jnp.concatenate([glr_hi, glr_lo, glr_hi], axis=1), wa_ref[...],
                    preferred_element_type=F32) + ba_ref[...]
        la_ref[rows, :] = (jnp.minimum(z, 0.0) - jnp.log(1.0 + jnp.exp(-jnp.abs(z)))) * (1.0 / GLA_TAU)

    norm(0)
    project(0, lambda: norm(1))
    project(1, lambda: None)


def _tile_of_pair(t, tiles_per_seq, n_ctx_tiles, ctx_row, layer):
    def seq_tile(s):
        return lax.rem(s * PAIR + t, tiles_per_seq)

    def mod_index(s):
        tile = s * PAIR + t
        return (layer, jnp.where(lax.rem(tile, tiles_per_seq) < n_ctx_tiles, ctx_row,
                                 lax.div(tile, tiles_per_seq)), 0, 0)

    return seq_tile, pl.BlockSpec((None, 1, 1, 6 * D_MODEL), mod_index)


def _in_proj(xa, mods, g_pre, w_in, rope_tab, wa, ba, n_ctx_tiles, layer):
    B, T, D = xa.shape
    R = ROW_TILE
    nt = T // R
    assert (B * nt) % PAIR == 0
    ctx_row = mods.shape[1] - 1

    def pair(width):
        return pl.BlockSpec((PAIR * R, width), lambda s: (s, 0))

    seq_tiles, mod_specs = zip(*(_tile_of_pair(t, nt, n_ctx_tiles, ctx_row, layer) for t in range(PAIR)))
    rope_specs = [pl.BlockSpec((R, 3 * LANE), lambda s, f=f: (f(s), 0)) for f in seq_tiles]
    widths = (ATT_DIM, 2 * KV_DIM, POOL_DIM, 2 * GLA_QK_DIM + GLA_V_DIM, GLA_V_DIM,
              2 * GLA_QK_DIM, D)
    dtypes = (BF16, BF16, F32, BF16, BF16, F32, BF16)
    outs = pl.pallas_call(
        _in_proj_kernel,
        grid=(B * nt // PAIR,),
        in_specs=[
            pair(D), *mod_specs,
            _layer_spec((1, D), layer),
            _layer_spec((D, IN_DIM_PAD), layer),
            *rope_specs,
            _layer_spec((3 * LANE, 2 * GLA_QK_DIM), layer),
            _layer_spec((1, 2 * GLA_QK_DIM), layer),
        ],
        out_specs=[pair(w) for w in widths],
        out_shape=[jax.ShapeDtypeStruct((B * T, w), dt) for w, dt in zip(widths, dtypes)],
        compiler_params=_params(48, 1),
        name="in_proj",
    )(xa.reshape(B * T, D), mods, mods, g_pre, w_in, rope_tab, rope_tab, wa, ba)
    return [o.reshape(B, T, -1) for o in outs]


def _attn_kernel(sink_ref, q_ref, kv_ref, o_ref, s_s, *, layer, tile_off, n_ctx_tiles, n_ctx, n_lat):
    R = ROW_TILE
    W = WINDOW
    assert R == 2 * W == 2 * LANE
    n_win = R // W + 2
    i = pl.program_id(1) + tile_off
    n_blk = (n_ctx + n_lat) // W
    nt_dims = (((1,), (1,)), ((), ()))
    q = q_ref[0]

    def heads(kv, split, merge):
        n_keys = kv.shape[0]
        ks, v_ext_ts = [], []
        for g in range(ATT_KV_HEADS):
            ks.append(kv[:, g * HEAD_DIM:(g + 1) * HEAD_DIM])
            v = kv[:, KV_DIM + g * HEAD_DIM:KV_DIM + (g + 1) * HEAD_DIM].astype(F32)
            v_ext_ts.append(
                jnp.concatenate([v, jnp.ones((n_keys, LANE - HEAD_DIM), F32)], axis=1).T.astype(BF16))

        n_slots = s_s.shape[0]

        def scores(h):
            qh = q[:, h * HEAD_DIM:(h + 1) * HEAD_DIM]
            s_s[h % n_slots, 0:n_keys, :] = lax.dot_general(ks[h // ATT_GROUP], qh, nt_dims,
                                                             preferred_element_type=F32)

        n_at_once = n_slots // 2
        for h in range(n_at_once):
            scores(h)
        for h0 in range(0, ATT_HEADS, n_at_once):
            batch = tuple(range(h0, h0 + n_at_once))
            for h in batch:
                if h + n_at_once < ATT_HEADS:
                    scores(h + n_at_once)
            sinks = [sink_ref[layer, h] * LOG2E for h in batch]
            halves = [split(s_s.at[h % n_slots]) for h in batch]
            ms = [[jnp.maximum(jnp.max(s_half, axis=0, keepdims=True), sk) for s_half in hs]
                  for hs, sk in zip(halves, sinks)]
            ps = [[jnp.exp2(s_half - m).astype(BF16) for s_half, m in zip(hs, mh)]
                  for hs, mh in zip(halves, ms)]
            o_exts = [jnp.dot(v_ext_ts[h // ATT_GROUP], merge(p), preferred_element_type=F32)
                      for h, p in zip(batch, ps)]
            out_t = [o_ext[0:HEAD_DIM] / (o_ext[HEAD_DIM:HEAD_DIM + 1]
                                          + jnp.exp2(sk - jnp.concatenate(mh, axis=1)))
                     for o_ext, sk, mh in zip(o_exts, sinks, ms)]
            for j in range(0, n_at_once, 2):
                o_ref[0, :, (h0 + j) * HEAD_DIM:(h0 + j + 2) * HEAD_DIM] = (
                    jnp.concatenate(out_t[j:j + 2], axis=0).T.astype(BF16))

    @pl.when(i < n_ctx_tiles)
    def _context_queries():
        heads(kv_ref[0, 0:n_ctx, :],
              lambda ref: [ref[0:n_ctx, a * LANE:(a + 1) * LANE] for a in range(R // LANE)],
              lambda ps: jnp.concatenate(ps, axis=1))

    @pl.when(i >= n_ctx_tiles)
    def _latent_queries():
        first_blk = i * (R // W) - 1
        parts = []
        for blk in range(n_win):
            idx = jnp.clip(first_blk + blk, 0, n_blk - 1)
            parts.append(kv_ref[0, pl.ds(pl.multiple_of(idx * W, W), W), :])
        parts.append(kv_ref[0, 0:n_ctx, :])
        n_loc = n_win * W
        kj = lax.broadcasted_iota(jnp.int32, (W, LANE), 0)
        qi = lax.broadcasted_iota(jnp.int32, (W, LANE), 1)
        below = jnp.where(kj >= qi, 0.0, NEG_BIG).astype(F32)
        above = jnp.where(kj <= qi, 0.0, NEG_BIG).astype(F32)
        below_first = below + jnp.where(i == n_ctx_tiles, NEG_BIG, 0.0)
        above_last = above + jnp.where(i == n_blk // (R // W) - 1, NEG_BIG, 0.0)

        def split(ref):
            def blk(b, a):
                return ref[b * W:(b + 1) * W, a * LANE:(a + 1) * LANE]
            return [jnp.concatenate([blk(0, 0) + below_first, blk(1, 0), blk(2, 0) + above,
                                     ref[n_loc:n_loc + n_ctx, 0:LANE]], axis=0),
                    jnp.concatenate([blk(1, 1) + below, blk(2, 1), blk(3, 1) + above_last,
                                     ref[n_loc:n_loc + n_ctx, LANE:2 * LANE]], axis=0)]

        def merge(ps):
            zero = jnp.zeros((W, LANE), BF16)
            live = (n_win - 1) * W
            return jnp.concatenate(
                [jnp.concatenate([ps[0][0:live], zero, ps[0][live:]], axis=0),
                 jnp.concatenate([zero, ps[1]], axis=0)], axis=1)

        heads(jnp.concatenate(parts, axis=0), split, merge)


def _attention(q, kv, sink, n_ctx, tile_off, layer):
    B, T, _ = q.shape
    R = ROW_TILE
    nt = T // R
    kern = functools.partial(_attn_kernel, layer=layer, tile_off=tile_off, n_ctx_tiles=n_ctx // R,
                             n_ctx=n_ctx, n_lat=T - n_ctx)
    return pl.pallas_call(
        kern,
        grid=(B, nt - tile_off),
        in_specs=[
            pl.BlockSpec(memory_space=pltpu.SMEM),
            pl.BlockSpec((1, R, ATT_DIM), lambda b, i: (b, i + tile_off, 0)),
            pl.BlockSpec((1, T, 2 * KV_DIM), lambda b, i: (b, 0, 0)),
        ],
        out_specs=pl.BlockSpec((1, R, ATT_DIM), lambda b, i: (b, i + tile_off, 0)),
        out_shape=jax.ShapeDtypeStruct((B, T, ATT_DIM), BF16),
        scratch_shapes=[pltpu.VMEM((2 * ATT_GROUP, R + 2 * WINDOW + n_ctx, R), F32)],
        compiler_params=_params(48, 2),
        name="attention",
    )(sink, q, kv)


def _pool_kernel(u_ref, o_ref, p0, p2, p4, p8, *, n_ctx, n_lat):
    R = ROW_TILE
    T = n_ctx + n_lat
    RE = R + 2 * SUBLANE
    off_ctx = POOL_PAD
    off_lat = 2 * POOL_PAD
    n_comp = p0.shape[0] // R - 1

    read_end = n_comp * R + 2 * SUBLANE
    for lo_row, hi_row in ((0, off_ctx), (off_ctx + n_ctx, n_ctx + off_lat), (T + off_lat, read_end)):
        p0[lo_row:hi_row, :] = jnp.zeros((hi_row - lo_row, POOL_DIM), F32)
    for buf in (p2, p4, p8):
        buf[0:SUBLANE, :] = jnp.zeros((SUBLANE, POOL_DIM), F32)
        buf[read_end - SUBLANE:read_end, :] = jnp.zeros((SUBLANE, POOL_DIM), F32)
    p0[off_ctx:off_ctx + n_ctx, :] = u_ref[0, 0:n_ctx, :]
    p0[n_ctx + off_lat:T + off_lat, :] = u_ref[0, n_ctx:T, :]

    def stage(src, dst, back, fwd):
        def body(c, carry):
            r = pl.multiple_of(c * R, R)
            ext = src[pl.ds(r, RE), :]
            dst[pl.ds(r + SUBLANE, R), :] = _shifted_rows(ext, -back, R) + _shifted_rows(ext, fwd, R)
            return carry
        lax.fori_loop(0, n_comp, body, 0)

    stage(p0, p2, 1, 0)
    stage(p2, p4, 1, 1)
    stage(p4, p8, 2, 2)

    lane_group = _iota_div((R, POOL_DIM), 1, POOL_GROUP_DIM)
    row = lax.broadcasted_iota(jnp.int32, (R, POOL_DIM), 0)
    lo = jnp.full((R, POOL_DIM), POOL_WINDOWS[0] // 2, jnp.int32)
    hi = jnp.full((R, POOL_DIM), POOL_WINDOWS[0] - POOL_WINDOWS[0] // 2 - 1, jnp.int32)
    for g in range(1, len(POOL_WINDOWS)):
        w = POOL_WINDOWS[g]
        lo = jnp.where(lane_group == g, w // 2, lo)
        hi = jnp.where(lane_group == g, w - w // 2 - 1, hi)

    def emit(i, carry):
        is_ctx = i < n_ctx // R
        r0 = pl.multiple_of(i * R, R)
        r = pl.multiple_of(r0 + jnp.where(is_ctx, off_ctx, off_lat), SUBLANE)
        pos = row + jnp.where(is_ctx, r0, r0 - n_ctx)
        seq_len = jnp.where(is_ctx, n_ctx, n_lat)
        ext8 = p8[pl.ds(r - SUBLANE, RE), :]
        sums = (p2[pl.ds(r, R), :], p4[pl.ds(r, R), :], ext8[SUBLANE:SUBLANE + R],
                _shifted_rows(ext8, -4, R) + _shifted_rows(ext8, 4, R))
        tot = sums[0]
        for g in range(1, len(POOL_WINDOWS)):
            tot = jnp.where(lane_group == g, sums[g], tot)
        cnt = (jnp.minimum(pos + hi + 1, seq_len) - jnp.maximum(pos - lo, 0)).astype(F32)
        o_ref[0, pl.ds(r0, R), :] = (tot / cnt - u_ref[0, pl.ds(r0, R), :]).astype(BF16)
        return carry

    lax.fori_loop(0, T // R, emit, 0)


def _pool(pu, n_ctx):
    B, T, _ = pu.shape
    R = ROW_TILE
    rows = (pl.cdiv(T + 3 * POOL_PAD, R) + 1) * R
    kern = functools.partial(_pool_kernel, n_ctx=n_ctx, n_lat=T - n_ctx)
    return pl.pallas_call(
        kern,
        grid=(B,),
        in_specs=[pl.BlockSpec((1, T, POOL_DIM), lambda b: (b, 0, 0))],
        out_specs=pl.BlockSpec((1, T, POOL_DIM), lambda b: (b, 0, 0)),
        out_shape=jax.ShapeDtypeStruct((B, T, POOL_DIM), BF16),
        scratch_shapes=[pltpu.VMEM((rows, POOL_DIM), F32)] * 4,
        compiler_params=_params(48, 1),
        name="pool",
    )(pu)


def _gla_kernel(qkv_ref, la_ref, gn_ref, o_ref, of_s, ob_s, stf, stb, *, n_ctx):
    CH = GLA_CHUNK
    R = ROW_TILE
    NC = R // CH
    T = of_s.shape[0]
    n_tiles = T // R
    n_ctx_tiles = n_ctx // R
    QK = GLA_QK_DIM
    DV = GLA_V_DIM
    H = GLA_HEADS

    stf[...] = jnp.zeros(stf.shape, F32)
    stb[...] = jnp.zeros(stb.shape, F32)

    same_chunk = _iota_div((R, R), 0, CH) == _iota_div((R, R), 1, CH)
    ri = lax.broadcasted_iota(jnp.int32, (R, R), 0)
    ci = lax.broadcasted_iota(jnp.int32, (R, R), 1)
    tri_lo = (same_chunk & (ci <= ri)).astype(BF16)
    tri_up = (same_chunk & (ci >= ri)).astype(BF16)
    rs = jnp.bitwise_and(lax.broadcasted_iota(jnp.int32, (H * R, R), 0), R - 1)
    cs = lax.broadcasted_iota(jnp.int32, (H * R, R), 1)
    same_chunk_h = (lax.shift_right_logical(rs, CH.bit_length() - 1)
                    == lax.shift_right_logical(cs, CH.bit_length() - 1))
    keep_lo = same_chunk_h & (cs <= rs)
    keep_up = same_chunk_h & (cs >= rs)
    q_head = _iota_div((H * R, QK), 0, R) == _iota_div((H * R, QK), 1, GLA_DK)
    o_head = _iota_div((R, DV), 1, GLA_DV)
    st_keep = _iota_div((DV, QK), 0, GLA_DV) == _iota_div((DV, QK), 1, GLA_DK)
    nt_dims = (((1,), (1,)), ((), ()))
    tn_dims = (((0,), (0,)), ((), ()))

    def split3(a):
        a1 = a.astype(BF16)
        r1 = a - a1.astype(F32)
        a2 = r1.astype(BF16)
        a3 = (r1 - a2.astype(F32)).astype(BF16)
        return jnp.concatenate([a1, a2, a3], axis=1)

    def rows_of_chunks(cum, offset):
        picks = [cum[c * CH + offset:c * CH + offset + 1, :] for c in range(NC)]
        full = jnp.concatenate([jnp.broadcast_to(p, (CH, QK)) for p in picks], axis=0)
        return picks, full

    def load(t, col0, tri):
        r0 = pl.multiple_of(t * R, R)
        a = la_ref[0, pl.ds(r0, R), col0:col0 + QK]
        c3 = jnp.dot(tri, split3(a), preferred_element_type=F32)
        return dict(r0=r0, cum=c3[:, 0:QK] + c3[:, QK:2 * QK] + c3[:, 2 * QK:3 * QK],
                    q=qkv_ref[0, pl.ds(r0, R), 0:QK].astype(F32),
                    k=qkv_ref[0, pl.ds(r0, R), QK:2 * QK].astype(F32),
                    v=qkv_ref[0, pl.ds(r0, R), 2 * QK:])

    def scale(d, last):
        cum, q, k = d["cum"], d["q"], d["k"]
        tots, tot_b = rows_of_chunks(cum, last)
        _, mid_b = rows_of_chunks(cum, CH // 2)
        d["dec"] = [jnp.exp(t) for t in tots]
        d["qe"] = (q * jnp.exp(cum)).astype(BF16)
        qm = q * jnp.exp(cum - mid_b)
        d["qs"] = jnp.where(q_head, jnp.concatenate([qm] * H, axis=0), 0.0).astype(BF16)
        d["km"] = (k * jnp.exp(mid_b - cum)).astype(BF16)
        d["kl"] = (k * jnp.exp(tot_b - cum)).astype(BF16)

    def scores(d, keep):
        att = lax.dot_general(d["qs"], d["km"], nt_dims, preferred_element_type=F32)
        d["att"] = jnp.where(keep, att, 0.0).astype(BF16)
        d["kv"] = [lax.dot_general(d["v"][c * CH:(c + 1) * CH], d["kl"][c * CH:(c + 1) * CH], tn_dims,
                                   preferred_element_type=F32) for c in range(NC)]

    def intra(d):
        o_all = jnp.dot(d["att"], d["v"], preferred_element_type=F32)
        o = o_all[0:R]
        for hd in range(1, H):
            o = jnp.where(o_head == hd, o_all[hd * R:(hd + 1) * R], o)
        d["o"] = o

    def inter(d, order, st_ref, o_scr):
        state = st_ref[...]
        parts = [None] * NC
        for c in order:
            parts[c] = lax.dot_general(d["qe"][c * CH:(c + 1) * CH], state.astype(BF16), nt_dims,
                                       preferred_element_type=F32)
            state = state * d["dec"][c] + jnp.where(st_keep, d["kv"][c], 0.0)
        st_ref[...] = state
        o_scr[pl.ds(d["r0"], R), :] = d["o"] + jnp.concatenate(parts, axis=0)

    def step(s, carry):
        tb = jnp.where(s < n_ctx_tiles, n_ctx_tiles - 1 - s, n_tiles - 1 - (s - n_ctx_tiles))
        f = load(s, 0, tri_lo)
        b = load(tb, QK, tri_up)
        scale(f, CH - 1)
        scale(b, 0)
        scores(f, keep_lo)
        scores(b, keep_up)
        intra(f)
        intra(b)
        inter(f, range(NC), stf, of_s)
        inter(b, range(NC - 1, -1, -1), stb, ob_s)
        return carry

    lax.fori_loop(0, n_tiles, step, 0)

    gmean = (_iota_div((DV, DV), 0, GLA_DV) == _iota_div((DV, DV), 1, GLA_DV)).astype(BF16) * (1.0 / GLA_DV)
    gn = gn_ref[...]

    def finish(i, carry):
        r0 = pl.multiple_of(i * R, R)
        o = of_s[pl.ds(r0, R), :] + ob_s[pl.ds(r0, R), :]
        oo = o * o
        hi = oo.astype(BF16)
        lo = (oo - hi.astype(F32)).astype(BF16)
        ms = jnp.dot(hi, gmean, preferred_element_type=F32) + jnp.dot(lo, gmean, preferred_element_type=F32)
        o_ref[0, pl.ds(r0, R), :] = (o * lax.rsqrt(ms + EPS) * gn).astype(BF16)
        return carry

    lax.fori_loop(0, n_tiles, finish, 0)


def _gla(gqkv, la, gla_norm, n_ctx, layer):
    B, T, _ = gqkv.shape
    kern = functools.partial(_gla_kernel, n_ctx=n_ctx)
    return pl.pallas_call(
        kern,
        grid=(B,),
        in_specs=[
            pl.BlockSpec((1, T, 2 * GLA_QK_DIM + GLA_V_DIM), lambda b: (b, 0, 0)),
            pl.BlockSpec((1, T, 2 * GLA_QK_DIM), lambda b: (b, 0, 0)),
            _layer_spec((1, GLA_V_DIM), layer),
        ],
        out_specs=pl.BlockSpec((1, T, GLA_V_DIM), lambda b: (b, 0, 0)),
        out_shape=jax.ShapeDtypeStruct((B, T, GLA_V_DIM), BF16),
        scratch_shapes=[pltpu.VMEM((T, GLA_V_DIM), F32), pltpu.VMEM((T, GLA_V_DIM), F32),
                        pltpu.VMEM((GLA_V_DIM, GLA_QK_DIM), F32),
                        pltpu.VMEM((GLA_V_DIM, GLA_QK_DIM), F32)],
        compiler_params=_params(48, 1),
        name="gla",
    )(gqkv, la, gla_norm)


N_MERGE_TILED = 6


def _merge_kernel(*refs):
    D = D_MODEL
    R = ROW_TILE
    tiled = refs[:PAIR * N_MERGE_TILED]
    mod_refs = refs[PAIR * N_MERGE_TILED:PAIR * N_MERGE_TILED + PAIR]
    (gpost_ref, wba_ref, pbd_ref, ps_ref, wbp_ref, wbg_ref, wmg_ref, wo_ref,
     o_ref) = refs[PAIR * N_MERGE_TILED + PAIR:]

    for t in range(PAIR):
        x_ref, att_ref, dp_ref, on_ref, sgr_ref, h_ref = tiled[t::PAIR]
        h = h_ref[...]

        def gate(j):
            return _sigmoid(jnp.dot(h, wmg_ref[:, j * D:(j + 1) * D], preferred_element_type=F32))

        m = gate(0) * jnp.dot(att_ref[...], wba_ref[...], preferred_element_type=F32)
        yp = jnp.dot(dp_ref[...], pbd_ref[...], preferred_element_type=F32) * ps_ref[...]
        m = m + gate(1) * jnp.dot(yp.astype(BF16), wbp_ref[...], preferred_element_type=F32)
        m = m + gate(2) * jnp.dot(on_ref[...] * sgr_ref[...], wbg_ref[...], preferred_element_type=F32)
        y = jnp.dot(m.astype(BF16), wo_ref[...], preferred_element_type=F32)
        ms = jnp.mean(y * y, axis=-1, keepdims=True)
        o_ref[t * R:(t + 1) * R, :] = (
            x_ref[...] + mod_refs[t][0][:, 2 * D:3 * D] * (y * lax.rsqrt(ms + EPS) * gpost_ref[...]))


def _merge(xa, y_att, d_pool, on, sgr, h, mods, g_post, wba, pbd, ps, wbp, wbg, wmg, wo,
           n_ctx_tiles, tile_off, layer):
    B, T, D = xa.shape
    R = ROW_TILE
    nt = T // R
    kept = nt - tile_off
    assert (B * kept) % PAIR == 0
    ctx_row = mods.shape[1] - 1

    def coords(s, t):
        tile = s * PAIR + t
        return lax.div(tile, kept), lax.rem(tile, kept) + tile_off

    def tile_specs(width):
        def spec(t):
            def index(s):
                b, i = coords(s, t)
                return (b * nt + i, 0)
            return pl.BlockSpec((R, width), index)
        return [spec(t) for t in range(PAIR)]

    def mod_spec(t):
        def index(s):
            b, i = coords(s, t)
            return (layer, jnp.where(i < n_ctx_tiles, ctx_row, b), 0, 0)
        return pl.BlockSpec((None, 1, 1, 6 * D), index)

    tiled = (xa, y_att, d_pool, on, sgr, h)
    assert len(tiled) == N_MERGE_TILED
    flat = [a.reshape(B * T, a.shape[-1]) for a in tiled]
    out = pl.pallas_call(
        _merge_kernel,
        grid=(B * kept // PAIR,),
        in_specs=[
            *(spec for a in flat for spec in tile_specs(a.shape[-1])),
            *(mod_spec(t) for t in range(PAIR)),
            _layer_spec((1, D), layer),
            _layer_spec((ATT_DIM, D), layer), _layer_spec((POOL_DIM, POOL_DIM), layer),
            _layer_spec((1, POOL_DIM), layer), _layer_spec((POOL_DIM, D), layer),
            _layer_spec((GLA_V_DIM, D), layer), _layer_spec((D, 3 * D), layer),
            _layer_spec((D, D), layer),
        ],
        out_specs=pl.BlockSpec((PAIR * R, D), lambda s: (s, 0)),
        out_shape=jax.ShapeDtypeStruct((B * kept * R, D), F32),
        compiler_params=_params(48, 1),
        name="merge",
    )(*(a for a in flat for _ in range(PAIR)), *([mods] * PAIR), g_post, wba, pbd, ps, wbp, wbg, wmg, wo)
    return out.reshape(B, kept * R, D)


def _ffn_kernel(x_ref, xpa_ref, xna_ref, xpb_ref, xnb_ref, moda_ref, modb_ref, gpre_ref, wup_ref, cw_ref,
                cb_ref, wdn_ref, gpost_ref, o_ref, he_s, u_s, act_s, acc_s, *, n_ctx_tiles, tiles_per_seq):
    D = D_MODEL
    R = ROW_TILE
    S = SUBLANE
    G = R // S
    CK = FF_CHUNK
    n_chunks = D_FF // CK
    halo_refs = ((xpa_ref, xna_ref), (xpb_ref, xnb_ref))
    mods = (moda_ref[0], modb_ref[0])
    r_idx = lax.broadcasted_iota(jnp.int32, (R, R), 0)
    c_idx = lax.broadcasted_iota(jnp.int32, (R, R), 1)

    def seq_of(p):
        return jnp.bitwise_and(p, S - 1) * G + lax.shift_right_logical(p, SUBLANE_LOG2)

    to_perm = (c_idx == seq_of(r_idx)).astype(BF16)
    to_seq = (r_idx == seq_of(c_idx)).astype(BF16)
    sub = lax.broadcasted_iota(jnp.int32, (S, 2 * CK), 0)
    halo_row = lax.broadcasted_iota(jnp.int32, (S, 1), 0)

    def chunk_cols(ref, j):
        return jnp.concatenate([ref[:, j * CK:(j + 1) * CK], ref[:, D_FF + j * CK:D_FF + (j + 1) * CK]],
                               axis=1)

    def prologue(t):
        i = lax.rem(pl.program_id(0) * PAIR + t, tiles_per_seq)
        scale, shift = mods[t][:, 4 * D:5 * D], mods[t][:, 3 * D:4 * D]
        h = _modulated_norm(x_ref[t * R:(t + 1) * R, :], gpre_ref[...], scale, shift).astype(BF16)
        he_s[t, 0:R, :] = jnp.dot(to_perm, h, preferred_element_type=F32).astype(BF16)
        xp_ref, xn_ref = halo_refs[t]
        halo = jnp.concatenate([xp_ref[S - 1:S, :], xn_ref[0:1, :], jnp.zeros((S - 2, D), F32)], axis=0)
        seq_start = (i == 0) | (i == n_ctx_tiles)
        seq_end = (i == n_ctx_tiles - 1) | (i == tiles_per_seq - 1)
        outside = ((halo_row == 0) & seq_start) | ((halo_row == 1) & seq_end) | (halo_row >= 2)
        he_s[t, R:R + S, :] = jnp.where(
            outside, 0.0, _modulated_norm(halo, gpre_ref[...], scale, shift)).astype(BF16)

    def up(t, j):
        he = he_s[t]
        u_s[t, j % 2, :, 0:CK] = jnp.dot(he, wup_ref[:, j * CK:(j + 1) * CK], preferred_element_type=F32)
        u_s[t, j % 2, :, CK:] = jnp.dot(he, wup_ref[:, D_FF + j * CK:D_FF + (j + 1) * CK],
                                        preferred_element_type=F32)

    def finish(t, j):
        u = u_s[t, j % 2, 0:R, :]
        edge = u_s[t, j % 2, R:R + S, :]
        prev0 = jnp.where(sub == 0, edge[0:1], pltpu.roll(u[R - S:R], 1, 0))
        next_last = jnp.where(sub == S - 1, edge[1:2], pltpu.roll(u[0:S], S - 1, 0))
        prev = jnp.concatenate([prev0, u[0:R - S]], axis=0)
        nxt = jnp.concatenate([u[S:R], next_last], axis=0)
        cw = chunk_cols(cw_ref, j)
        c = prev * cw[0:1] + u * cw[1:2] + nxt * cw[2:3] + chunk_cols(cb_ref, j)
        g = c[:, CK:]
        act_s[t, :, j * CK:(j + 1) * CK] = (c[:, :CK] * (g * _sigmoid(g))).astype(BF16)

    def down(t, j0, j1):
        y = jnp.dot(act_s[t, :, j0 * CK:j1 * CK], wdn_ref[j0 * CK:j1 * CK, :], preferred_element_type=F32)
        if j0 == 0:
            acc_s[t] = y
        else:
            acc_s[t] += y

    def epilogue(t):
        acc = acc_s[t]
        acc_hi = acc.astype(BF16)
        acc_lo = (acc - acc_hi.astype(F32)).astype(BF16)
        y = (jnp.dot(to_seq, acc_hi, preferred_element_type=F32)
             + jnp.dot(to_seq, acc_lo, preferred_element_type=F32))
        ms = jnp.mean(y * y, axis=-1, keepdims=True)
        o_ref[t * R:(t + 1) * R, :] = (x_ref[t * R:(t + 1) * R, :]
                                       + mods[t][:, 5 * D:6 * D] * (y * lax.rsqrt(ms + EPS) * gpost_ref[...]))

    def body(t, after_first_up):
        up(t, 0)
        after_first_up()
        group_start = 0
        for j in range(n_chunks):
            if j + 1 < n_chunks:
                up(t, j + 1)
            finish(t, j)
            if j + 1 - group_start == FF_GROUP or j + 1 == n_chunks:
                down(t, group_start, j + 1)
                group_start = j + 1

    prologue(0)
    body(0, lambda: prologue(1))
    body(1, lambda: epilogue(0))
    epilogue(1)


def _ffn(xa, mods, g_pre, w_up, conv_w, conv_b, w_down, g_post, n_ctx_tiles, layer):
    B, T, D = xa.shape
    R = ROW_TILE
    H = SUBLANE
    nt = T // R
    assert (B * nt) % PAIR == 0 and PAIR == 2
    ctx_row = mods.shape[1] - 1
    rpb = R // H
    RE = R + H
    last_blk = B * T // H - 1

    def halo(t, after):
        def index(s):
            tile = s * PAIR + t
            return ((jnp.minimum((tile + 1) * rpb, last_blk) if after else jnp.maximum(tile * rpb - 1, 0)), 0)
        return pl.BlockSpec((H, D), index)

    def mod_spec(t):
        def index(s):
            tile = s * PAIR + t
            return (layer, jnp.where(lax.rem(tile, nt) < n_ctx_tiles, ctx_row, lax.div(tile, nt)), 0, 0)
        return pl.BlockSpec((None, 1, 1, 6 * D), index)

    x2 = xa.reshape(B * T, D)
    kern = functools.partial(_ffn_kernel, n_ctx_tiles=n_ctx_tiles, tiles_per_seq=nt)
    out = pl.pallas_call(
        kern,
        grid=(B * nt // PAIR,),
        in_specs=[
            pl.BlockSpec((PAIR * R, D), lambda s: (s, 0)),
            halo(0, False), halo(0, True), halo(1, False), halo(1, True),
            mod_spec(0), mod_spec(1),
            _layer_spec((1, D), layer),
            _layer_spec((D, 2 * D_FF), layer), _layer_spec((3, 2 * D_FF), layer),
            _layer_spec((1, 2 * D_FF), layer), _layer_spec((D_FF, D), layer),
            _layer_spec((1, D), layer),
        ],
        out_specs=pl.BlockSpec((PAIR * R, D), lambda s: (s, 0)),
        out_shape=jax.ShapeDtypeStruct((B * T, D), F32),
        scratch_shapes=[pltpu.VMEM((PAIR, RE, D), BF16),
                        pltpu.VMEM((PAIR, 2, RE, 2 * FF_CHUNK), F32),
                        pltpu.VMEM((PAIR, R, D_FF), BF16),
                        pltpu.VMEM((PAIR, R, D), F32)],
        compiler_params=_params(56, 1),
        name="ffn",
    )(x2, x2, x2, x2, x2, mods, mods, g_pre, w_up, conv_w, conv_b, w_down, g_post)
    return out.reshape(B, T, D)


def _rope_table(n_ctx, n_lat):
    rows = n_lat // GRID_W
    row = jnp.repeat(jnp.arange(rows, dtype=F32), GRID_W)
    col = jnp.tile(jnp.arange(GRID_W, dtype=F32), rows)
    n = HEAD_DIM // 4
    inv = ROPE_BASE ** (-jnp.arange(n, dtype=F32) / n)
    ang = jnp.concatenate([row[:, None] * inv, col[:, None] * inv], axis=-1)
    cos, sin = jnp.cos(ang), jnp.sin(ang)
    zero = jnp.zeros_like(sin)
    reps = LANE // HEAD_DIM
    cos_t = jnp.tile(jnp.concatenate([cos, cos], axis=-1), (1, reps))
    sin_lo = jnp.tile(jnp.concatenate([-sin, zero], axis=-1), (1, reps))
    sin_hi = jnp.tile(jnp.concatenate([zero, sin], axis=-1), (1, reps))
    lat = jnp.concatenate([cos_t, sin_lo, sin_hi], axis=-1)
    ctx = jnp.concatenate([jnp.ones((n_ctx, LANE), F32), jnp.zeros((n_ctx, 2 * LANE), F32)], axis=-1)
    return jnp.concatenate([ctx, lat], axis=0)


def kernel(x, c, ctx, c_ctx, w_ada, b_ada, g_pre_mix, g_post_mix, g_pre_ffn, g_post_ffn, w_in, att_sink, pool_w, pool_scale, gla_wa2, gla_ba, gla_norm, w_br_att, w_br_pool, w_br_gla, w_o, w_up, conv_w, conv_b, w_down):
    B, L, D = x.shape
    C = ctx.shape[1]
    depth = w_in.shape[0]
    R = ROW_TILE
    assert D == D_MODEL and C % R == 0 and L % R == 0 and L % GRID_W == 0
    n_ctx_tiles = C // R

    mod_rows = -(-(B + 1) // SUBLANE) * SUBLANE
    cc = jnp.zeros((mod_rows, D), F32).at[:B].set(c).at[mod_rows - 1].set(c_ctx)
    mods = _ada_table(cc, w_ada, b_ada).reshape(depth, mod_rows, 1, 6 * D)

    rope_tab = _rope_table(C, L)
    split = _O_GLR + 2 * GLA_GATE_RANK
    w_in_p = w_in[..., :IN_DIM_PAD].astype(BF16)
    w_mg = w_in[..., split:].astype(BF16)
    wa = jnp.zeros((depth, LANE, 2 * GLA_QK_DIM), F32)
    wa = wa.at[:, 0:GLA_GATE_RANK, 0:GLA_QK_DIM].set(gla_wa2[:, 0])
    wa = wa.at[:, GLA_GATE_RANK:2 * GLA_GATE_RANK, GLA_QK_DIM:].set(gla_wa2[:, 1])
    wa_hi = wa.astype(BF16)
    wa = jnp.concatenate([wa_hi, wa_hi, (wa - wa_hi.astype(F32)).astype(BF16)], axis=1)
    ba = gla_ba.reshape(depth, 1, 2 * GLA_QK_DIM)
    pbd = jnp.zeros((depth, POOL_DIM, POOL_DIM), F32)
    for g in range(len(POOL_WINDOWS)):
        sl = slice(g * POOL_GROUP_DIM, (g + 1) * POOL_GROUP_DIM)
        pbd = pbd.at[:, sl, sl].set(pool_w[:, g])
    pbd = pbd.astype(BF16)
    wba, wbp, wbg, wo = (w.astype(BF16) for w in (w_br_att, w_br_pool, w_br_gla, w_o))
    wup, wdn = w_up.astype(BF16), w_down.astype(BF16)

    def rows(t):
        return t.reshape(depth, 1, -1)

    g_pre_mix, g_post_mix, g_pre_ffn, g_post_ffn, pool_scale, gla_norm, conv_b = (
        rows(t) for t in (g_pre_mix, g_post_mix, g_pre_ffn, g_post_ffn, pool_scale, gla_norm, conv_b))

    xa = jnp.concatenate([ctx, x], axis=1)
    for l in range(depth):
        last = l == depth - 1
        off = n_ctx_tiles if last else 0
        q, kv, pu, gqkv, sgr, la, h = _in_proj(xa, mods, g_pre_mix, w_in_p, rope_tab, wa, ba, n_ctx_tiles, l)
        y_att = _attention(q, kv, att_sink, C, off, l)
        d_pool = _pool(pu, C)
        on = _gla(gqkv, la, gla_norm, C, l)
        xa = _merge(xa, y_att, d_pool, on, sgr, h, mods, g_post_mix, wba, pbd, pool_scale, wbp, wbg, w_mg, wo,
                    n_ctx_tiles, off, l)
        xa = _ffn(xa, mods, g_pre_ffn, wup, conv_w, conv_b, wdn, g_post_ffn, n_ctx_tiles - off, l)
    return xa
```

```python
import functools

import jax
import jax.numpy as jnp
import numpy as np
from jax import lax
from jax.experimental import pallas as pl
from jax.experimental.pallas import tpu as pltpu

F32 = jnp.float32
BF16 = jnp.bfloat16
HIGHEST = lax.Precision.HIGHEST

D_MODEL = 1024
GRID_W = 64
EPS = 1e-6

HEAD_DIM = 64
ATT_HEADS = 8
ATT_KV_HEADS = 2
ATT_GROUP = ATT_HEADS // ATT_KV_HEADS
WINDOW = 128
ROPE_BASE = 10000.0
ATT_DIM = ATT_HEADS * HEAD_DIM
KV_DIM = ATT_KV_HEADS * HEAD_DIM

POOL_WINDOWS = (2, 4, 8, 16)
POOL_GROUP_DIM = 64
POOL_DIM = len(POOL_WINDOWS) * POOL_GROUP_DIM
POOL_PAD = 16

GLA_HEADS = 4
GLA_DK = 32
GLA_DV = 64
GLA_GATE_RANK = 16
GLA_TAU = 16.0
GLA_QK_DIM = GLA_HEADS * GLA_DK
GLA_V_DIM = GLA_HEADS * GLA_DV
GLA_CHUNK = 64
GLA_TILES_PER_STEP = 4

D_FF = 2816
FF_CHUNK = 256
FF_GROUP = 4
PAIR = 2

LANE = 128
SUBLANE = 8
SUBLANE_LOG2 = SUBLANE.bit_length() - 1
ROW_TILE = 256

_GLR_PAD = LANE - 2 * GLA_GATE_RANK
_O_AQ = 0
_O_AK = _O_AQ + ATT_DIM
_O_AV = _O_AK + KV_DIM
_O_PU = _O_AV + KV_DIM
_O_GQ = _O_PU + POOL_DIM
_O_GK = _O_GQ + GLA_QK_DIM
_O_GV = _O_GK + GLA_QK_DIM
_O_GR = _O_GV + GLA_V_DIM
_O_GLR = _O_GR + GLA_V_DIM
IN_DIM_PAD = _O_GLR + LANE
_O_MG = 2 * D_MODEL
assert IN_DIM_PAD <= _O_MG

NEG_BIG = -1e30
LOG2E = float(np.log2(np.e))


def _params(vmem_mb, n_axes):
    return pltpu.CompilerParams(
        dimension_semantics=("arbitrary",) * n_axes,
        vmem_limit_bytes=vmem_mb * 1024 * 1024,
    )


def _resident(shape):
    nd = len(shape)
    return pl.BlockSpec(shape, lambda *_: (0,) * nd, pipeline_mode=pl.Buffered(1))


def _layer_spec(shape, layer):
    nd = len(shape)
    return pl.BlockSpec((None,) + tuple(shape), lambda *_: (layer,) + (0,) * nd,
                        pipeline_mode=pl.Buffered(1))


def _mod_spec(layer, ctx_row, is_ctx_tile):
    return pl.BlockSpec((None, 1, 1, 6 * D_MODEL),
                        lambda b, i: (layer, jnp.where(is_ctx_tile(i), ctx_row, b), 0, 0))


def _sigmoid(v):
    return 1.0 / (1.0 + jnp.exp(-v))


def _iota_div(shape, axis, divisor):
    shift = int(np.log2(divisor))
    assert 1 << shift == divisor
    return lax.shift_right_logical(lax.broadcasted_iota(jnp.int32, shape, axis), shift)


def _shifted_rows(ext, shift, n_rows):
    if shift:
        ext = pltpu.roll(ext, (-shift) % ext.shape[0], 0)
    return ext[SUBLANE:SUBLANE + n_rows]


def _modulated_norm(x, gain, scale, shift):
    ms = jnp.mean(x * x, axis=-1, keepdims=True)
    return (x * lax.rsqrt(ms + EPS) * gain) * (1.0 + scale) + shift


def _ada_kernel(c_ref, w_ref, b_ref, o_ref):
    c = c_ref[...]
    a = c * _sigmoid(c)
    o_ref[0] = jnp.dot(a, w_ref[0], precision=HIGHEST, preferred_element_type=F32) + b_ref[0]


def _ada_table(cc, w_ada, b_ada):
    depth = w_ada.shape[0]
    rows = cc.shape[0]
    width = 2 * D_MODEL
    n_col = w_ada.shape[2] // width
    return pl.pallas_call(
        _ada_kernel,
        grid=(depth, n_col),
        in_specs=[
            pl.BlockSpec((rows, D_MODEL), lambda l, j: (0, 0)),
            pl.BlockSpec((1, D_MODEL, width), lambda l, j: (l, 0, j)),
            pl.BlockSpec((1, 1, width), lambda l, j: (l, 0, j)),
        ],
        out_specs=pl.BlockSpec((1, rows, width), lambda l, j: (l, 0, j)),
        out_shape=jax.ShapeDtypeStruct((depth, rows, w_ada.shape[2]), F32),
        compiler_params=_params(32, 2),
        name="ada_table",
    )(cc, w_ada, b_ada.reshape(depth, 1, -1))


def _in_proj_kernel(x_ref, moda_ref, modb_ref, g_ref, w_ref, ropea_ref, ropeb_ref, wa_ref, ba_ref,
                    q_ref, kv_ref, pu_ref, gqkv_ref, sgr_ref, la_ref, h_ref):
    D = D_MODEL
    R = ROW_TILE
    mods = (moda_ref[0], modb_ref[0])
    ropes = (ropea_ref, ropeb_ref)

    def norm(t):
        rows = pl.ds(t * R, R)
        h_ref[rows, :] = _modulated_norm(x_ref[rows, :], g_ref[...], mods[t][:, D:2 * D],
                                         mods[t][:, 0:D]).astype(BF16)

    def project(t, after_first):
        rows = pl.ds(t * R, R)
        h = h_ref[rows, :]
        rope_ref = ropes[t]

        def proj(lo, width):
            return jnp.dot(h, w_ref[:, lo:lo + width], preferred_element_type=F32)

        cosf = rope_ref[:, 0:LANE]
        sin_lo = rope_ref[:, LANE:2 * LANE]
        sin_hi = rope_ref[:, 2 * LANE:3 * LANE]

        def rope(v):
            return (v * cosf + pltpu.roll(v, LANE - HEAD_DIM // 2, 1) * sin_lo
                    + pltpu.roll(v, HEAD_DIM // 2, 1) * sin_hi)

        aq = proj(_O_AQ, ATT_DIM)
        after_first()
        for j in range(ATT_DIM // LANE):
            q_ref[rows, j * LANE:(j + 1) * LANE] = (
                rope(aq[:, j * LANE:(j + 1) * LANE]) * (HEAD_DIM ** -0.5 * LOG2E)).astype(BF16)
        kv_ref[rows, 0:KV_DIM] = rope(proj(_O_AK, KV_DIM)).astype(BF16)
        kv_ref[rows, KV_DIM:2 * KV_DIM] = proj(_O_AV, KV_DIM).astype(BF16)

        pu_ref[rows, :] = proj(_O_PU, POOL_DIM)

        gqkv_ref[rows, 0:GLA_QK_DIM] = (proj(_O_GQ, GLA_QK_DIM) * (GLA_DK ** -0.5)).astype(BF16)
        gqkv_ref[rows, GLA_QK_DIM:] = proj(_O_GK, GLA_QK_DIM + GLA_V_DIM).astype(BF16)
        gr = proj(_O_GR, GLA_V_DIM)
        sgr_ref[rows, :] = (gr * _sigmoid(gr)).astype(BF16)

        glr = proj(_O_GLR, LANE)
        glr_hi = glr.astype(BF16)
        glr_lo = (glr - glr_hi.astype(F32)).astype(BF16)
        z = jnp.dot(jnp.concatenate([glr_hi, glr_lo, glr_hi], axis=1), wa_ref[...],
                    preferred_element_type=F32) + ba_ref[...]
        la_ref[rows, :] = (jnp.minimum(z, 0.0) - jnp.log(1.0 + jnp.exp(-jnp.abs(z)))) * (1.0 / GLA_TAU)

    norm(0)
    project(0, lambda: norm(1))
    project(1, lambda: None)


def _tile_of_pair(t, tiles_per_seq, n_ctx_tiles, ctx_row, layer):
    def seq_tile(s):
        return lax.rem(s * PAIR + t, tiles_per_seq)

    def mod_index(s):
        tile = s * PAIR + t
        return (layer, jnp.where(lax.rem(tile, tiles_per_seq) < n_ctx_tiles, ctx_row,
                                 lax.div(tile, tiles_per_seq)), 0, 0)

    return seq_tile, pl.BlockSpec((None, 1, 1, 6 * D_MODEL), mod_index)


def _in_proj(xa, mods, g_pre, w_in, rope_tab, wa, ba, n_ctx_tiles, layer):
    B, T, D = xa.shape
    R = ROW_TILE
    nt = T // R
    assert (B * nt) % PAIR == 0
    ctx_row = mods.shape[1] - 1

    def pair(width):
        return pl.BlockSpec((PAIR * R, width), lambda s: (s, 0))

    seq_tiles, mod_specs = zip(*(_tile_of_pair(t, nt, n_ctx_tiles, ctx_row, layer) for t in range(PAIR)))
    rope_specs = [pl.BlockSpec((R, 3 * LANE), lambda s, f=f: (f(s), 0)) for f in seq_tiles]
    widths = (ATT_DIM, 2 * KV_DIM, POOL_DIM, 2 * GLA_QK_DIM + GLA_V_DIM, GLA_V_DIM,
              2 * GLA_QK_DIM, D)
    dtypes = (BF16, BF16, F32, BF16, BF16, F32, BF16)
    outs = pl.pallas_call(
        _in_proj_kernel,
        grid=(B * nt // PAIR,),
        in_specs=[
            pair(D), *mod_specs,
            _layer_spec((1, D), layer),
            _layer_spec((D, _O_MG), layer),
            *rope_specs,
            _layer_spec((3 * LANE, 2 * GLA_QK_DIM), layer),
            _layer_spec((1, 2 * GLA_QK_DIM), layer),
        ],
        out_specs=[pair(w) for w in widths],
        out_shape=[jax.ShapeDtypeStruct((B * T, w), dt) for w, dt in zip(widths, dtypes)],
        compiler_params=_params(48, 1),
        name="in_proj",
    )(xa.reshape(B * T, D), mods, mods, g_pre, w_in, rope_tab, rope_tab, wa, ba)
    return [o.reshape(B, T, -1) for o in outs]


def _attn_kernel(sink_ref, q_ref, kv_ref, o_ref, s_s, *, layer, tile_off, n_ctx_tiles, n_ctx, n_lat):
    R = ROW_TILE
    W = WINDOW
    assert R == 2 * W == 2 * LANE
    n_win = R // W + 2
    i = pl.program_id(1) + tile_off
    n_blk = (n_ctx + n_lat) // W
    nt_dims = (((1,), (1,)), ((), ()))
    q = q_ref[0]

    def heads(kv, split, merge):
        n_keys = kv.shape[0]
        ks, v_ext_ts = [], []
        for g in range(ATT_KV_HEADS):
            ks.append(kv[:, g * HEAD_DIM:(g + 1) * HEAD_DIM])
            v = kv[:, KV_DIM + g * HEAD_DIM:KV_DIM + (g + 1) * HEAD_DIM].astype(F32)
            v_ext_ts.append(
                jnp.concatenate([v, jnp.ones((n_keys, LANE - HEAD_DIM), F32)], axis=1).T.astype(BF16))

        n_slots = s_s.shape[0]

        def scores(h):
            qh = q[:, h * HEAD_DIM:(h + 1) * HEAD_DIM]
            s_s[h % n_slots, 0:n_keys, :] = lax.dot_general(ks[h // ATT_GROUP], qh, nt_dims,
                                                             preferred_element_type=F32)

        n_at_once = n_slots // 2
        for h in range(n_at_once):
            scores(h)
        for h0 in range(0, ATT_HEADS, n_at_once):
            batch = tuple(range(h0, h0 + n_at_once))
            for h in batch:
                if h + n_at_once < ATT_HEADS:
                    scores(h + n_at_once)
            sinks = [sink_ref[layer, h] * LOG2E for h in batch]
            halves = [split(s_s.at[h % n_slots]) for h in batch]
            ms = [[jnp.maximum(jnp.max(s_half, axis=0, keepdims=True), sk) for s_half in hs]
                  for hs, sk in zip(halves, sinks)]
            ps = [[jnp.exp2(s_half - m).astype(BF16) for s_half, m in zip(hs, mh)]
                  for hs, mh in zip(halves, ms)]
            o_exts = [jnp.dot(v_ext_ts[h // ATT_GROUP], merge(p), preferred_element_type=F32)
                      for h, p in zip(batch, ps)]
            out_t = [o_ext[0:HEAD_DIM] / (o_ext[HEAD_DIM:HEAD_DIM + 1]
                                          + jnp.exp2(sk - jnp.concatenate(mh, axis=1)))
                     for o_ext, sk, mh in zip(o_exts, sinks, ms)]
            for j in range(0, n_at_once, 2):
                o_ref[0, :, (h0 + j) * HEAD_DIM:(h0 + j + 2) * HEAD_DIM] = (
                    jnp.concatenate(out_t[j:j + 2], axis=0).T.astype(BF16))

    @pl.when(i < n_ctx_tiles)
    def _context_queries():
        heads(kv_ref[0, 0:n_ctx, :],
              lambda ref: [ref[0:n_ctx, a * LANE:(a + 1) * LANE] for a in range(R // LANE)],
              lambda ps: jnp.concatenate(ps, axis=1))

    @pl.when(i >= n_ctx_tiles)
    def _latent_queries():
        first_blk = i * (R // W) - 1
        parts = []
        for blk in range(n_win):
            idx = jnp.clip(first_blk + blk, 0, n_blk - 1)
            parts.append(kv_ref[0, pl.ds(pl.multiple_of(idx * W, W), W), :])
        parts.append(kv_ref[0, 0:n_ctx, :])
        n_loc = n_win * W
        kj = lax.broadcasted_iota(jnp.int32, (W, LANE), 0)
        qi = lax.broadcasted_iota(jnp.int32, (W, LANE), 1)
        below = jnp.where(kj >= qi, 0.0, NEG_BIG).astype(F32)
        above = jnp.where(kj <= qi, 0.0, NEG_BIG).astype(F32)
        below_first = below + jnp.where(i == n_ctx_tiles, NEG_BIG, 0.0)
        above_last = above + jnp.where(i == n_blk // (R // W) - 1, NEG_BIG, 0.0)

        def split(ref):
            def blk(b, a):
                return ref[b * W:(b + 1) * W, a * LANE:(a + 1) * LANE]
            return [jnp.concatenate([blk(0, 0) + below_first, blk(1, 0), blk(2, 0) + above,
                                     ref[n_loc:n_loc + n_ctx, 0:LANE]], axis=0),
                    jnp.concatenate([blk(1, 1) + below, blk(2, 1), blk(3, 1) + above_last,
                                     ref[n_loc:n_loc + n_ctx, LANE:2 * LANE]], axis=0)]

        def merge(ps):
            zero = jnp.zeros((W, LANE), BF16)
            live = (n_win - 1) * W
            return jnp.concatenate(
                [jnp.concatenate([ps[0][0:live], zero, ps[0][live:]], axis=0),
                 jnp.concatenate([zero, ps[1]], axis=0)], axis=1)

        heads(jnp.concatenate(parts, axis=0), split, merge)


def _attention(q, kv, sink, n_ctx, tile_off, layer):
    B, T, _ = q.shape
    R = ROW_TILE
    nt = T // R
    kern = functools.partial(_attn_kernel, layer=layer, tile_off=tile_off, n_ctx_tiles=n_ctx // R,
                             n_ctx=n_ctx, n_lat=T - n_ctx)
    return pl.pallas_call(
        kern,
        grid=(B, nt - tile_off),
        in_specs=[
            pl.BlockSpec(memory_space=pltpu.SMEM),
            pl.BlockSpec((1, R, ATT_DIM), lambda b, i: (b, i + tile_off, 0)),
            pl.BlockSpec((1, T, 2 * KV_DIM), lambda b, i: (b, 0, 0)),
        ],
        out_specs=pl.BlockSpec((1, R, ATT_DIM), lambda b, i: (b, i + tile_off, 0)),
        out_shape=jax.ShapeDtypeStruct((B, T, ATT_DIM), BF16),
        scratch_shapes=[pltpu.VMEM((2 * ATT_GROUP, R + 2 * WINDOW + n_ctx, R), F32)],
        compiler_params=_params(48, 2),
        name="attention",
    )(sink, q, kv)


def _pool_kernel(u_ref, o_ref, p0, p2, p4, p8, *, n_ctx, n_lat):
    R = ROW_TILE
    T = n_ctx + n_lat
    RE = R + 2 * SUBLANE
    off_ctx = POOL_PAD
    off_lat = 2 * POOL_PAD
    n_comp = p0.shape[0] // R - 1

    read_end = n_comp * R + 2 * SUBLANE
    for lo_row, hi_row in ((0, off_ctx), (off_ctx + n_ctx, n_ctx + off_lat), (T + off_lat, read_end)):
        p0[lo_row:hi_row, :] = jnp.zeros((hi_row - lo_row, POOL_DIM), F32)
    for buf in (p2, p4, p8):
        buf[0:SUBLANE, :] = jnp.zeros((SUBLANE, POOL_DIM), F32)
        buf[read_end - SUBLANE:read_end, :] = jnp.zeros((SUBLANE, POOL_DIM), F32)
    p0[off_ctx:off_ctx + n_ctx, :] = u_ref[0, 0:n_ctx, :]
    p0[n_ctx + off_lat:T + off_lat, :] = u_ref[0, n_ctx:T, :]

    def stage(src, dst, back, fwd):
        def body(c, carry):
            r = pl.multiple_of(c * R, R)
            ext = src[pl.ds(r, RE), :]
            dst[pl.ds(r + SUBLANE, R), :] = _shifted_rows(ext, -back, R) + _shifted_rows(ext, fwd, R)
            return carry
        lax.fori_loop(0, n_comp, body, 0)

    stage(p0, p2, 1, 0)
    stage(p2, p4, 1, 1)
    stage(p4, p8, 2, 2)

    lane_group = _iota_div((R, POOL_DIM), 1, POOL_GROUP_DIM)
    row = lax.broadcasted_iota(jnp.int32, (R, POOL_DIM), 0)
    lo = jnp.full((R, POOL_DIM), POOL_WINDOWS[0] // 2, jnp.int32)
    hi = jnp.full((R, POOL_DIM), POOL_WINDOWS[0] - POOL_WINDOWS[0] // 2 - 1, jnp.int32)
    for g in range(1, len(POOL_WINDOWS)):
        w = POOL_WINDOWS[g]
        lo = jnp.where(lane_group == g, w // 2, lo)
        hi = jnp.where(lane_group == g, w - w // 2 - 1, hi)

    def emit(i, carry):
        is_ctx = i < n_ctx // R
        r0 = pl.multiple_of(i * R, R)
        r = pl.multiple_of(r0 + jnp.where(is_ctx, off_ctx, off_lat), SUBLANE)
        pos = row + jnp.where(is_ctx, r0, r0 - n_ctx)
        seq_len = jnp.where(is_ctx, n_ctx, n_lat)
        ext8 = p8[pl.ds(r - SUBLANE, RE), :]
        sums = (p2[pl.ds(r, R), :], p4[pl.ds(r, R), :], ext8[SUBLANE:SUBLANE + R],
                _shifted_rows(ext8, -4, R) + _shifted_rows(ext8, 4, R))
        tot = sums[0]
        for g in range(1, len(POOL_WINDOWS)):
            tot = jnp.where(lane_group == g, sums[g], tot)
        cnt = (jnp.minimum(pos + hi + 1, seq_len) - jnp.maximum(pos - lo, 0)).astype(F32)
        o_ref[0, pl.ds(r0, R), :] = (tot / cnt - u_ref[0, pl.ds(r0, R), :]).astype(BF16)
        return carry

    lax.fori_loop(0, T // R, emit, 0)


def _pool(pu, n_ctx):
    B, T, _ = pu.shape
    R = ROW_TILE
    rows = (pl.cdiv(T + 3 * POOL_PAD, R) + 1) * R
    kern = functools.partial(_pool_kernel, n_ctx=n_ctx, n_lat=T - n_ctx)
    return pl.pallas_call(
        kern,
        grid=(B,),
        in_specs=[pl.BlockSpec((1, T, POOL_DIM), lambda b: (b, 0, 0))],
        out_specs=pl.BlockSpec((1, T, POOL_DIM), lambda b: (b, 0, 0)),
        out_shape=jax.ShapeDtypeStruct((B, T, POOL_DIM), BF16),
        scratch_shapes=[pltpu.VMEM((rows, POOL_DIM), F32)] * 4,
        compiler_params=_params(48, 1),
        name="pool",
    )(pu)


def _gla_kernel(qkv_ref, la_ref, gn_ref, o_ref, of_s, ob_s, stf, stb, *, n_ctx):
    CH = GLA_CHUNK
    R = ROW_TILE
    NC = R // CH
    T = of_s.shape[0]
    n_tiles = T // R
    n_ctx_tiles = n_ctx // R
    QK = GLA_QK_DIM
    DV = GLA_V_DIM
    H = GLA_HEADS

    stf[...] = jnp.zeros(stf.shape, F32)
    stb[...] = jnp.zeros(stb.shape, F32)

    same_chunk = _iota_div((R, R), 0, CH) == _iota_div((R, R), 1, CH)
    ri = lax.broadcasted_iota(jnp.int32, (R, R), 0)
    ci = lax.broadcasted_iota(jnp.int32, (R, R), 1)
    tri_lo = (same_chunk & (ci <= ri)).astype(BF16)
    tri_up = (same_chunk & (ci >= ri)).astype(BF16)
    rs = jnp.bitwise_and(lax.broadcasted_iota(jnp.int32, (H * R, R), 0), R - 1)
    cs = lax.broadcasted_iota(jnp.int32, (H * R, R), 1)
    same_chunk_h = (lax.shift_right_logical(rs, CH.bit_length() - 1)
                    == lax.shift_right_logical(cs, CH.bit_length() - 1))
    keep_lo = same_chunk_h & (cs <= rs)
    keep_up = same_chunk_h & (cs >= rs)
    qk_head = _iota_div((H * R, QK), 0, R) == _iota_div((H * R, QK), 1, GLA_DK)
    chunk_of_row = _iota_div((R, QK), 0, CH)
    st_keep = _iota_div((DV, QK), 0, GLA_DV) == _iota_div((DV, QK), 1, GLA_DK)
    nt_dims = (((1,), (1,)), ((), ()))

    def split3(a):
        a1 = a.astype(BF16)
        r1 = a - a1.astype(F32)
        a2 = r1.astype(BF16)
        a3 = (r1 - a2.astype(F32)).astype(BF16)
        return jnp.concatenate([a1, a2, a3], axis=1)

    def rows_of_chunks(cum, offset):
        picks = [cum[c * CH + offset:c * CH + offset + 1, :] for c in range(NC)]
        full = jnp.concatenate([jnp.broadcast_to(p, (CH, QK)) for p in picks], axis=0)
        return picks, full

    def load(t, col0, tri):
        r0 = t * R if isinstance(t, int) else pl.multiple_of(t * R, R)
        a = la_ref[0, pl.ds(r0, R), col0:col0 + QK]
        c3 = jnp.dot(tri, split3(a), preferred_element_type=F32)
        return dict(r0=r0, cum=c3[:, 0:QK] + c3[:, QK:2 * QK] + c3[:, 2 * QK:3 * QK],
                    q=qkv_ref[0, pl.ds(r0, R), 0:QK].astype(F32),
                    k=qkv_ref[0, pl.ds(r0, R), QK:2 * QK].astype(F32),
                    v=qkv_ref[0, pl.ds(r0, R), 2 * QK:])

    def scale(d, last):
        cum, q, k = d["cum"], d["q"], d["k"]
        tots, tot_b = rows_of_chunks(cum, last)
        _, mid_b = rows_of_chunks(cum, CH // 2)
        d["dec"] = [jnp.exp(t) for t in tots]
        d["qe"] = (q * jnp.exp(cum)).astype(BF16)
        d["qm"] = (q * jnp.exp(cum - mid_b)).astype(BF16)
        km = k * jnp.exp(mid_b - cum)
        d["ks"] = jnp.where(qk_head, jnp.concatenate([km] * H, axis=0), 0.0).astype(BF16)
        kl = (k * jnp.exp(tot_b - cum)).astype(BF16)
        d["kl_blocks"] = jnp.concatenate(
            [jnp.where(chunk_of_row == c, kl, jnp.zeros_like(kl)) for c in range(NC)], axis=1)
        d["v_t"] = d["v"].astype(F32).T.astype(BF16)

    def scores(d, keep_t):
        att_t = lax.dot_general(d["ks"], d["qm"], nt_dims, preferred_element_type=F32)
        d["att_t"] = jnp.where(keep_t, att_t, 0.0).astype(BF16)
        kv = jnp.dot(d["v_t"], d["kl_blocks"], preferred_element_type=F32)
        d["kv"] = [kv[:, c * QK:(c + 1) * QK] for c in range(NC)]

    def intra(d):
        o_t = jnp.concatenate(
            [jnp.dot(d["v_t"][hd * GLA_DV:(hd + 1) * GLA_DV], d["att_t"][hd * R:(hd + 1) * R],
                     preferred_element_type=F32) for hd in range(H)], axis=0)
        d["o"] = o_t.T

    def inter(tiles, order, st_ref, o_scr):
        state = st_ref[...]
        for d in tiles:
            parts = [None] * NC
            for c in order:
                parts[c] = lax.dot_general(d["qe"][c * CH:(c + 1) * CH], state.astype(BF16), nt_dims,
                                           preferred_element_type=F32)
                state = state * d["dec"][c] + jnp.where(st_keep, d["kv"][c], 0.0)
            o_scr[pl.ds(d["r0"], R), :] = d["o"] + jnp.concatenate(parts, axis=0)
        st_ref[...] = state

    def advance(f_tiles, b_tiles):
        fs = [load(t, 0, tri_lo) for t in f_tiles]
        bs = [load(t, QK, tri_up) for t in b_tiles]
        for d in fs:
            scale(d, CH - 1)
        for d in bs:
            scale(d, 0)
        for d in fs:
            scores(d, keep_up)
        for d in bs:
            scores(d, keep_lo)
        for d in fs + bs:
            intra(d)
        inter(fs, range(NC), stf, of_s)
        inter(bs, range(NC - 1, -1, -1), stb, ob_s)

    for s in range(n_ctx_tiles):
        advance([s], [n_ctx_tiles - 1 - s])
    n_lat_tiles = n_tiles - n_ctx_tiles
    per_step = GLA_TILES_PER_STEP if n_lat_tiles % GLA_TILES_PER_STEP == 0 else 1

    def step(s, carry):
        first = n_ctx_tiles + s * per_step
        last = n_tiles - 1 - s * per_step
        advance([first + j for j in range(per_step)], [last - j for j in range(per_step)])
        return carry

    lax.fori_loop(0, n_lat_tiles // per_step, step, 0)

    gmean = (_iota_div((DV, DV), 0, GLA_DV) == _iota_div((DV, DV), 1, GLA_DV)).astype(BF16) * (1.0 / GLA_DV)
    gn = gn_ref[...]

    def finish(i, carry):
        r0 = pl.multiple_of(i * R, R)
        o = of_s[pl.ds(r0, R), :] + ob_s[pl.ds(r0, R), :]
        oo = o * o
        hi = oo.astype(BF16)
        lo = (oo - hi.astype(F32)).astype(BF16)
        ms = jnp.dot(hi, gmean, preferred_element_type=F32) + jnp.dot(lo, gmean, preferred_element_type=F32)
        o_ref[0, pl.ds(r0, R), :] = (o * lax.rsqrt(ms + EPS) * gn).astype(BF16)
        return carry

    lax.fori_loop(0, n_tiles, finish, 0)


def _gla(gqkv, la, gla_norm, n_ctx, layer):
    B, T, _ = gqkv.shape
    kern = functools.partial(_gla_kernel, n_ctx=n_ctx)
    return pl.pallas_call(
        kern,
        grid=(B,),
        in_specs=[
            pl.BlockSpec((1, T, 2 * GLA_QK_DIM + GLA_V_DIM), lambda b: (b, 0, 0)),
            pl.BlockSpec((1, T, 2 * GLA_QK_DIM), lambda b: (b, 0, 0)),
            _layer_spec((1, GLA_V_DIM), layer),
        ],
        out_specs=pl.BlockSpec((1, T, GLA_V_DIM), lambda b: (b, 0, 0)),
        out_shape=jax.ShapeDtypeStruct((B, T, GLA_V_DIM), BF16),
        scratch_shapes=[pltpu.VMEM((T, GLA_V_DIM), F32), pltpu.VMEM((T, GLA_V_DIM), F32),
                        pltpu.VMEM((GLA_V_DIM, GLA_QK_DIM), F32),
                        pltpu.VMEM((GLA_V_DIM, GLA_QK_DIM), F32)],
        compiler_params=_params(48, 1),
        name="gla",
    )(gqkv, la, gla_norm)


N_MERGE_TILED = 6


def _merge_kernel(*refs):
    D = D_MODEL
    R = ROW_TILE
    tiled = refs[:PAIR * N_MERGE_TILED]
    mod_refs = refs[PAIR * N_MERGE_TILED:PAIR * N_MERGE_TILED + PAIR]
    (gpost_ref, wba_ref, pbd_ref, ps_ref, wbp_ref, wbg_ref, wmg0_ref, wmg1_ref, wmg2_ref, wo_ref,
     o_ref) = refs[PAIR * N_MERGE_TILED + PAIR:]
    wmg_refs = (wmg0_ref, wmg1_ref, wmg2_ref)

    for t in range(PAIR):
        x_ref, att_ref, dp_ref, on_ref, sgr_ref, h_ref = tiled[t::PAIR]
        h = h_ref[...]

        def gate(j):
            return _sigmoid(jnp.dot(h, wmg_refs[j][...], preferred_element_type=F32))

        m = gate(0) * jnp.dot(att_ref[...], wba_ref[...], preferred_element_type=F32)
        yp = jnp.dot(dp_ref[...], pbd_ref[...], preferred_element_type=F32) * ps_ref[...]
        m = m + gate(1) * jnp.dot(yp.astype(BF16), wbp_ref[...], preferred_element_type=F32)
        m = m + gate(2) * jnp.dot(on_ref[...] * sgr_ref[...], wbg_ref[...], preferred_element_type=F32)
        y = jnp.dot(m.astype(BF16), wo_ref[...], preferred_element_type=F32)
        ms = jnp.mean(y * y, axis=-1, keepdims=True)
        o_ref[t * R:(t + 1) * R, :] = (
            x_ref[...] + mod_refs[t][0][:, 2 * D:3 * D] * (y * lax.rsqrt(ms + EPS) * gpost_ref[...]))


def _merge(xa, y_att, d_pool, on, sgr, h, mods, g_post, wba, pbd, ps, wbp, wbg, wmg, wo,
           n_ctx_tiles, tile_off, layer):
    B, T, D = xa.shape
    R = ROW_TILE
    nt = T // R
    kept = nt - tile_off
    assert (B * kept) % PAIR == 0
    ctx_row = mods.shape[1] - 1

    def coords(s, t):
        tile = s * PAIR + t
        return lax.div(tile, kept), lax.rem(tile, kept) + tile_off

    def tile_specs(width):
        def spec(t):
            def index(s):
                b, i = coords(s, t)
                return (b * nt + i, 0)
            return pl.BlockSpec((R, width), index)
        return [spec(t) for t in range(PAIR)]

    def mod_spec(t):
        def index(s):
            b, i = coords(s, t)
            return (layer, jnp.where(i < n_ctx_tiles, ctx_row, b), 0, 0)
        return pl.BlockSpec((None, 1, 1, 6 * D), index)

    tiled = (xa, y_att, d_pool, on, sgr, h)
    assert len(tiled) == N_MERGE_TILED
    flat = [a.reshape(B * T, a.shape[-1]) for a in tiled]
    out = pl.pallas_call(
        _merge_kernel,
        grid=(B * kept // PAIR,),
        in_specs=[
            *(spec for a in flat for spec in tile_specs(a.shape[-1])),
            *(mod_spec(t) for t in range(PAIR)),
            _layer_spec((1, D), layer),
            _layer_spec((ATT_DIM, D), layer), _layer_spec((POOL_DIM, POOL_DIM), layer),
            _layer_spec((1, POOL_DIM), layer), _layer_spec((POOL_DIM, D), layer),
            _layer_spec((GLA_V_DIM, D), layer),
            *(pl.BlockSpec((None, D, D), lambda s, j=j: (layer, 0, _O_MG // D + j),
                           pipeline_mode=pl.Buffered(1)) for j in range(3)),
            _layer_spec((D, D), layer),
        ],
        out_specs=pl.BlockSpec((PAIR * R, D), lambda s: (s, 0)),
        out_shape=jax.ShapeDtypeStruct((B * kept * R, D), F32),
        compiler_params=_params(48, 1),
        name="merge",
    )(*(a for a in flat for _ in range(PAIR)), *([mods] * PAIR), g_post, wba, pbd, ps, wbp, wbg,
      wmg, wmg, wmg, wo)
    return out.reshape(B, kept * R, D)


def _ffn_kernel(x_ref, xpa_ref, xna_ref, xpb_ref, xnb_ref, moda_ref, modb_ref, gpre_ref, wup_ref, cw_ref,
                cb_ref, wdn_ref, gpost_ref, o_ref, he_s, u_s, act_s, acc_s, *, n_ctx_tiles, tiles_per_seq):
    D = D_MODEL
    R = ROW_TILE
    S = SUBLANE
    G = R // S
    CK = FF_CHUNK
    n_chunks = D_FF // CK
    halo_refs = ((xpa_ref, xna_ref), (xpb_ref, xnb_ref))
    mods = (moda_ref[0], modb_ref[0])
    r_idx = lax.broadcasted_iota(jnp.int32, (R, R), 0)
    c_idx = lax.broadcasted_iota(jnp.int32, (R, R), 1)

    def seq_of(p):
        return jnp.bitwise_and(p, S - 1) * G + lax.shift_right_logical(p, SUBLANE_LOG2)

    to_perm = (c_idx == seq_of(r_idx)).astype(BF16)
    to_seq = (r_idx == seq_of(c_idx)).astype(BF16)
    sub = lax.broadcasted_iota(jnp.int32, (S, 2 * CK), 0)
    halo_row = lax.broadcasted_iota(jnp.int32, (S, 1), 0)

    def chunk_cols(ref, j):
        return jnp.concatenate([ref[:, j * CK:(j + 1) * CK], ref[:, D_FF + j * CK:D_FF + (j + 1) * CK]],
                               axis=1)

    def prologue(t):
        i = lax.rem(pl.program_id(0) * PAIR + t, tiles_per_seq)
        scale, shift = mods[t][:, 4 * D:5 * D], mods[t][:, 3 * D:4 * D]
        h = _modulated_norm(x_ref[t * R:(t + 1) * R, :], gpre_ref[...], scale, shift).astype(BF16)
        he_s[t, 0:R, :] = jnp.dot(to_perm, h, preferred_element_type=F32).astype(BF16)
        xp_ref, xn_ref = halo_refs[t]
        halo = jnp.concatenate([xp_ref[S - 1:S, :], xn_ref[0:1, :], jnp.zeros((S - 2, D), F32)], axis=0)
        seq_start = (i == 0) | (i == n_ctx_tiles)
        seq_end = (i == n_ctx_tiles - 1) | (i == tiles_per_seq - 1)
        outside = ((halo_row == 0) & seq_start) | ((halo_row == 1) & seq_end) | (halo_row >= 2)
        he_s[t, R:R + S, :] = jnp.where(
            outside, 0.0, _modulated_norm(halo, gpre_ref[...], scale, shift)).astype(BF16)

    def up(t, j):
        he = he_s[t]
        u_s[t, j % 2, :, 0:CK] = jnp.dot(he, wup_ref[:, j * CK:(j + 1) * CK], preferred_element_type=F32)
        u_s[t, j % 2, :, CK:] = jnp.dot(he, wup_ref[:, D_FF + j * CK:D_FF + (j + 1) * CK],
                                        preferred_element_type=F32)

    def finish(t, j):
        u = u_s[t, j % 2, 0:R, :]
        edge = u_s[t, j % 2, R:R + S, :]
        prev0 = jnp.where(sub == 0, edge[0:1], pltpu.roll(u[R - S:R], 1, 0))
        next_last = jnp.where(sub == S - 1, edge[1:2], pltpu.roll(u[0:S], S - 1, 0))
        prev = jnp.concatenate([prev0, u[0:R - S]], axis=0)
        nxt = jnp.concatenate([u[S:R], next_last], axis=0)
        cw = chunk_cols(cw_ref, j)
        c = prev * cw[0:1] + u * cw[1:2] + nxt * cw[2:3] + chunk_cols(cb_ref, j)
        g = c[:, CK:]
        act_s[t, :, j * CK:(j + 1) * CK] = (c[:, :CK] * (g * _sigmoid(g))).astype(BF16)

    def down(t, j0, j1):
        y = jnp.dot(act_s[t, :, j0 * CK:j1 * CK], wdn_ref[j0 * CK:j1 * CK, :], preferred_element_type=F32)
        if j0 == 0:
            acc_s[t] = y
        else:
            acc_s[t] += y

    def epilogue(t):
        acc = acc_s[t]
        acc_hi = acc.astype(BF16)
        acc_lo = (acc - acc_hi.astype(F32)).astype(BF16)
        y = (jnp.dot(to_seq, acc_hi, preferred_element_type=F32)
             + jnp.dot(to_seq, acc_lo, preferred_element_type=F32))
        ms = jnp.mean(y * y, axis=-1, keepdims=True)
        o_ref[t * R:(t + 1) * R, :] = (x_ref[t * R:(t + 1) * R, :]
                                       + mods[t][:, 5 * D:6 * D] * (y * lax.rsqrt(ms + EPS) * gpost_ref[...]))

    def body(t, after_first_up):
        up(t, 0)
        after_first_up()
        group_start = 0
        for j in range(n_chunks):
            if j + 1 < n_chunks:
                up(t, j + 1)
            finish(t, j)
            if j + 1 - group_start == FF_GROUP or j + 1 == n_chunks:
                down(t, group_start, j + 1)
                group_start = j + 1

    prologue(0)
    body(0, lambda: prologue(1))
    body(1, lambda: epilogue(0))
    epilogue(1)


def _ffn(xa, mods, g_pre, w_up, conv_w, conv_b, w_down, g_post, n_ctx_tiles, layer):
    B, T, D = xa.shape
    R = ROW_TILE
    H = SUBLANE
    nt = T // R
    assert (B * nt) % PAIR == 0 and PAIR == 2
    ctx_row = mods.shape[1] - 1
    rpb = R // H
    RE = R + H
    last_blk = B * T // H - 1

    def halo(t, after):
        def index(s):
            tile = s * PAIR + t
            return ((jnp.minimum((tile + 1) * rpb, last_blk) if after else jnp.maximum(tile * rpb - 1, 0)), 0)
        return pl.BlockSpec((H, D), index)

    def mod_spec(t):
        def index(s):
            tile = s * PAIR + t
            return (layer, jnp.where(lax.rem(tile, nt) < n_ctx_tiles, ctx_row, lax.div(tile, nt)), 0, 0)
        return pl.BlockSpec((None, 1, 1, 6 * D), index)

    x2 = xa.reshape(B * T, D)
    kern = functools.partial(_ffn_kernel, n_ctx_tiles=n_ctx_tiles, tiles_per_seq=nt)
    out = pl.pallas_call(
        kern,
        grid=(B * nt // PAIR,),
        in_specs=[
            pl.BlockSpec((PAIR * R, D), lambda s: (s, 0)),
            halo(0, False), halo(0, True), halo(1, False), halo(1, True),
            mod_spec(0), mod_spec(1),
            _layer_spec((1, D), layer),
            _layer_spec((D, 2 * D_FF), layer), _layer_spec((3, 2 * D_FF), layer),
            _layer_spec((1, 2 * D_FF), layer), _layer_spec((D_FF, D), layer),
            _layer_spec((1, D), layer),
        ],
        out_specs=pl.BlockSpec((PAIR * R, D), lambda s: (s, 0)),
        out_shape=jax.ShapeDtypeStruct((B * T, D), F32),
        scratch_shapes=[pltpu.VMEM((PAIR, RE, D), BF16),
                        pltpu.VMEM((PAIR, 2, RE, 2 * FF_CHUNK), F32),
                        pltpu.VMEM((PAIR, R, D_FF), BF16),
                        pltpu.VMEM((PAIR, R, D), F32)],
        compiler_params=_params(56, 1),
        name="ffn",
    )(x2, x2, x2, x2, x2, mods, mods, g_pre, w_up, conv_w, conv_b, w_down, g_post)
    return out.reshape(B, T, D)


def _rope_table(n_ctx, n_lat):
    rows = n_lat // GRID_W
    row = jnp.repeat(jnp.arange(rows, dtype=F32), GRID_W)
    col = jnp.tile(jnp.arange(GRID_W, dtype=F32), rows)
    n = HEAD_DIM // 4
    inv = ROPE_BASE ** (-jnp.arange(n, dtype=F32) / n)
    ang = jnp.concatenate([row[:, None] * inv, col[:, None] * inv], axis=-1)
    cos, sin = jnp.cos(ang), jnp.sin(ang)
    zero = jnp.zeros_like(sin)
    reps = LANE // HEAD_DIM
    cos_t = jnp.tile(jnp.concatenate([cos, cos], axis=-1), (1, reps))
    sin_lo = jnp.tile(jnp.concatenate([-sin, zero], axis=-1), (1, reps))
    sin_hi = jnp.tile(jnp.concatenate([zero, sin], axis=-1), (1, reps))
    lat = jnp.concatenate([cos_t, sin_lo, sin_hi], axis=-1)
    ctx = jnp.concatenate([jnp.ones((n_ctx, LANE), F32), jnp.zeros((n_ctx, 2 * LANE), F32)], axis=-1)
    return jnp.concatenate([ctx, lat], axis=0)


def kernel(x, c, ctx, c_ctx, w_ada, b_ada, g_pre_mix, g_post_mix, g_pre_ffn, g_post_ffn, w_in, att_sink, pool_w, pool_scale, gla_wa2, gla_ba, gla_norm, w_br_att, w_br_pool, w_br_gla, w_o, w_up, conv_w, conv_b, w_down):
    B, L, D = x.shape
    C = ctx.shape[1]
    depth = w_in.shape[0]
    R = ROW_TILE
    assert D == D_MODEL and C % R == 0 and L % R == 0 and L % GRID_W == 0
    n_ctx_tiles = C // R

    mod_rows = -(-(B + 1) // SUBLANE) * SUBLANE
    cc = jnp.zeros((mod_rows, D), F32).at[:B].set(c).at[mod_rows - 1].set(c_ctx)
    mods = _ada_table(cc, w_ada, b_ada).reshape(depth, mod_rows, 1, 6 * D)

    rope_tab = _rope_table(C, L)
    split = _O_GLR + 2 * GLA_GATE_RANK
    w_in_p = jnp.concatenate(
        [w_in[..., :split], jnp.zeros((depth, D, _O_MG - split), w_in.dtype), w_in[..., split:]],
        axis=-1).astype(BF16)
    w_mg = w_in_p
    wa = jnp.zeros((depth, LANE, 2 * GLA_QK_DIM), F32)
    wa = wa.at[:, 0:GLA_GATE_RANK, 0:GLA_QK_DIM].set(gla_wa2[:, 0])
    wa = wa.at[:, GLA_GATE_RANK:2 * GLA_GATE_RANK, GLA_QK_DIM:].set(gla_wa2[:, 1])
    wa_hi = wa.astype(BF16)
    wa = jnp.concatenate([wa_hi, wa_hi, (wa - wa_hi.astype(F32)).astype(BF16)], axis=1)
    ba = gla_ba.reshape(depth, 1, 2 * GLA_QK_DIM)
    pbd = jnp.zeros((depth, POOL_DIM, POOL_DIM), F32)
    for g in range(len(POOL_WINDOWS)):
        sl = slice(g * POOL_GROUP_DIM, (g + 1) * POOL_GROUP_DIM)
        pbd = pbd.at[:, sl, sl].set(pool_w[:, g])
    pbd = pbd.astype(BF16)
    wba, wbp, wbg, wo = (w.astype(BF16) for w in (w_br_att, w_br_pool, w_br_gla, w_o))
    wup, wdn = w_up.astype(BF16), w_down.astype(BF16)

    def rows(t):
        return t.reshape(depth, 1, -1)

    g_pre_mix, g_post_mix, g_pre_ffn, g_post_ffn, pool_scale, gla_norm, conv_b = (
        rows(t) for t in (g_pre_mix, g_post_mix, g_pre_ffn, g_post_ffn, pool_scale, gla_norm, conv_b))

    xa = jnp.concatenate([ctx, x], axis=1)
    for l in range(depth):
        last = l == depth - 1
        off = n_ctx_tiles if last else 0
        q, kv, pu, gqkv, sgr, la, h = _in_proj(xa, mods, g_pre_mix, w_in_p, rope_tab, wa, ba, n_ctx_tiles, l)
        y_att = _attention(q, kv, att_sink, C, off, l)
        d_pool = _pool(pu, C)
        on = _gla(gqkv, la, gla_norm, C, l)
        xa = _merge(xa, y_att, d_pool, on, sgr, h, mods, g_post_mix, wba, pbd, pool_scale, wbp, wbg, w_mg, wo,
                    n_ctx_tiles, off, l)
        xa = _ffn(xa, mods, g_pre_ffn, wup, conv_w, conv_b, wdn, g_post_ffn, n_ctx_tiles - off, l)
    return xa
```

```python
import functools

import jax
import jax.numpy as jnp
import numpy as np
from jax import lax
from jax.experimental import pallas as pl
from jax.experimental.pallas import tpu as pltpu

F32 = jnp.float32
BF16 = jnp.bfloat16
HIGHEST = lax.Precision.HIGHEST

D_MODEL = 1024
GRID_W = 64
EPS = 1e-6

HEAD_DIM = 64
ATT_HEADS = 8
ATT_KV_HEADS = 2
ATT_GROUP = ATT_HEADS // ATT_KV_HEADS
WINDOW = 128
ROPE_BASE = 10000.0
ATT_DIM = ATT_HEADS * HEAD_DIM
KV_DIM = ATT_KV_HEADS * HEAD_DIM

POOL_WINDOWS = (2, 4, 8, 16)
POOL_GROUP_DIM = 64
POOL_DIM = len(POOL_WINDOWS) * POOL_GROUP_DIM

GLA_HEADS = 4
GLA_DK = 32
GLA_DV = 64
GLA_GATE_RANK = 16
GLA_TAU = 16.0
GLA_QK_DIM = GLA_HEADS * GLA_DK
GLA_V_DIM = GLA_HEADS * GLA_DV
GLA_CHUNK = 64
GLA_TILES_PER_STEP = 4

D_FF = 2816
FF_CHUNK = 256
FF_GROUP = 4
PAIR = 2

LANE = 128
SUBLANE = 8
SUBLANE_LOG2 = SUBLANE.bit_length() - 1
ROW_TILE = 256

_O_AQ = 0
_O_AK = _O_AQ + ATT_DIM
_O_AV = _O_AK + KV_DIM
_O_PU = _O_AV + KV_DIM
_O_GQ = _O_PU + POOL_DIM
_O_GK = _O_GQ + GLA_QK_DIM
_O_GV = _O_GK + GLA_QK_DIM
_O_GR = _O_GV + GLA_V_DIM
_O_GLR = _O_GR + GLA_V_DIM
IN_DIM_PAD = _O_GLR + LANE
_O_MG = 2 * D_MODEL
assert IN_DIM_PAD <= _O_MG

NEG_BIG = -1e30
LOG2E = float(np.log2(np.e))


def _params(vmem_mb, n_axes):
    return pltpu.CompilerParams(
        dimension_semantics=("arbitrary",) * n_axes,
        vmem_limit_bytes=vmem_mb * 1024 * 1024,
    )


def _layer_spec(shape, layer):
    nd = len(shape)
    return pl.BlockSpec((None,) + tuple(shape), lambda *_: (layer,) + (0,) * nd,
                        pipeline_mode=pl.Buffered(1))


def _sigmoid(v):
    return 1.0 / (1.0 + jnp.exp(-v))


def _iota_div(shape, axis, divisor):
    shift = int(np.log2(divisor))
    assert 1 << shift == divisor
    return lax.shift_right_logical(lax.broadcasted_iota(jnp.int32, shape, axis), shift)


def _modulated_norm(x, gain, scale, shift):
    ms = jnp.mean(x * x, axis=-1, keepdims=True)
    return (x * lax.rsqrt(ms + EPS) * gain) * (1.0 + scale) + shift


def _ada_kernel(c_ref, w_ref, b_ref, o_ref):
    c = c_ref[...]
    a = c * _sigmoid(c)
    o_ref[0] = jnp.dot(a, w_ref[0], precision=HIGHEST, preferred_element_type=F32) + b_ref[0]


def _ada_table(cc, w_ada, b_ada):
    depth = w_ada.shape[0]
    rows = cc.shape[0]
    width = 2 * D_MODEL
    n_col = w_ada.shape[2] // width
    return pl.pallas_call(
        _ada_kernel,
        grid=(depth, n_col),
        in_specs=[
            pl.BlockSpec((rows, D_MODEL), lambda l, j: (0, 0)),
            pl.BlockSpec((1, D_MODEL, width), lambda l, j: (l, 0, j)),
            pl.BlockSpec((1, 1, width), lambda l, j: (l, 0, j)),
        ],
        out_specs=pl.BlockSpec((1, rows, width), lambda l, j: (l, 0, j)),
        out_shape=jax.ShapeDtypeStruct((depth, rows, w_ada.shape[2]), F32),
        compiler_params=_params(32, 2),
        name="ada_table",
    )(cc, w_ada, b_ada.reshape(depth, 1, -1))


def _stream_tile(x_refs, t, tiles_per_seq, n_ctx_tiles):
    if len(x_refs) == 1:
        return x_refs[0][t * ROW_TILE:(t + 1) * ROW_TILE, :]
    if len(x_refs) == PAIR:
        return x_refs[t][...]
    i = lax.rem(pl.program_id(0) * PAIR + t, tiles_per_seq)
    return jnp.where(i < n_ctx_tiles, x_refs[t][...], x_refs[PAIR + t][...])


def _stream_specs(x_src, tile_coords, tiles_per_seq, n_ctx_tiles):
    R = ROW_TILE
    if not isinstance(x_src, tuple):
        B, T, D = x_src.shape
        return [x_src.reshape(B * T, D)], [pl.BlockSpec((PAIR * R, D), lambda s: (s, 0))]
    ctx, lat = x_src
    B, _, D = ctx.shape
    n_lat_tiles = tiles_per_seq - n_ctx_tiles

    def spec(t, is_ctx):
        def index(s):
            b, i = tile_coords(s, t)
            if is_ctx:
                return (b * n_ctx_tiles + jnp.minimum(i, n_ctx_tiles - 1), 0)
            return (b * n_lat_tiles + jnp.maximum(i - n_ctx_tiles, 0), 0)
        return pl.BlockSpec((R, D), index)

    arrays = [ctx.reshape(-1, D)] * PAIR + [lat.reshape(-1, D)] * PAIR
    return arrays, [spec(t, True) for t in range(PAIR)] + [spec(t, False) for t in range(PAIR)]


def _in_proj_kernel(*refs, n_x, tiles_per_seq, n_ctx_tiles):
    x_refs = refs[:n_x]
    (moda_ref, modb_ref, g_ref, w_ref, ropea_ref, ropeb_ref, wa_ref, ba_ref,
     q_ref, kv_ref, pu_ref, gqkv_ref, sgr_ref, la_ref, h_ref) = refs[n_x:]
    D = D_MODEL
    R = ROW_TILE
    mods = (moda_ref[0], modb_ref[0])
    ropes = (ropea_ref, ropeb_ref)

    def norm(t):
        rows = pl.ds(t * R, R)
        h_ref[rows, :] = _modulated_norm(_stream_tile(x_refs, t, tiles_per_seq, n_ctx_tiles), g_ref[...],
                                         mods[t][:, D:2 * D], mods[t][:, 0:D]).astype(BF16)

    def project(t, after_first):
        rows = pl.ds(t * R, R)
        h = h_ref[rows, :]
        rope_ref = ropes[t]

        def proj(lo, width):
            return jnp.dot(h, w_ref[:, lo:lo + width], preferred_element_type=F32)

        cosf = rope_ref[:, 0:LANE]
        sin_lo = rope_ref[:, LANE:2 * LANE]
        sin_hi = rope_ref[:, 2 * LANE:3 * LANE]

        def rope(v):
            return (v * cosf + pltpu.roll(v, LANE - HEAD_DIM // 2, 1) * sin_lo
                    + pltpu.roll(v, HEAD_DIM // 2, 1) * sin_hi)

        aq = proj(_O_AQ, ATT_DIM)
        after_first()
        for j in range(ATT_DIM // LANE):
            q_ref[rows, j * LANE:(j + 1) * LANE] = (
                rope(aq[:, j * LANE:(j + 1) * LANE]) * (HEAD_DIM ** -0.5 * LOG2E)).astype(BF16)
        kv_ref[rows, 0:KV_DIM] = rope(proj(_O_AK, KV_DIM)).astype(BF16)
        kv_ref[rows, KV_DIM:2 * KV_DIM] = proj(_O_AV, KV_DIM).astype(BF16)

        pu_ref[rows, :] = proj(_O_PU, POOL_DIM)

        gqkv_ref[rows, 0:GLA_QK_DIM] = (proj(_O_GQ, GLA_QK_DIM) * (GLA_DK ** -0.5)).astype(BF16)
        gqkv_ref[rows, GLA_QK_DIM:] = proj(_O_GK, GLA_QK_DIM + GLA_V_DIM).astype(BF16)
        gr = proj(_O_GR, GLA_V_DIM)
        sgr_ref[rows, :] = (gr * _sigmoid(gr)).astype(BF16)

        glr = proj(_O_GLR, LANE)
        glr_hi = glr.astype(BF16)
        glr_lo = (glr - glr_hi.astype(F32)).astype(BF16)
        z = jnp.dot(jnp.concatenate([glr_hi, glr_lo, glr_hi], axis=1), wa_ref[...],
                    preferred_element_type=F32) + ba_ref[...]
        la_ref[rows, :] = (jnp.minimum(z, 0.0) - jnp.log(1.0 + jnp.exp(-jnp.abs(z)))) * (1.0 / GLA_TAU)

    norm(0)
    project(0, lambda: norm(1))
    project(1, lambda: None)


def _tile_of_pair(t, tiles_per_seq, n_ctx_tiles, ctx_row, layer):
    def seq_tile(s):
        return lax.rem(s * PAIR + t, tiles_per_seq)

    def mod_index(s):
        tile = s * PAIR + t
        return (layer, jnp.where(lax.rem(tile, tiles_per_seq) < n_ctx_tiles, ctx_row,
                                 lax.div(tile, tiles_per_seq)), 0, 0)

    return seq_tile, pl.BlockSpec((None, 1, 1, 6 * D_MODEL), mod_index)


def _in_proj(x_src, mods, g_pre, w_in, rope_tab, wa, ba, n_ctx_tiles, layer):
    if isinstance(x_src, tuple):
        B, D = x_src[0].shape[0], x_src[0].shape[2]
        T = x_src[0].shape[1] + x_src[1].shape[1]
    else:
        B, T, D = x_src.shape
    R = ROW_TILE
    nt = T // R
    assert (B * nt) % PAIR == 0
    ctx_row = mods.shape[1] - 1
    x_arrays, x_specs = _stream_specs(
        x_src, lambda s, t: (lax.div(s * PAIR + t, nt), lax.rem(s * PAIR + t, nt)), nt, n_ctx_tiles)

    def pair(width):
        return pl.BlockSpec((PAIR * R, width), lambda s: (s, 0))

    seq_tiles, mod_specs = zip(*(_tile_of_pair(t, nt, n_ctx_tiles, ctx_row, layer) for t in range(PAIR)))
    rope_specs = [pl.BlockSpec((R, 3 * LANE), lambda s, f=f: (f(s), 0)) for f in seq_tiles]
    widths = (ATT_DIM, 2 * KV_DIM, POOL_DIM, 2 * GLA_QK_DIM + GLA_V_DIM, GLA_V_DIM,
              2 * GLA_QK_DIM, D)
    dtypes = (BF16, BF16, F32, BF16, BF16, F32, BF16)
    kern = functools.partial(_in_proj_kernel, n_x=len(x_arrays), tiles_per_seq=nt, n_ctx_tiles=n_ctx_tiles)
    outs = pl.pallas_call(
        kern,
        grid=(B * nt // PAIR,),
        in_specs=[
            *x_specs, *mod_specs,
            _layer_spec((1, D), layer),
            _layer_spec((D, _O_MG), layer),
            *rope_specs,
            _layer_spec((3 * LANE, 2 * GLA_QK_DIM), layer),
            _layer_spec((1, 2 * GLA_QK_DIM), layer),
        ],
        out_specs=[pair(w) for w in widths],
        out_shape=[jax.ShapeDtypeStruct((B * T, w), dt) for w, dt in zip(widths, dtypes)],
        compiler_params=_params(48, 1),
        name="in_proj",
    )(*x_arrays, mods, mods, g_pre, w_in, rope_tab, rope_tab, wa, ba)
    return [o.reshape(B, T, -1) for o in outs]


def _attn_kernel(sink_ref, q_ref, kv_ref, o_ref, s_s, *, layer, tile_off, n_ctx_tiles, n_ctx, n_lat):
    R = ROW_TILE
    W = WINDOW
    assert R == 2 * W == 2 * LANE
    n_win = R // W + 2
    i = pl.program_id(1) + tile_off
    n_blk = (n_ctx + n_lat) // W
    nt_dims = (((1,), (1,)), ((), ()))
    q = q_ref[0]

    def heads(kv, split, merge):
        n_keys = kv.shape[0]
        ks, v_ext_ts = [], []
        for g in range(ATT_KV_HEADS):
            ks.append(kv[:, g * HEAD_DIM:(g + 1) * HEAD_DIM])
            v = kv[:, KV_DIM + g * HEAD_DIM:KV_DIM + (g + 1) * HEAD_DIM].astype(F32)
            v_ext_ts.append(
                jnp.concatenate([v, jnp.ones((n_keys, LANE - HEAD_DIM), F32)], axis=1).T.astype(BF16))

        n_slots = s_s.shape[0]

        def scores(h):
            qh = q[:, h * HEAD_DIM:(h + 1) * HEAD_DIM]
            s_s[h % n_slots, 0:n_keys, :] = lax.dot_general(ks[h // ATT_GROUP], qh, nt_dims,
                                                             preferred_element_type=F32)

        n_at_once = n_slots // 2
        for h in range(n_at_once):
            scores(h)
        for h0 in range(0, ATT_HEADS, n_at_once):
            batch = tuple(range(h0, h0 + n_at_once))
            for h in batch:
                if h + n_at_once < ATT_HEADS:
                    scores(h + n_at_once)
            sinks = [sink_ref[layer, h] * LOG2E for h in batch]
            halves = [split(s_s.at[h % n_slots]) for h in batch]
            ms = [[jnp.maximum(jnp.max(s_half, axis=0, keepdims=True), sk) for s_half in hs]
                  for hs, sk in zip(halves, sinks)]
            ps = [[jnp.exp2(s_half - m).astype(BF16) for s_half, m in zip(hs, mh)]
                  for hs, mh in zip(halves, ms)]
            o_exts = [jnp.dot(v_ext_ts[h // ATT_GROUP], merge(p), preferred_element_type=F32)
                      for h, p in zip(batch, ps)]
            out_t = [o_ext[0:HEAD_DIM] / (o_ext[HEAD_DIM:HEAD_DIM + 1]
                                          + jnp.exp2(sk - jnp.concatenate(mh, axis=1)))
                     for o_ext, sk, mh in zip(o_exts, sinks, ms)]
            for j in range(0, n_at_once, 2):
                o_ref[0, :, (h0 + j) * HEAD_DIM:(h0 + j + 2) * HEAD_DIM] = (
                    jnp.concatenate(out_t[j:j + 2], axis=0).T.astype(BF16))

    @pl.when(i < n_ctx_tiles)
    def _context_queries():
        heads(kv_ref[0, 0:n_ctx, :],
              lambda ref: [ref[0:n_ctx, a * LANE:(a + 1) * LANE] for a in range(R // LANE)],
              lambda ps: jnp.concatenate(ps, axis=1))

    @pl.when(i >= n_ctx_tiles)
    def _latent_queries():
        first_blk = i * (R // W) - 1
        parts = []
        for blk in range(n_win):
            idx = jnp.clip(first_blk + blk, 0, n_blk - 1)
            parts.append(kv_ref[0, pl.ds(pl.multiple_of(idx * W, W), W), :])
        parts.append(kv_ref[0, 0:n_ctx, :])
        n_loc = n_win * W
        kj = lax.broadcasted_iota(jnp.int32, (W, LANE), 0)
        qi = lax.broadcasted_iota(jnp.int32, (W, LANE), 1)
        below = jnp.where(kj >= qi, 0.0, NEG_BIG).astype(F32)
        above = jnp.where(kj <= qi, 0.0, NEG_BIG).astype(F32)
        below_first = below + jnp.where(i == n_ctx_tiles, NEG_BIG, 0.0)
        above_last = above + jnp.where(i == n_blk // (R // W) - 1, NEG_BIG, 0.0)

        def split(ref):
            def blk(b, a):
                return ref[b * W:(b + 1) * W, a * LANE:(a + 1) * LANE]
            return [jnp.concatenate([blk(0, 0) + below_first, blk(1, 0), blk(2, 0) + above,
                                     ref[n_loc:n_loc + n_ctx, 0:LANE]], axis=0),
                    jnp.concatenate([blk(1, 1) + below, blk(2, 1), blk(3, 1) + above_last,
                                     ref[n_loc:n_loc + n_ctx, LANE:2 * LANE]], axis=0)]

        def merge(ps):
            zero = jnp.zeros((W, LANE), BF16)
            live = (n_win - 1) * W
            return jnp.concatenate(
                [jnp.concatenate([ps[0][0:live], zero, ps[0][live:]], axis=0),
                 jnp.concatenate([zero, ps[1]], axis=0)], axis=1)

        heads(jnp.concatenate(parts, axis=0), split, merge)


def _attention(q, kv, sink, n_ctx, tile_off, layer):
    B, T, _ = q.shape
    R = ROW_TILE
    nt = T // R
    kern = functools.partial(_attn_kernel, layer=layer, tile_off=tile_off, n_ctx_tiles=n_ctx // R,
                             n_ctx=n_ctx, n_lat=T - n_ctx)
    return pl.pallas_call(
        kern,
        grid=(B, nt - tile_off),
        in_specs=[
            pl.BlockSpec(memory_space=pltpu.SMEM),
            pl.BlockSpec((1, R, ATT_DIM), lambda b, i: (b, i + tile_off, 0)),
            pl.BlockSpec((1, T, 2 * KV_DIM), lambda b, i: (b, 0, 0)),
        ],
        out_specs=pl.BlockSpec((1, R, ATT_DIM), lambda b, i: (b, i + tile_off, 0)),
        out_shape=jax.ShapeDtypeStruct((B, T, ATT_DIM), BF16),
        scratch_shapes=[pltpu.VMEM((2 * ATT_GROUP, R + 2 * WINDOW + n_ctx, R), F32)],
        compiler_params=_params(48, 2),
        name="attention",
    )(sink, q, kv)


def _gla_kernel(qkv_ref, la_ref, gn_ref, o_ref, of_s, ob_s, stf, stb, *, n_ctx):
    CH = GLA_CHUNK
    R = ROW_TILE
    NC = R // CH
    T = of_s.shape[0]
    n_tiles = T // R
    n_ctx_tiles = n_ctx // R
    QK = GLA_QK_DIM
    DV = GLA_V_DIM
    H = GLA_HEADS

    stf[...] = jnp.zeros(stf.shape, F32)
    stb[...] = jnp.zeros(stb.shape, F32)

    same_chunk = _iota_div((R, R), 0, CH) == _iota_div((R, R), 1, CH)
    ri = lax.broadcasted_iota(jnp.int32, (R, R), 0)
    ci = lax.broadcasted_iota(jnp.int32, (R, R), 1)
    tri_lo = (same_chunk & (ci <= ri)).astype(BF16)
    tri_up = (same_chunk & (ci >= ri)).astype(BF16)
    rs = jnp.bitwise_and(lax.broadcasted_iota(jnp.int32, (H * R, R), 0), R - 1)
    cs = lax.broadcasted_iota(jnp.int32, (H * R, R), 1)
    same_chunk_h = (lax.shift_right_logical(rs, CH.bit_length() - 1)
                    == lax.shift_right_logical(cs, CH.bit_length() - 1))
    keep_lo = same_chunk_h & (cs <= rs)
    keep_up = same_chunk_h & (cs >= rs)
    qk_head = _iota_div((H * R, QK), 0, R) == _iota_div((H * R, QK), 1, GLA_DK)
    chunk_of_row = _iota_div((R, QK), 0, CH)
    st_keep = _iota_div((DV, QK), 0, GLA_DV) == _iota_div((DV, QK), 1, GLA_DK)
    nt_dims = (((1,), (1,)), ((), ()))

    def split3(a):
        a1 = a.astype(BF16)
        r1 = a - a1.astype(F32)
        a2 = r1.astype(BF16)
        a3 = (r1 - a2.astype(F32)).astype(BF16)
        return jnp.concatenate([a1, a2, a3], axis=1)

    def rows_of_chunks(cum, offset):
        picks = [cum[c * CH + offset:c * CH + offset + 1, :] for c in range(NC)]
        full = jnp.concatenate([jnp.broadcast_to(p, (CH, QK)) for p in picks], axis=0)
        return picks, full

    def load(t, col0, tri):
        r0 = t * R if isinstance(t, int) else pl.multiple_of(t * R, R)
        a = la_ref[0, pl.ds(r0, R), col0:col0 + QK]
        c3 = jnp.dot(tri, split3(a), preferred_element_type=F32)
        return dict(r0=r0, cum=c3[:, 0:QK] + c3[:, QK:2 * QK] + c3[:, 2 * QK:3 * QK],
                    q=qkv_ref[0, pl.ds(r0, R), 0:QK].astype(F32),
                    k=qkv_ref[0, pl.ds(r0, R), QK:2 * QK].astype(F32),
                    v=qkv_ref[0, pl.ds(r0, R), 2 * QK:])

    def scale(d, last):
        cum, q, k = d["cum"], d["q"], d["k"]
        tots, tot_b = rows_of_chunks(cum, last)
        _, mid_b = rows_of_chunks(cum, CH // 2)
        d["dec"] = [jnp.exp(t) for t in tots]
        d["qe"] = (q * jnp.exp(cum)).astype(BF16)
        d["qm"] = (q * jnp.exp(cum - mid_b)).astype(BF16)
        km = k * jnp.exp(mid_b - cum)
        d["ks"] = jnp.where(qk_head, jnp.concatenate([km] * H, axis=0), 0.0).astype(BF16)
        kl = (k * jnp.exp(tot_b - cum)).astype(BF16)
        d["kl_blocks"] = jnp.concatenate(
            [jnp.where(chunk_of_row == c, kl, jnp.zeros_like(kl)) for c in range(NC)], axis=1)
        d["v_t"] = d["v"].astype(F32).T.astype(BF16)

    def scores(d, keep_t):
        att_t = lax.dot_general(d["ks"], d["qm"], nt_dims, preferred_element_type=F32)
        d["att_t"] = jnp.where(keep_t, att_t, 0.0).astype(BF16)
        kv = jnp.dot(d["v_t"], d["kl_blocks"], preferred_element_type=F32)
        d["kv"] = [kv[:, c * QK:(c + 1) * QK] for c in range(NC)]

    def intra(d):
        o_t = jnp.concatenate(
            [jnp.dot(d["v_t"][hd * GLA_DV:(hd + 1) * GLA_DV], d["att_t"][hd * R:(hd + 1) * R],
                     preferred_element_type=F32) for hd in range(H)], axis=0)
        d["o"] = o_t.T

    def inter(tiles, order, st_ref, o_scr):
        state = st_ref[...]
        for d in tiles:
            parts = [None] * NC
            for c in order:
                parts[c] = lax.dot_general(d["qe"][c * CH:(c + 1) * CH], state.astype(BF16), nt_dims,
                                           preferred_element_type=F32)
                state = state * d["dec"][c] + jnp.where(st_keep, d["kv"][c], 0.0)
            o_scr[pl.ds(d["r0"], R), :] = d["o"] + jnp.concatenate(parts, axis=0)
        st_ref[...] = state

    def advance(f_tiles, b_tiles):
        fs = [load(t, 0, tri_lo) for t in f_tiles]
        bs = [load(t, QK, tri_up) for t in b_tiles]
        for d in fs:
            scale(d, CH - 1)
        for d in bs:
            scale(d, 0)
        for d in fs:
            scores(d, keep_up)
        for d in bs:
            scores(d, keep_lo)
        for d in fs + bs:
            intra(d)
        inter(fs, range(NC), stf, of_s)
        inter(bs, range(NC - 1, -1, -1), stb, ob_s)

    for s in range(n_ctx_tiles):
        advance([s], [n_ctx_tiles - 1 - s])
    n_lat_tiles = n_tiles - n_ctx_tiles
    per_step = GLA_TILES_PER_STEP if n_lat_tiles % GLA_TILES_PER_STEP == 0 else 1

    def step(s, carry):
        first = n_ctx_tiles + s * per_step
        last = n_tiles - 1 - s * per_step
        advance([first + j for j in range(per_step)], [last - j for j in range(per_step)])
        return carry

    lax.fori_loop(0, n_lat_tiles // per_step, step, 0)

    gmean = (_iota_div((DV, DV), 0, GLA_DV) == _iota_div((DV, DV), 1, GLA_DV)).astype(BF16) * (1.0 / GLA_DV)
    gn = gn_ref[...]

    def finish(i, carry):
        r0 = pl.multiple_of(i * R, R)
        o = of_s[pl.ds(r0, R), :] + ob_s[pl.ds(r0, R), :]
        oo = o * o
        hi = oo.astype(BF16)
        lo = (oo - hi.astype(F32)).astype(BF16)
        ms = jnp.dot(hi, gmean, preferred_element_type=F32) + jnp.dot(lo, gmean, preferred_element_type=F32)
        o_ref[0, pl.ds(r0, R), :] = (o * lax.rsqrt(ms + EPS) * gn).astype(BF16)
        return carry

    lax.fori_loop(0, n_tiles, finish, 0)


def _gla(gqkv, la, gla_norm, n_ctx, layer):
    B, T, _ = gqkv.shape
    kern = functools.partial(_gla_kernel, n_ctx=n_ctx)
    return pl.pallas_call(
        kern,
        grid=(B,),
        in_specs=[
            pl.BlockSpec((1, T, 2 * GLA_QK_DIM + GLA_V_DIM), lambda b: (b, 0, 0)),
            pl.BlockSpec((1, T, 2 * GLA_QK_DIM), lambda b: (b, 0, 0)),
            _layer_spec((1, GLA_V_DIM), layer),
        ],
        out_specs=pl.BlockSpec((1, T, GLA_V_DIM), lambda b: (b, 0, 0)),
        out_shape=jax.ShapeDtypeStruct((B, T, GLA_V_DIM), BF16),
        scratch_shapes=[pltpu.VMEM((T, GLA_V_DIM), F32), pltpu.VMEM((T, GLA_V_DIM), F32),
                        pltpu.VMEM((GLA_V_DIM, GLA_QK_DIM), F32),
                        pltpu.VMEM((GLA_V_DIM, GLA_QK_DIM), F32)],
        compiler_params=_params(48, 1),
        name="gla",
    )(gqkv, la, gla_norm)


N_MERGE_TILED = 5


def _pool_tile(u, before, after, pos0, seq_len):
    R = u.shape[0]
    ext = jnp.concatenate([before, u, after], axis=0)
    n = ext.shape[0]

    def shifted(a, k):
        return pltpu.roll(a, (-k) % n, 0)

    w2 = ext + shifted(ext, -1)
    w4 = shifted(w2, 1) + shifted(w2, -1)
    w8 = shifted(w4, 2) + shifted(w4, -2)
    w16 = shifted(w8, 4) + shifted(w8, -4)
    sums = [a[SUBLANE:SUBLANE + R] for a in (w2, w4, w8, w16)]
    lane_group = _iota_div((R, POOL_DIM), 1, POOL_GROUP_DIM)
    pos = lax.broadcasted_iota(jnp.int32, (R, POOL_DIM), 0) + pos0
    tot = sums[0]
    lo = jnp.full((R, POOL_DIM), POOL_WINDOWS[0] // 2, jnp.int32)
    hi = jnp.full((R, POOL_DIM), POOL_WINDOWS[0] - POOL_WINDOWS[0] // 2 - 1, jnp.int32)
    for g in range(1, len(POOL_WINDOWS)):
        w = POOL_WINDOWS[g]
        tot = jnp.where(lane_group == g, sums[g], tot)
        lo = jnp.where(lane_group == g, w // 2, lo)
        hi = jnp.where(lane_group == g, w - w // 2 - 1, hi)
    cnt = (jnp.minimum(pos + hi + 1, seq_len) - jnp.maximum(pos - lo, 0)).astype(F32)
    return tot / cnt - u


def _merge_kernel(*refs, n_x, n_ctx_tiles, tile_off, kept, n_ctx, n_lat):
    D = D_MODEL
    R = ROW_TILE
    x_refs, refs = refs[:n_x], refs[n_x:]
    n_tiled = PAIR * N_MERGE_TILED
    tiled = refs[:n_tiled]
    halos = refs[n_tiled:n_tiled + 2 * PAIR]
    mod_refs = refs[n_tiled + 2 * PAIR:n_tiled + 3 * PAIR]
    (gpost_ref, wba_ref, pbd_ref, ps_ref, wbp_ref, wbg_ref, wmg0_ref, wmg1_ref, wmg2_ref, wo_ref,
     o_ref) = refs[n_tiled + 3 * PAIR:]
    wmg_refs = (wmg0_ref, wmg1_ref, wmg2_ref)
    tiles_per_seq = n_ctx_tiles + n_lat // R

    for t in range(PAIR):
        att_ref, pu_ref, on_ref, sgr_ref, h_ref = tiled[t::PAIR]
        h = h_ref[...]

        def gate(j):
            return _sigmoid(jnp.dot(h, wmg_refs[j][...], preferred_element_type=F32))

        m = gate(0) * jnp.dot(att_ref[...], wba_ref[...], preferred_element_type=F32)

        i = lax.rem(pl.program_id(0) * PAIR + t, kept) + tile_off
        is_ctx = i < n_ctx_tiles
        seq_start = (i == 0) | (i == n_ctx_tiles)
        seq_end = (i == n_ctx_tiles - 1) | (i == tiles_per_seq - 1)
        before = jnp.where(seq_start, 0.0, halos[2 * t][...])
        after = jnp.where(seq_end, 0.0, halos[2 * t + 1][...])
        d_pool = _pool_tile(pu_ref[...], before, after, jnp.where(is_ctx, i, i - n_ctx_tiles) * R,
                            jnp.where(is_ctx, n_ctx, n_lat))
        yp = jnp.dot(d_pool.astype(BF16), pbd_ref[...], preferred_element_type=F32) * ps_ref[...]
        m = m + gate(1) * jnp.dot(yp.astype(BF16), wbp_ref[...], preferred_element_type=F32)
        m = m + gate(2) * jnp.dot(on_ref[...] * sgr_ref[...], wbg_ref[...], preferred_element_type=F32)
        y = jnp.dot(m.astype(BF16), wo_ref[...], preferred_element_type=F32)
        ms = jnp.mean(y * y, axis=-1, keepdims=True)
        o_ref[t * R:(t + 1) * R, :] = (
            _stream_tile(x_refs, t, tiles_per_seq, n_ctx_tiles)
            + mod_refs[t][0][:, 2 * D:3 * D] * (y * lax.rsqrt(ms + EPS) * gpost_ref[...]))


def _merge(x_src, y_att, pu, on, sgr, h, mods, g_post, wba, pbd, ps, wbp, wbg, wmg, wo,
           n_ctx_tiles, tile_off, layer):
    B, T, _ = y_att.shape
    D = D_MODEL
    R = ROW_TILE
    H = SUBLANE
    nt = T // R
    kept = nt - tile_off
    assert (B * kept) % PAIR == 0
    ctx_row = mods.shape[1] - 1
    rpb = R // H
    last_blk = B * T // H - 1

    def halo_spec(t, after):
        def index(s):
            b, i = coords(s, t)
            tile = b * nt + i
            return ((jnp.minimum((tile + 1) * rpb, last_blk) if after else jnp.maximum(tile * rpb - 1, 0)), 0)
        return pl.BlockSpec((H, POOL_DIM), index)

    def coords(s, t):
        tile = s * PAIR + t
        return lax.div(tile, kept), lax.rem(tile, kept) + tile_off

    def tile_specs(width):
        def spec(t):
            def index(s):
                b, i = coords(s, t)
                return (b * nt + i, 0)
            return pl.BlockSpec((R, width), index)
        return [spec(t) for t in range(PAIR)]

    def mod_spec(t):
        def index(s):
            b, i = coords(s, t)
            return (layer, jnp.where(i < n_ctx_tiles, ctx_row, b), 0, 0)
        return pl.BlockSpec((None, 1, 1, 6 * D), index)

    tiled = (y_att, pu, on, sgr, h)
    assert len(tiled) == N_MERGE_TILED
    flat = [a.reshape(B * T, a.shape[-1]) for a in tiled]
    pu_flat = flat[1]
    if isinstance(x_src, tuple):
        assert tile_off == 0
        x_arrays, x_specs = _stream_specs(x_src, coords, nt, n_ctx_tiles)
    else:
        x_arrays, x_specs = [x_src.reshape(B * T, D)] * PAIR, tile_specs(D)
    kern = functools.partial(_merge_kernel, n_x=len(x_arrays), n_ctx_tiles=n_ctx_tiles, tile_off=tile_off,
                             kept=kept, n_ctx=n_ctx_tiles * R, n_lat=T - n_ctx_tiles * R)
    out = pl.pallas_call(
        kern,
        grid=(B * kept // PAIR,),
        in_specs=[
            *x_specs,
            *(spec for a in flat for spec in tile_specs(a.shape[-1])),
            *(halo_spec(t, after) for t in range(PAIR) for after in (False, True)),
            *(mod_spec(t) for t in range(PAIR)),
            _layer_spec((1, D), layer),
            _layer_spec((ATT_DIM, D), layer), _layer_spec((POOL_DIM, POOL_DIM), layer),
            _layer_spec((1, POOL_DIM), layer), _layer_spec((POOL_DIM, D), layer),
            _layer_spec((GLA_V_DIM, D), layer),
            *(pl.BlockSpec((None, D, D), lambda s, j=j: (layer, 0, _O_MG // D + j),
                           pipeline_mode=pl.Buffered(1)) for j in range(3)),
            _layer_spec((D, D), layer),
        ],
        out_specs=pl.BlockSpec((PAIR * R, D), lambda s: (s, 0)),
        out_shape=jax.ShapeDtypeStruct((B * kept * R, D), F32),
        compiler_params=_params(48, 1),
        name="merge",
    )(*x_arrays, *(a for a in flat for _ in range(PAIR)), *([pu_flat] * (2 * PAIR)), *([mods] * PAIR),
      g_post, wba, pbd, ps, wbp, wbg, wmg, wmg, wmg, wo)
    return out.reshape(B, kept * R, D)


def _ffn_kernel(x_ref, xpa_ref, xna_ref, xpb_ref, xnb_ref, moda_ref, modb_ref, gpre_ref, wup_ref, cw_ref,
                cb_ref, wdn_ref, gpost_ref, o_ref, he_s, u_s, act_s, acc_s, *, n_ctx_tiles, tiles_per_seq):
    D = D_MODEL
    R = ROW_TILE
    S = SUBLANE
    G = R // S
    CK = FF_CHUNK
    n_chunks = D_FF // CK
    halo_refs = ((xpa_ref, xna_ref), (xpb_ref, xnb_ref))
    mods = (moda_ref[0], modb_ref[0])
    r_idx = lax.broadcasted_iota(jnp.int32, (R, R), 0)
    c_idx = lax.broadcasted_iota(jnp.int32, (R, R), 1)

    def seq_of(p):
        return jnp.bitwise_and(p, S - 1) * G + lax.shift_right_logical(p, SUBLANE_LOG2)

    to_perm = (c_idx == seq_of(r_idx)).astype(BF16)
    to_seq = (r_idx == seq_of(c_idx)).astype(BF16)
    sub = lax.broadcasted_iota(jnp.int32, (S, 2 * CK), 0)
    halo_row = lax.broadcasted_iota(jnp.int32, (S, 1), 0)

    def chunk_cols(ref, j):
        return jnp.concatenate([ref[:, j * CK:(j + 1) * CK], ref[:, D_FF + j * CK:D_FF + (j + 1) * CK]],
                               axis=1)

    def prologue(t):
        i = lax.rem(pl.program_id(0) * PAIR + t, tiles_per_seq)
        scale, shift = mods[t][:, 4 * D:5 * D], mods[t][:, 3 * D:4 * D]
        h = _modulated_norm(x_ref[t * R:(t + 1) * R, :], gpre_ref[...], scale, shift).astype(BF16)
        he_s[t, 0:R, :] = jnp.dot(to_perm, h, preferred_element_type=F32).astype(BF16)
        xp_ref, xn_ref = halo_refs[t]
        halo = jnp.concatenate([xp_ref[S - 1:S, :], xn_ref[0:1, :], jnp.zeros((S - 2, D), F32)], axis=0)
        seq_start = (i == 0) | (i == n_ctx_tiles)
        seq_end = (i == n_ctx_tiles - 1) | (i == tiles_per_seq - 1)
        outside = ((halo_row == 0) & seq_start) | ((halo_row == 1) & seq_end) | (halo_row >= 2)
        he_s[t, R:R + S, :] = jnp.where(
            outside, 0.0, _modulated_norm(halo, gpre_ref[...], scale, shift)).astype(BF16)

    def up(t, j):
        he = he_s[t]
        u_s[t, j % 2, :, 0:CK] = jnp.dot(he, wup_ref[:, j * CK:(j + 1) * CK], preferred_element_type=F32)
        u_s[t, j % 2, :, CK:] = jnp.dot(he, wup_ref[:, D_FF + j * CK:D_FF + (j + 1) * CK],
                                        preferred_element_type=F32)

    def finish(t, j):
        u = u_s[t, j % 2, 0:R, :]
        edge = u_s[t, j % 2, R:R + S, :]
        prev0 = jnp.where(sub == 0, edge[0:1], pltpu.roll(u[R - S:R], 1, 0))
        next_last = jnp.where(sub == S - 1, edge[1:2], pltpu.roll(u[0:S], S - 1, 0))
        prev = jnp.concatenate([prev0, u[0:R - S]], axis=0)
        nxt = jnp.concatenate([u[S:R], next_last], axis=0)
        cw = chunk_cols(cw_ref, j)
        c = prev * cw[0:1] + u * cw[1:2] + nxt * cw[2:3] + chunk_cols(cb_ref, j)
        g = c[:, CK:]
        act_s[t, :, j * CK:(j + 1) * CK] = (c[:, :CK] * (g * _sigmoid(g))).astype(BF16)

    def down(t, j0, j1):
        y = jnp.dot(act_s[t, :, j0 * CK:j1 * CK], wdn_ref[j0 * CK:j1 * CK, :], preferred_element_type=F32)
        if j0 == 0:
            acc_s[t] = y
        else:
            acc_s[t] += y

    def epilogue(t):
        acc = acc_s[t]
        acc_hi = acc.astype(BF16)
        acc_lo = (acc - acc_hi.astype(F32)).astype(BF16)
        y = (jnp.dot(to_seq, acc_hi, preferred_element_type=F32)
             + jnp.dot(to_seq, acc_lo, preferred_element_type=F32))
        ms = jnp.mean(y * y, axis=-1, keepdims=True)
        o_ref[t * R:(t + 1) * R, :] = (x_ref[t * R:(t + 1) * R, :]
                                       + mods[t][:, 5 * D:6 * D] * (y * lax.rsqrt(ms + EPS) * gpost_ref[...]))

    def body(t, after_first_up):
        up(t, 0)
        after_first_up()
        group_start = 0
        for j in range(n_chunks):
            if j + 1 < n_chunks:
                up(t, j + 1)
            finish(t, j)
            if j + 1 - group_start == FF_GROUP or j + 1 == n_chunks:
                down(t, group_start, j + 1)
                group_start = j + 1

    prologue(0)
    body(0, lambda: prologue(1))
    body(1, lambda: epilogue(0))
    epilogue(1)


def _ffn(xa, mods, g_pre, w_up, conv_w, conv_b, w_down, g_post, n_ctx_tiles, layer):
    B, T, D = xa.shape
    R = ROW_TILE
    H = SUBLANE
    nt = T // R
    assert (B * nt) % PAIR == 0 and PAIR == 2
    ctx_row = mods.shape[1] - 1
    rpb = R // H
    RE = R + H
    last_blk = B * T // H - 1

    def halo(t, after):
        def index(s):
            tile = s * PAIR + t
            return ((jnp.minimum((tile + 1) * rpb, last_blk) if after else jnp.maximum(tile * rpb - 1, 0)), 0)
        return pl.BlockSpec((H, D), index)

    def mod_spec(t):
        def index(s):
            tile = s * PAIR + t
            return (layer, jnp.where(lax.rem(tile, nt) < n_ctx_tiles, ctx_row, lax.div(tile, nt)), 0, 0)
        return pl.BlockSpec((None, 1, 1, 6 * D), index)

    x2 = xa.reshape(B * T, D)
    kern = functools.partial(_ffn_kernel, n_ctx_tiles=n_ctx_tiles, tiles_per_seq=nt)
    out = pl.pallas_call(
        kern,
        grid=(B * nt // PAIR,),
        in_specs=[
            pl.BlockSpec((PAIR * R, D), lambda s: (s, 0)),
            halo(0, False), halo(0, True), halo(1, False), halo(1, True),
            mod_spec(0), mod_spec(1),
            _layer_spec((1, D), layer),
            _layer_spec((D, 2 * D_FF), layer), _layer_spec((3, 2 * D_FF), layer),
            _layer_spec((1, 2 * D_FF), layer), _layer_spec((D_FF, D), layer),
            _layer_spec((1, D), layer),
        ],
        out_specs=pl.BlockSpec((PAIR * R, D), lambda s: (s, 0)),
        out_shape=jax.ShapeDtypeStruct((B * T, D), F32),
        scratch_shapes=[pltpu.VMEM((PAIR, RE, D), BF16),
                        pltpu.VMEM((PAIR, 2, RE, 2 * FF_CHUNK), F32),
                        pltpu.VMEM((PAIR, R, D_FF), BF16),
                        pltpu.VMEM((PAIR, R, D), F32)],
        compiler_params=_params(56, 1),
        name="ffn",
    )(x2, x2, x2, x2, x2, mods, mods, g_pre, w_up, conv_w, conv_b, w_down, g_post)
    return out.reshape(B, T, D)


def _rope_table(n_ctx, n_lat):
    rows = n_lat // GRID_W
    row = jnp.repeat(jnp.arange(rows, dtype=F32), GRID_W)
    col = jnp.tile(jnp.arange(GRID_W, dtype=F32), rows)
    n = HEAD_DIM // 4
    inv = ROPE_BASE ** (-jnp.arange(n, dtype=F32) / n)
    ang = jnp.concatenate([row[:, None] * inv, col[:, None] * inv], axis=-1)
    cos, sin = jnp.cos(ang), jnp.sin(ang)
    zero = jnp.zeros_like(sin)
    reps = LANE // HEAD_DIM
    cos_t = jnp.tile(jnp.concatenate([cos, cos], axis=-1), (1, reps))
    sin_lo = jnp.tile(jnp.concatenate([-sin, zero], axis=-1), (1, reps))
    sin_hi = jnp.tile(jnp.concatenate([zero, sin], axis=-1), (1, reps))
    lat = jnp.concatenate([cos_t, sin_lo, sin_hi], axis=-1)
    ctx = jnp.concatenate([jnp.ones((n_ctx, LANE), F32), jnp.zeros((n_ctx, 2 * LANE), F32)], axis=-1)
    return jnp.concatenate([ctx, lat], axis=0)


def kernel(x, c, ctx, c_ctx, w_ada, b_ada, g_pre_mix, g_post_mix, g_pre_ffn, g_post_ffn, w_in, att_sink, pool_w, pool_scale, gla_wa2, gla_ba, gla_norm, w_br_att, w_br_pool, w_br_gla, w_o, w_up, conv_w, conv_b, w_down):
    B, L, D = x.shape
    C = ctx.shape[1]
    depth = w_in.shape[0]
    R = ROW_TILE
    assert D == D_MODEL and C % R == 0 and L % R == 0 and L % GRID_W == 0
    n_ctx_tiles = C // R

    mod_rows = -(-(B + 1) // SUBLANE) * SUBLANE
    cc = jnp.zeros((mod_rows, D), F32).at[:B].set(c).at[mod_rows - 1].set(c_ctx)
    mods = _ada_table(cc, w_ada, b_ada).reshape(depth, mod_rows, 1, 6 * D)

    rope_tab = _rope_table(C, L)
    split = _O_GLR + 2 * GLA_GATE_RANK
    w_in_p = jnp.concatenate(
        [w_in[..., :split], jnp.zeros((depth, D, _O_MG - split), w_in.dtype), w_in[..., split:]],
        axis=-1).astype(BF16)
    w_mg = w_in_p
    wa = jnp.zeros((depth, LANE, 2 * GLA_QK_DIM), F32)
    wa = wa.at[:, 0:GLA_GATE_RANK, 0:GLA_QK_DIM].set(gla_wa2[:, 0])
    wa = wa.at[:, GLA_GATE_RANK:2 * GLA_GATE_RANK, GLA_QK_DIM:].set(gla_wa2[:, 1])
    wa_hi = wa.astype(BF16)
    wa = jnp.concatenate([wa_hi, wa_hi, (wa - wa_hi.astype(F32)).astype(BF16)], axis=1)
    ba = gla_ba.reshape(depth, 1, 2 * GLA_QK_DIM)
    pbd = jnp.zeros((depth, POOL_DIM, POOL_DIM), F32)
    for g in range(len(POOL_WINDOWS)):
        sl = slice(g * POOL_GROUP_DIM, (g + 1) * POOL_GROUP_DIM)
        pbd = pbd.at[:, sl, sl].set(pool_w[:, g])
    pbd = pbd.astype(BF16)
    wba, wbp, wbg, wo = (w.astype(BF16) for w in (w_br_att, w_br_pool, w_br_gla, w_o))
    wup, wdn = w_up.astype(BF16), w_down.astype(BF16)

    def rows(t):
        return t.reshape(depth, 1, -1)

    g_pre_mix, g_post_mix, g_pre_ffn, g_post_ffn, pool_scale, gla_norm, conv_b = (
        rows(t) for t in (g_pre_mix, g_post_mix, g_pre_ffn, g_post_ffn, pool_scale, gla_norm, conv_b))

    xa = (ctx, x) if depth > 1 else jnp.concatenate([ctx, x], axis=1)
    for l in range(depth):
        last = l == depth - 1
        off = n_ctx_tiles if last else 0
        q, kv, pu, gqkv, sgr, la, h = _in_proj(xa, mods, g_pre_mix, w_in_p, rope_tab, wa, ba, n_ctx_tiles, l)
        y_att = _attention(q, kv, att_sink, C, off, l)
        on = _gla(gqkv, la, gla_norm, C, l)
        xa = _merge(xa, y_att, pu, on, sgr, h, mods, g_post_mix, wba, pbd, pool_scale, wbp, wbg, w_mg, wo,
                    n_ctx_tiles, off, l)
        xa = _ffn(xa, mods, g_pre_ffn, wup, conv_w, conv_b, wdn, g_post_ffn, n_ctx_tiles - off, l)
    return xa
```

```python
import functools

import jax
import jax.numpy as jnp
import numpy as np
from jax import lax
from jax.experimental import pallas as pl
from jax.experimental.pallas import tpu as pltpu

F32 = jnp.float32
BF16 = jnp.bfloat16
HIGHEST = lax.Precision.HIGHEST

D_MODEL = 1024
GRID_W = 64
EPS = 1e-6

HEAD_DIM = 64
ATT_HEADS = 8
ATT_KV_HEADS = 2
ATT_GROUP = ATT_HEADS // ATT_KV_HEADS
WINDOW = 128
ROPE_BASE = 10000.0
ATT_DIM = ATT_HEADS * HEAD_DIM
KV_DIM = ATT_KV_HEADS * HEAD_DIM

POOL_WINDOWS = (2, 4, 8, 16)
POOL_GROUP_DIM = 64
POOL_DIM = len(POOL_WINDOWS) * POOL_GROUP_DIM

GLA_HEADS = 4
GLA_DK = 32
GLA_DV = 64
GLA_GATE_RANK = 16
GLA_TAU = 16.0
N_BRANCH = 3
GLA_QK_DIM = GLA_HEADS * GLA_DK
GLA_V_DIM = GLA_HEADS * GLA_DV
GLA_CHUNK = 64
GLA_TILES_PER_STEP = 4

D_FF = 2816
FF_CHUNK = 256
FF_GROUP = 4
PAIR = 2

LANE = 128
SUBLANE = 8
SUBLANE_LOG2 = SUBLANE.bit_length() - 1
ROW_TILE = 256

_O_AQ = 0
_O_AK = _O_AQ + ATT_DIM
_O_AV = _O_AK + KV_DIM
_O_PU = _O_AV + KV_DIM
_O_GQ = _O_PU + POOL_DIM
_O_GK = _O_GQ + GLA_QK_DIM
_O_GV = _O_GK + GLA_QK_DIM
_O_GR = _O_GV + GLA_V_DIM
_O_GLR = _O_GR + GLA_V_DIM
IN_DIM_PAD = _O_GLR + LANE
_O_MG = _O_GLR + 2 * GLA_GATE_RANK
W_SPLIT_ROWS = 256

NEG_BIG = -1e30
LOG2E = float(np.log2(np.e))


def _params(vmem_mb, n_axes):
    return pltpu.CompilerParams(
        dimension_semantics=("arbitrary",) * n_axes,
        vmem_limit_bytes=vmem_mb * 1024 * 1024,
    )


def _layer_spec(shape, layer):
    nd = len(shape)
    return pl.BlockSpec((None,) + tuple(shape), lambda *_: (layer,) + (0,) * nd,
                        pipeline_mode=pl.Buffered(1))


def _sigmoid(v):
    return 1.0 / (1.0 + jnp.exp(-v))


def _iota_div(shape, axis, divisor):
    shift = int(np.log2(divisor))
    assert 1 << shift == divisor
    return lax.shift_right_logical(lax.broadcasted_iota(jnp.int32, shape, axis), shift)


def _modulated_norm(x, gain, scale, shift):
    ms = jnp.mean(x * x, axis=-1, keepdims=True)
    return (x * lax.rsqrt(ms + EPS) * gain) * (1.0 + scale) + shift


def _ada_kernel(c_ref, w_ref, b_ref, o_ref):
    c = c_ref[...]
    a = c * _sigmoid(c)
    o_ref[0] = jnp.dot(a, w_ref[0], precision=HIGHEST, preferred_element_type=F32) + b_ref[0]


def _ada_table(cc, w_ada, b_ada):
    depth = w_ada.shape[0]
    rows = cc.shape[0]
    width = 2 * D_MODEL
    n_col = w_ada.shape[2] // width
    return pl.pallas_call(
        _ada_kernel,
        grid=(depth, n_col),
        in_specs=[
            pl.BlockSpec((rows, D_MODEL), lambda l, j: (0, 0)),
            pl.BlockSpec((1, D_MODEL, width), lambda l, j: (l, 0, j)),
            pl.BlockSpec((1, 1, width), lambda l, j: (l, 0, j)),
        ],
        out_specs=pl.BlockSpec((1, rows, width), lambda l, j: (l, 0, j)),
        out_shape=jax.ShapeDtypeStruct((depth, rows, w_ada.shape[2]), F32),
        compiler_params=_params(32, 2),
        name="ada_table",
    )(cc, w_ada, b_ada.reshape(depth, 1, -1))


def _split_w_in_kernel(w_ref, wp_ref, wg_ref):
    wp_ref[0] = w_ref[0, :, 0:IN_DIM_PAD].astype(BF16)
    wg_ref[0] = w_ref[0, :, _O_MG:_O_MG + N_BRANCH * D_MODEL].astype(BF16)


def _split_w_in(w_in):
    depth, D, n_in = w_in.shape
    assert n_in == _O_MG + N_BRANCH * D and D % W_SPLIT_ROWS == 0
    rows = W_SPLIT_ROWS
    return pl.pallas_call(
        _split_w_in_kernel,
        grid=(depth, D // rows),
        in_specs=[pl.BlockSpec((1, rows, n_in), lambda l, r: (l, r, 0))],
        out_specs=[pl.BlockSpec((1, rows, IN_DIM_PAD), lambda l, r: (l, r, 0)),
                   pl.BlockSpec((1, rows, N_BRANCH * D), lambda l, r: (l, r, 0))],
        out_shape=[jax.ShapeDtypeStruct((depth, D, IN_DIM_PAD), BF16),
                   jax.ShapeDtypeStruct((depth, D, N_BRANCH * D), BF16)],
        compiler_params=_params(48, 2),
        name="split_w_in",
    )(w_in)


def _stream_tile(x_refs, t, tiles_per_seq, n_ctx_tiles):
    if len(x_refs) == 1:
        return x_refs[0][t * ROW_TILE:(t + 1) * ROW_TILE, :]
    if len(x_refs) == PAIR:
        return x_refs[t][...]
    i = lax.rem(pl.program_id(0) * PAIR + t, tiles_per_seq)
    return jnp.where(i < n_ctx_tiles, x_refs[t][...], x_refs[PAIR + t][...])


def _stream_specs(x_src, tile_coords, tiles_per_seq, n_ctx_tiles):
    R = ROW_TILE
    if not isinstance(x_src, tuple):
        B, T, D = x_src.shape
        return [x_src.reshape(B * T, D)], [pl.BlockSpec((PAIR * R, D), lambda s: (s, 0))]
    ctx, lat = x_src
    B, _, D = ctx.shape
    n_lat_tiles = tiles_per_seq - n_ctx_tiles

    def spec(t, is_ctx):
        def index(s):
            b, i = tile_coords(s, t)
            if is_ctx:
                return (b * n_ctx_tiles + jnp.minimum(i, n_ctx_tiles - 1), 0)
            return (b * n_lat_tiles + jnp.maximum(i - n_ctx_tiles, 0), 0)
        return pl.BlockSpec((R, D), index)

    arrays = [ctx.reshape(-1, D)] * PAIR + [lat.reshape(-1, D)] * PAIR
    return arrays, [spec(t, True) for t in range(PAIR)] + [spec(t, False) for t in range(PAIR)]


def _in_proj_kernel(*refs, n_x, tiles_per_seq, n_ctx_tiles):
    x_refs = refs[:n_x]
    (moda_ref, modb_ref, g_ref, w_ref, ropea_ref, ropeb_ref, wa_ref, ba_ref,
     q_ref, kv_ref, pu_ref, gqkv_ref, sgr_ref, la_ref, h_ref) = refs[n_x:]
    D = D_MODEL
    R = ROW_TILE
    mods = (moda_ref[0], modb_ref[0])
    ropes = (ropea_ref, ropeb_ref)

    def norm(t):
        rows = pl.ds(t * R, R)
        h_ref[rows, :] = _modulated_norm(_stream_tile(x_refs, t, tiles_per_seq, n_ctx_tiles), g_ref[...],
                                         mods[t][:, D:2 * D], mods[t][:, 0:D]).astype(BF16)

    def project(t, after_first):
        rows = pl.ds(t * R, R)
        h = h_ref[rows, :]
        rope_ref = ropes[t]

        def proj(lo, width):
            return jnp.dot(h, w_ref[:, lo:lo + width], preferred_element_type=F32)

        cosf = rope_ref[:, 0:LANE]
        sin_lo = rope_ref[:, LANE:2 * LANE]
        sin_hi = rope_ref[:, 2 * LANE:3 * LANE]

        def rope(v):
            return (v * cosf + pltpu.roll(v, LANE - HEAD_DIM // 2, 1) * sin_lo
                    + pltpu.roll(v, HEAD_DIM // 2, 1) * sin_hi)

        aq = proj(_O_AQ, ATT_DIM)
        after_first()
        for j in range(ATT_DIM // LANE):
            q_ref[rows, j * LANE:(j + 1) * LANE] = (
                rope(aq[:, j * LANE:(j + 1) * LANE]) * (HEAD_DIM ** -0.5 * LOG2E)).astype(BF16)
        kv_ref[rows, 0:KV_DIM] = rope(proj(_O_AK, KV_DIM)).astype(BF16)
        kv_ref[rows, KV_DIM:2 * KV_DIM] = proj(_O_AV, KV_DIM).astype(BF16)

        pu_ref[rows, :] = proj(_O_PU, POOL_DIM)

        gqkv_ref[rows, 0:GLA_QK_DIM] = (proj(_O_GQ, GLA_QK_DIM) * (GLA_DK ** -0.5)).astype(BF16)
        gqkv_ref[rows, GLA_QK_DIM:] = proj(_O_GK, GLA_QK_DIM + GLA_V_DIM).astype(BF16)
        gr = proj(_O_GR, GLA_V_DIM)
        sgr_ref[rows, :] = (gr * _sigmoid(gr)).astype(BF16)

        glr = proj(_O_GLR, LANE)
        glr_hi = glr.astype(BF16)
        glr_lo = (glr - glr_hi.astype(F32)).astype(BF16)
        z = jnp.dot(jnp.concatenate([glr_hi, glr_lo, glr_hi], axis=1), wa_ref[...],
                    preferred_element_type=F32) + ba_ref[...]
        la_ref[rows, :] = (jnp.minimum(z, 0.0) - jnp.log(1.0 + jnp.exp(-jnp.abs(z)))) * (1.0 / GLA_TAU)

    norm(0)
    project(0, lambda: norm(1))
    project(1, lambda: None)


def _tile_of_pair(t, tiles_per_seq, n_ctx_tiles, ctx_row, layer):
    def seq_tile(s):
        return lax.rem(s * PAIR + t, tiles_per_seq)

    def mod_index(s):
        tile = s * PAIR + t
        return (layer, jnp.where(lax.rem(tile, tiles_per_seq) < n_ctx_tiles, ctx_row,
                                 lax.div(tile, tiles_per_seq)), 0, 0)

    return seq_tile, pl.BlockSpec((None, 1, 1, 6 * D_MODEL), mod_index)


def _in_proj(x_src, mods, g_pre, w_in, rope_tab, wa, ba, n_ctx_tiles, layer):
    if isinstance(x_src, tuple):
        B, D = x_src[0].shape[0], x_src[0].shape[2]
        T = x_src[0].shape[1] + x_src[1].shape[1]
    else:
        B, T, D = x_src.shape
    R = ROW_TILE
    nt = T // R
    assert (B * nt) % PAIR == 0
    ctx_row = mods.shape[1] - 1
    x_arrays, x_specs = _stream_specs(
        x_src, lambda s, t: (lax.div(s * PAIR + t, nt), lax.rem(s * PAIR + t, nt)), nt, n_ctx_tiles)

    def pair(width):
        return pl.BlockSpec((PAIR * R, width), lambda s: (s, 0))

    seq_tiles, mod_specs = zip(*(_tile_of_pair(t, nt, n_ctx_tiles, ctx_row, layer) for t in range(PAIR)))
    rope_specs = [pl.BlockSpec((R, 3 * LANE), lambda s, f=f: (f(s), 0)) for f in seq_tiles]
    widths = (ATT_DIM, 2 * KV_DIM, POOL_DIM, 2 * GLA_QK_DIM + GLA_V_DIM, GLA_V_DIM,
              2 * GLA_QK_DIM, D)
    dtypes = (BF16, BF16, F32, BF16, BF16, F32, BF16)
    kern = functools.partial(_in_proj_kernel, n_x=len(x_arrays), tiles_per_seq=nt, n_ctx_tiles=n_ctx_tiles)
    outs = pl.pallas_call(
        kern,
        grid=(B * nt // PAIR,),
        in_specs=[
            *x_specs, *mod_specs,
            _layer_spec((1, D), layer),
            _layer_spec((D, IN_DIM_PAD), layer),
            *rope_specs,
            _layer_spec((3 * LANE, 2 * GLA_QK_DIM), layer),
            _layer_spec((1, 2 * GLA_QK_DIM), layer),
        ],
        out_specs=[pair(w) for w in widths],
        out_shape=[jax.ShapeDtypeStruct((B * T, w), dt) for w, dt in zip(widths, dtypes)],
        compiler_params=_params(48, 1),
        name="in_proj",
    )(*x_arrays, mods, mods, g_pre, w_in, rope_tab, rope_tab, wa, ba)
    return [o.reshape(B, T, -1) for o in outs]


def _attn_kernel(sink_ref, q_ref, kv_ref, o_ref, s_s, *, layer, tile_off, n_ctx_tiles, n_ctx, n_lat):
    R = ROW_TILE
    W = WINDOW
    assert R == 2 * W == 2 * LANE
    n_win = R // W + 2
    i = pl.program_id(1) + tile_off
    n_blk = (n_ctx + n_lat) // W
    nt_dims = (((1,), (1,)), ((), ()))
    q = q_ref[0]

    def heads(kv, split, merge):
        n_keys = kv.shape[0]
        ks, v_ext_ts = [], []
        for g in range(ATT_KV_HEADS):
            ks.append(kv[:, g * HEAD_DIM:(g + 1) * HEAD_DIM])
            v = kv[:, KV_DIM + g * HEAD_DIM:KV_DIM + (g + 1) * HEAD_DIM].astype(F32)
            v_ext_ts.append(
                jnp.concatenate([v, jnp.ones((n_keys, LANE - HEAD_DIM), F32)], axis=1).T.astype(BF16))

        n_slots = s_s.shape[0]

        def scores(h):
            qh = q[:, h * HEAD_DIM:(h + 1) * HEAD_DIM]
            s_s[h % n_slots, 0:n_keys, :] = lax.dot_general(ks[h // ATT_GROUP], qh, nt_dims,
                                                             preferred_element_type=F32)

        n_at_once = n_slots // 2
        for h in range(n_at_once):
            scores(h)
        for h0 in range(0, ATT_HEADS, n_at_once):
            batch = tuple(range(h0, h0 + n_at_once))
            for h in batch:
                if h + n_at_once < ATT_HEADS:
                    scores(h + n_at_once)
            sinks = [sink_ref[layer, h] * LOG2E for h in batch]
            halves = [split(s_s.at[h % n_slots]) for h in batch]
            ms = [[jnp.maximum(jnp.max(s_half, axis=0, keepdims=True), sk) for s_half in hs]
                  for hs, sk in zip(halves, sinks)]
            ps = [[jnp.exp2(s_half - m).astype(BF16) for s_half, m in zip(hs, mh)]
                  for hs, mh in zip(halves, ms)]
            o_exts = [jnp.dot(v_ext_ts[h // ATT_GROUP], merge(p), preferred_element_type=F32)
                      for h, p in zip(batch, ps)]
            out_t = [o_ext[0:HEAD_DIM] / (o_ext[HEAD_DIM:HEAD_DIM + 1]
                                          + jnp.exp2(sk - jnp.concatenate(mh, axis=1)))
                     for o_ext, sk, mh in zip(o_exts, sinks, ms)]
            for j in range(0, n_at_once, 2):
                o_ref[0, :, (h0 + j) * HEAD_DIM:(h0 + j + 2) * HEAD_DIM] = (
                    jnp.concatenate(out_t[j:j + 2], axis=0).T.astype(BF16))

    @pl.when(i < n_ctx_tiles)
    def _context_queries():
        heads(kv_ref[0, 0:n_ctx, :],
              lambda ref: [ref[0:n_ctx, a * LANE:(a + 1) * LANE] for a in range(R // LANE)],
              lambda ps: jnp.concatenate(ps, axis=1))

    @pl.when(i >= n_ctx_tiles)
    def _latent_queries():
        first_blk = i * (R // W) - 1
        parts = []
        for blk in range(n_win):
            idx = jnp.clip(first_blk + blk, 0, n_blk - 1)
            parts.append(kv_ref[0, pl.ds(pl.multiple_of(idx * W, W), W), :])
        parts.append(kv_ref[0, 0:n_ctx, :])
        n_loc = n_win * W
        kj = lax.broadcasted_iota(jnp.int32, (W, LANE), 0)
        qi = lax.broadcasted_iota(jnp.int32, (W, LANE), 1)
        below = jnp.where(kj >= qi, 0.0, NEG_BIG).astype(F32)
        above = jnp.where(kj <= qi, 0.0, NEG_BIG).astype(F32)
        below_first = below + jnp.where(i == n_ctx_tiles, NEG_BIG, 0.0)
        above_last = above + jnp.where(i == n_blk // (R // W) - 1, NEG_BIG, 0.0)

        def split(ref):
            def blk(b, a):
                return ref[b * W:(b + 1) * W, a * LANE:(a + 1) * LANE]
            return [jnp.concatenate([blk(0, 0) + below_first, blk(1, 0), blk(2, 0) + above,
                                     ref[n_loc:n_loc + n_ctx, 0:LANE]], axis=0),
                    jnp.concatenate([blk(1, 1) + below, blk(2, 1), blk(3, 1) + above_last,
                                     ref[n_loc:n_loc + n_ctx, LANE:2 * LANE]], axis=0)]

        def merge(ps):
            zero = jnp.zeros((W, LANE), BF16)
            live = (n_win - 1) * W
            return jnp.concatenate(
                [jnp.concatenate([ps[0][0:live], zero, ps[0][live:]], axis=0),
                 jnp.concatenate([zero, ps[1]], axis=0)], axis=1)

        heads(jnp.concatenate(parts, axis=0), split, merge)


def _attention(q, kv, sink, n_ctx, tile_off, layer):
    B, T, _ = q.shape
    R = ROW_TILE
    nt = T // R
    kern = functools.partial(_attn_kernel, layer=layer, tile_off=tile_off, n_ctx_tiles=n_ctx // R,
                             n_ctx=n_ctx, n_lat=T - n_ctx)
    return pl.pallas_call(
        kern,
        grid=(B, nt - tile_off),
        in_specs=[
            pl.BlockSpec(memory_space=pltpu.SMEM),
            pl.BlockSpec((1, R, ATT_DIM), lambda b, i: (b, i + tile_off, 0)),
            pl.BlockSpec((1, T, 2 * KV_DIM), lambda b, i: (b, 0, 0)),
        ],
        out_specs=pl.BlockSpec((1, R, ATT_DIM), lambda b, i: (b, i + tile_off, 0)),
        out_shape=jax.ShapeDtypeStruct((B, T, ATT_DIM), BF16),
        scratch_shapes=[pltpu.VMEM((2 * ATT_GROUP, R + 2 * WINDOW + n_ctx, R), F32)],
        compiler_params=_params(48, 2),
        name="attention",
    )(sink, q, kv)


def _gla_kernel(qkv_ref, la_ref, gn_ref, o_ref, of_s, ob_s, stf, stb, *, n_ctx):
    CH = GLA_CHUNK
    R = ROW_TILE
    NC = R // CH
    T = of_s.shape[0]
    n_tiles = T // R
    n_ctx_tiles = n_ctx // R
    QK = GLA_QK_DIM
    DV = GLA_V_DIM
    H = GLA_HEADS

    stf[...] = jnp.zeros(stf.shape, F32)
    stb[...] = jnp.zeros(stb.shape, F32)

    same_chunk = _iota_div((R, R), 0, CH) == _iota_div((R, R), 1, CH)
    ri = lax.broadcasted_iota(jnp.int32, (R, R), 0)
    ci = lax.broadcasted_iota(jnp.int32, (R, R), 1)
    tri_lo = (same_chunk & (ci <= ri)).astype(BF16)
    tri_up = (same_chunk & (ci >= ri)).astype(BF16)
    rs = jnp.bitwise_and(lax.broadcasted_iota(jnp.int32, (H * R, R), 0), R - 1)
    cs = lax.broadcasted_iota(jnp.int32, (H * R, R), 1)
    same_chunk_h = (lax.shift_right_logical(rs, CH.bit_length() - 1)
                    == lax.shift_right_logical(cs, CH.bit_length() - 1))
    keep_lo = same_chunk_h & (cs <= rs)
    keep_up = same_chunk_h & (cs >= rs)
    qk_head = _iota_div((H * R, QK), 0, R) == _iota_div((H * R, QK), 1, GLA_DK)
    chunk_of_row = _iota_div((R, QK), 0, CH)
    st_keep = _iota_div((DV, QK), 0, GLA_DV) == _iota_div((DV, QK), 1, GLA_DK)
    nt_dims = (((1,), (1,)), ((), ()))

    def split3(a):
        a1 = a.astype(BF16)
        r1 = a - a1.astype(F32)
        a2 = r1.astype(BF16)
        a3 = (r1 - a2.astype(F32)).astype(BF16)
        return jnp.concatenate([a1, a2, a3], axis=1)

    def rows_of_chunks(cum, offset):
        picks = [cum[c * CH + offset:c * CH + offset + 1, :] for c in range(NC)]
        full = jnp.concatenate([jnp.broadcast_to(p, (CH, QK)) for p in picks], axis=0)
        return picks, full

    def load(t, col0, tri):
        r0 = t * R if isinstance(t, int) else pl.multiple_of(t * R, R)
        a = la_ref[0, pl.ds(r0, R), col0:col0 + QK]
        c3 = jnp.dot(tri, split3(a), preferred_element_type=F32)
        return dict(r0=r0, cum=c3[:, 0:QK] + c3[:, QK:2 * QK] + c3[:, 2 * QK:3 * QK],
                    q=qkv_ref[0, pl.ds(r0, R), 0:QK].astype(F32),
                    k=qkv_ref[0, pl.ds(r0, R), QK:2 * QK].astype(F32),
                    v=qkv_ref[0, pl.ds(r0, R), 2 * QK:])

    def scale(d, last):
        cum, q, k = d["cum"], d["q"], d["k"]
        tots, tot_b = rows_of_chunks(cum, last)
        _, mid_b = rows_of_chunks(cum, CH // 2)
        d["dec"] = [jnp.exp(t) for t in tots]
        d["qe"] = (q * jnp.exp(cum)).astype(BF16)
        d["qm"] = (q * jnp.exp(cum - mid_b)).astype(BF16)
        km = k * jnp.exp(mid_b - cum)
        d["ks"] = jnp.where(qk_head, jnp.concatenate([km] * H, axis=0), 0.0).astype(BF16)
        kl = (k * jnp.exp(tot_b - cum)).astype(BF16)
        d["kl_blocks"] = jnp.concatenate(
            [jnp.where(chunk_of_row == c, kl, jnp.zeros_like(kl)) for c in range(NC)], axis=1)
        d["v_t"] = d["v"].astype(F32).T.astype(BF16)

    def scores(d, keep_t):
        att_t = lax.dot_general(d["ks"], d["qm"], nt_dims, preferred_element_type=F32)
        d["att_t"] = jnp.where(keep_t, att_t, 0.0).astype(BF16)
        kv = jnp.dot(d["v_t"], d["kl_blocks"], preferred_element_type=F32)
        d["kv"] = [kv[:, c * QK:(c + 1) * QK] for c in range(NC)]

    def intra(d):
        o_t = jnp.concatenate(
            [jnp.dot(d["v_t"][hd * GLA_DV:(hd + 1) * GLA_DV], d["att_t"][hd * R:(hd + 1) * R],
                     preferred_element_type=F32) for hd in range(H)], axis=0)
        d["o"] = o_t.T

    def inter(tiles, order, st_ref, o_scr):
        state = st_ref[...]
        for d in tiles:
            parts = [None] * NC
            for c in order:
                parts[c] = lax.dot_general(d["qe"][c * CH:(c + 1) * CH], state.astype(BF16), nt_dims,
                                           preferred_element_type=F32)
                state = state * d["dec"][c] + jnp.where(st_keep, d["kv"][c], 0.0)
            o_scr[pl.ds(d["r0"], R), :] = d["o"] + jnp.concatenate(parts, axis=0)
        st_ref[...] = state

    def advance(f_tiles, b_tiles):
        fs = [load(t, 0, tri_lo) for t in f_tiles]
        bs = [load(t, QK, tri_up) for t in b_tiles]
        for d in fs:
            scale(d, CH - 1)
        for d in bs:
            scale(d, 0)
        for d in fs:
            scores(d, keep_up)
        for d in bs:
            scores(d, keep_lo)
        for d in fs + bs:
            intra(d)
        inter(fs, range(NC), stf, of_s)
        inter(bs, range(NC - 1, -1, -1), stb, ob_s)

    for s in range(n_ctx_tiles):
        advance([s], [n_ctx_tiles - 1 - s])
    n_lat_tiles = n_tiles - n_ctx_tiles
    per_step = GLA_TILES_PER_STEP if n_lat_tiles % GLA_TILES_PER_STEP == 0 else 1

    def step(s, carry):
        first = n_ctx_tiles + s * per_step
        last = n_tiles - 1 - s * per_step
        advance([first + j for j in range(per_step)], [last - j for j in range(per_step)])
        return carry

    lax.fori_loop(0, n_lat_tiles // per_step, step, 0)

    gmean = (_iota_div((DV, DV), 0, GLA_DV) == _iota_div((DV, DV), 1, GLA_DV)).astype(BF16) * (1.0 / GLA_DV)
    gn = gn_ref[...]

    def finish(i, carry):
        r0 = pl.multiple_of(i * R, R)
        o = of_s[pl.ds(r0, R), :] + ob_s[pl.ds(r0, R), :]
        oo = o * o
        hi = oo.astype(BF16)
        lo = (oo - hi.astype(F32)).astype(BF16)
        ms = jnp.dot(hi, gmean, preferred_element_type=F32) + jnp.dot(lo, gmean, preferred_element_type=F32)
        o_ref[0, pl.ds(r0, R), :] = (o * lax.rsqrt(ms + EPS) * gn).astype(BF16)
        return carry

    lax.fori_loop(0, n_tiles, finish, 0)


def _gla(gqkv, la, gla_norm, n_ctx, layer):
    B, T, _ = gqkv.shape
    kern = functools.partial(_gla_kernel, n_ctx=n_ctx)
    return pl.pallas_call(
        kern,
        grid=(B,),
        in_specs=[
            pl.BlockSpec((1, T, 2 * GLA_QK_DIM + GLA_V_DIM), lambda b: (b, 0, 0)),
            pl.BlockSpec((1, T, 2 * GLA_QK_DIM), lambda b: (b, 0, 0)),
            _layer_spec((1, GLA_V_DIM), layer),
        ],
        out_specs=pl.BlockSpec((1, T, GLA_V_DIM), lambda b: (b, 0, 0)),
        out_shape=jax.ShapeDtypeStruct((B, T, GLA_V_DIM), BF16),
        scratch_shapes=[pltpu.VMEM((T, GLA_V_DIM), F32), pltpu.VMEM((T, GLA_V_DIM), F32),
                        pltpu.VMEM((GLA_V_DIM, GLA_QK_DIM), F32),
                        pltpu.VMEM((GLA_V_DIM, GLA_QK_DIM), F32)],
        compiler_params=_params(48, 1),
        name="gla",
    )(gqkv, la, gla_norm)


N_MERGE_TILED = 5


def _pool_tile(u, before, after, pos0, seq_len):
    R = u.shape[0]
    ext = jnp.concatenate([before, u, after], axis=0)
    n = ext.shape[0]

    def shifted(a, k):
        return pltpu.roll(a, (-k) % n, 0)

    w2 = ext + shifted(ext, -1)
    w4 = shifted(w2, 1) + shifted(w2, -1)
    w8 = shifted(w4, 2) + shifted(w4, -2)
    w16 = shifted(w8, 4) + shifted(w8, -4)
    sums = [a[SUBLANE:SUBLANE + R] for a in (w2, w4, w8, w16)]
    lane_group = _iota_div((R, POOL_DIM), 1, POOL_GROUP_DIM)
    pos = lax.broadcasted_iota(jnp.int32, (R, POOL_DIM), 0) + pos0
    tot = sums[0]
    lo = jnp.full((R, POOL_DIM), POOL_WINDOWS[0] // 2, jnp.int32)
    hi = jnp.full((R, POOL_DIM), POOL_WINDOWS[0] - POOL_WINDOWS[0] // 2 - 1, jnp.int32)
    for g in range(1, len(POOL_WINDOWS)):
        w = POOL_WINDOWS[g]
        tot = jnp.where(lane_group == g, sums[g], tot)
        lo = jnp.where(lane_group == g, w // 2, lo)
        hi = jnp.where(lane_group == g, w - w // 2 - 1, hi)
    cnt = (jnp.minimum(pos + hi + 1, seq_len) - jnp.maximum(pos - lo, 0)).astype(F32)
    return tot / cnt - u


def _merge_kernel(*refs, n_x, n_ctx_tiles, tile_off, kept, n_ctx, n_lat):
    D = D_MODEL
    R = ROW_TILE
    x_refs, refs = refs[:n_x], refs[n_x:]
    n_tiled = PAIR * N_MERGE_TILED
    tiled = refs[:n_tiled]
    halos = refs[n_tiled:n_tiled + 2 * PAIR]
    mod_refs = refs[n_tiled + 2 * PAIR:n_tiled + 3 * PAIR]
    (gpost_ref, wba_ref, pbd_ref, ps_ref, wbp_ref, wbg_ref, wmg0_ref, wmg1_ref, wmg2_ref, wo_ref,
     o_ref) = refs[n_tiled + 3 * PAIR:]
    wmg_refs = (wmg0_ref, wmg1_ref, wmg2_ref)
    tiles_per_seq = n_ctx_tiles + n_lat // R

    for t in range(PAIR):
        att_ref, pu_ref, on_ref, sgr_ref, h_ref = tiled[t::PAIR]
        h = h_ref[...]

        def gate(j):
            return _sigmoid(jnp.dot(h, wmg_refs[j][...], preferred_element_type=F32))

        m = gate(0) * jnp.dot(att_ref[...], wba_ref[...], preferred_element_type=F32)

        i = lax.rem(pl.program_id(0) * PAIR + t, kept) + tile_off
        is_ctx = i < n_ctx_tiles
        seq_start = (i == 0) | (i == n_ctx_tiles)
        seq_end = (i == n_ctx_tiles - 1) | (i == tiles_per_seq - 1)
        before = jnp.where(seq_start, 0.0, halos[2 * t][...])
        after = jnp.where(seq_end, 0.0, halos[2 * t + 1][...])
        d_pool = _pool_tile(pu_ref[...], before, after, jnp.where(is_ctx, i, i - n_ctx_tiles) * R,
                            jnp.where(is_ctx, n_ctx, n_lat))
        yp = jnp.dot(d_pool.astype(BF16), pbd_ref[...], preferred_element_type=F32) * ps_ref[...]
        m = m + gate(1) * jnp.dot(yp.astype(BF16), wbp_ref[...], preferred_element_type=F32)
        m = m + gate(2) * jnp.dot(on_ref[...] * sgr_ref[...], wbg_ref[...], preferred_element_type=F32)
        y = jnp.dot(m.astype(BF16), wo_ref[...], preferred_element_type=F32)
        ms = jnp.mean(y * y, axis=-1, keepdims=True)
        o_ref[t * R:(t + 1) * R, :] = (
            _stream_tile(x_refs, t, tiles_per_seq, n_ctx_tiles)
            + mod_refs[t][0][:, 2 * D:3 * D] * (y * lax.rsqrt(ms + EPS) * gpost_ref[...]))


def _merge(x_src, y_att, pu, on, sgr, h, mods, g_post, wba, pbd, ps, wbp, wbg, wmg, wo,
           n_ctx_tiles, tile_off, layer):
    B, T, _ = y_att.shape
    D = D_MODEL
    R = ROW_TILE
    H = SUBLANE
    nt = T // R
    kept = nt - tile_off
    assert (B * kept) % PAIR == 0
    ctx_row = mods.shape[1] - 1
    rpb = R // H
    last_blk = B * T // H - 1

    def halo_spec(t, after):
        def index(s):
            b, i = coords(s, t)
            tile = b * nt + i
            return ((jnp.minimum((tile + 1) * rpb, last_blk) if after else jnp.maximum(tile * rpb - 1, 0)), 0)
        return pl.BlockSpec((H, POOL_DIM), index)

    def coords(s, t):
        tile = s * PAIR + t
        return lax.div(tile, kept), lax.rem(tile, kept) + tile_off

    def tile_specs(width):
        def spec(t):
            def index(s):
                b, i = coords(s, t)
                return (b * nt + i, 0)
            return pl.BlockSpec((R, width), index)
        return [spec(t) for t in range(PAIR)]

    def mod_spec(t):
        def index(s):
            b, i = coords(s, t)
            return (layer, jnp.where(i < n_ctx_tiles, ctx_row, b), 0, 0)
        return pl.BlockSpec((None, 1, 1, 6 * D), index)

    tiled = (y_att, pu, on, sgr, h)
    assert len(tiled) == N_MERGE_TILED
    flat = [a.reshape(B * T, a.shape[-1]) for a in tiled]
    pu_flat = flat[1]
    if isinstance(x_src, tuple):
        assert tile_off == 0
        x_arrays, x_specs = _stream_specs(x_src, coords, nt, n_ctx_tiles)
    else:
        x_arrays, x_specs = [x_src.reshape(B * T, D)] * PAIR, tile_specs(D)
    kern = functools.partial(_merge_kernel, n_x=len(x_arrays), n_ctx_tiles=n_ctx_tiles, tile_off=tile_off,
                             kept=kept, n_ctx=n_ctx_tiles * R, n_lat=T - n_ctx_tiles * R)
    out = pl.pallas_call(
        kern,
        grid=(B * kept // PAIR,),
        in_specs=[
            *x_specs,
            *(spec for a in flat for spec in tile_specs(a.shape[-1])),
            *(halo_spec(t, after) for t in range(PAIR) for after in (False, True)),
            *(mod_spec(t) for t in range(PAIR)),
            _layer_spec((1, D), layer),
            _layer_spec((ATT_DIM, D), layer), _layer_spec((POOL_DIM, POOL_DIM), layer),
            _layer_spec((1, POOL_DIM), layer), _layer_spec((POOL_DIM, D), layer),
            _layer_spec((GLA_V_DIM, D), layer),
            *(pl.BlockSpec((None, D, D), lambda s, j=j: (layer, 0, j),
                           pipeline_mode=pl.Buffered(1)) for j in range(3)),
            _layer_spec((D, D), layer),
        ],
        out_specs=pl.BlockSpec((PAIR * R, D), lambda s: (s, 0)),
        out_shape=jax.ShapeDtypeStruct((B * kept * R, D), F32),
        compiler_params=_params(48, 1),
        name="merge",
    )(*x_arrays, *(a for a in flat for _ in range(PAIR)), *([pu_flat] * (2 * PAIR)), *([mods] * PAIR),
      g_post, wba, pbd, ps, wbp, wbg, wmg, wmg, wmg, wo)
    return out.reshape(B, kept * R, D)


def _ffn_kernel(x_ref, xpa_ref, xna_ref, xpb_ref, xnb_ref, moda_ref, modb_ref, gpre_ref, wup_ref, cw_ref,
                cb_ref, wdn_ref, gpost_ref, o_ref, he_s, u_s, act_s, acc_s, *, n_ctx_tiles, tiles_per_seq):
    D = D_MODEL
    R = ROW_TILE
    S = SUBLANE
    G = R // S
    CK = FF_CHUNK
    n_chunks = D_FF // CK
    halo_refs = ((xpa_ref, xna_ref), (xpb_ref, xnb_ref))
    mods = (moda_ref[0], modb_ref[0])
    r_idx = lax.broadcasted_iota(jnp.int32, (R, R), 0)
    c_idx = lax.broadcasted_iota(jnp.int32, (R, R), 1)

    def seq_of(p):
        return jnp.bitwise_and(p, S - 1) * G + lax.shift_right_logical(p, SUBLANE_LOG2)

    to_perm = (c_idx == seq_of(r_idx)).astype(BF16)
    to_seq = (r_idx == seq_of(c_idx)).astype(BF16)
    sub = lax.broadcasted_iota(jnp.int32, (S, 2 * CK), 0)
    halo_row = lax.broadcasted_iota(jnp.int32, (S, 1), 0)

    def chunk_cols(ref, j):
        return jnp.concatenate([ref[:, j * CK:(j + 1) * CK], ref[:, D_FF + j * CK:D_FF + (j + 1) * CK]],
                               axis=1)

    def prologue(t):
        i = lax.rem(pl.program_id(0) * PAIR + t, tiles_per_seq)
        scale, shift = mods[t][:, 4 * D:5 * D], mods[t][:, 3 * D:4 * D]
        h = _modulated_norm(x_ref[t * R:(t + 1) * R, :], gpre_ref[...], scale, shift).astype(BF16)
        he_s[t, 0:R, :] = jnp.dot(to_perm, h, preferred_element_type=F32).astype(BF16)
        xp_ref, xn_ref = halo_refs[t]
        halo = jnp.concatenate([xp_ref[S - 1:S, :], xn_ref[0:1, :], jnp.zeros((S - 2, D), F32)], axis=0)
        seq_start = (i == 0) | (i == n_ctx_tiles)
        seq_end = (i == n_ctx_tiles - 1) | (i == tiles_per_seq - 1)
        outside = ((halo_row == 0) & seq_start) | ((halo_row == 1) & seq_end) | (halo_row >= 2)
        he_s[t, R:R + S, :] = jnp.where(
            outside, 0.0, _modulated_norm(halo, gpre_ref[...], scale, shift)).astype(BF16)

    def up(t, j):
        he = he_s[t]
        u_s[t, j % 2, :, 0:CK] = jnp.dot(he, wup_ref[:, j * CK:(j + 1) * CK], preferred_element_type=F32)
        u_s[t, j % 2, :, CK:] = jnp.dot(he, wup_ref[:, D_FF + j * CK:D_FF + (j + 1) * CK],
                                        preferred_element_type=F32)

    def finish(t, j):
        u = u_s[t, j % 2, 0:R, :]
        edge = u_s[t, j % 2, R:R + S, :]
        prev0 = jnp.where(sub == 0, edge[0:1], pltpu.roll(u[R - S:R], 1, 0))
        next_last = jnp.where(sub == S - 1, edge[1:2], pltpu.roll(u[0:S], S - 1, 0))
        prev = jnp.concatenate([prev0, u[0:R - S]], axis=0)
        nxt = jnp.concatenate([u[S:R], next_last], axis=0)
        cw = chunk_cols(cw_ref, j)
        c = prev * cw[0:1] + u * cw[1:2] + nxt * cw[2:3] + chunk_cols(cb_ref, j)
        g = c[:, CK:]
        act_s[t, :, j * CK:(j + 1) * CK] = (c[:, :CK] * (g * _sigmoid(g))).astype(BF16)

    def down(t, j0, j1):
        y = jnp.dot(act_s[t, :, j0 * CK:j1 * CK], wdn_ref[j0 * CK:j1 * CK, :], preferred_element_type=F32)
        if j0 == 0:
            acc_s[t] = y
        else:
            acc_s[t] += y

    def epilogue(t):
        y = jnp.dot(to_seq, acc_s[t].astype(BF16), preferred_element_type=F32)
        ms = jnp.mean(y * y, axis=-1, keepdims=True)
        o_ref[t * R:(t + 1) * R, :] = (x_ref[t * R:(t + 1) * R, :]
                                       + mods[t][:, 5 * D:6 * D] * (y * lax.rsqrt(ms + EPS) * gpost_ref[...]))

    def body(t, after_first_up):
        up(t, 0)
        after_first_up()
        group_start = 0
        for j in range(n_chunks):
            if j + 1 < n_chunks:
                up(t, j + 1)
            finish(t, j)
            if j + 1 - group_start == FF_GROUP or j + 1 == n_chunks:
                down(t, group_start, j + 1)
                group_start = j + 1

    prologue(0)
    body(0, lambda: prologue(1))
    body(1, lambda: epilogue(0))
    epilogue(1)


def _ffn(xa, mods, g_pre, w_up, conv_w, conv_b, w_down, g_post, n_ctx_tiles, layer):
    B, T, D = xa.shape
    R = ROW_TILE
    H = SUBLANE
    nt = T // R
    assert (B * nt) % PAIR == 0 and PAIR == 2
    ctx_row = mods.shape[1] - 1
    rpb = R // H
    RE = R + H
    last_blk = B * T // H - 1

    def halo(t, after):
        def index(s):
            tile = s * PAIR + t
            return ((jnp.minimum((tile + 1) * rpb, last_blk) if after else jnp.maximum(tile * rpb - 1, 0)), 0)
        return pl.BlockSpec((H, D), index)

    def mod_spec(t):
        def index(s):
            tile = s * PAIR + t
            return (layer, jnp.where(lax.rem(tile, nt) < n_ctx_tiles, ctx_row, lax.div(tile, nt)), 0, 0)
        return pl.BlockSpec((None, 1, 1, 6 * D), index)

    x2 = xa.reshape(B * T, D)
    kern = functools.partial(_ffn_kernel, n_ctx_tiles=n_ctx_tiles, tiles_per_seq=nt)
    out = pl.pallas_call(
        kern,
        grid=(B * nt // PAIR,),
        in_specs=[
            pl.BlockSpec((PAIR * R, D), lambda s: (s, 0)),
            halo(0, False), halo(0, True), halo(1, False), halo(1, True),
            mod_spec(0), mod_spec(1),
            _layer_spec((1, D), layer),
            _layer_spec((D, 2 * D_FF), layer), _layer_spec((3, 2 * D_FF), layer),
            _layer_spec((1, 2 * D_FF), layer), _layer_spec((D_FF, D), layer),
            _layer_spec((1, D), layer),
        ],
        out_specs=pl.BlockSpec((PAIR * R, D), lambda s: (s, 0)),
        out_shape=jax.ShapeDtypeStruct((B * T, D), F32),
        scratch_shapes=[pltpu.VMEM((PAIR, RE, D), BF16),
                        pltpu.VMEM((PAIR, 2, RE, 2 * FF_CHUNK), F32),
                        pltpu.VMEM((PAIR, R, D_FF), BF16),
                        pltpu.VMEM((PAIR, R, D), F32)],
        compiler_params=_params(56, 1),
        name="ffn",
    )(x2, x2, x2, x2, x2, mods, mods, g_pre, w_up, conv_w, conv_b, w_down, g_post)
    return out.reshape(B, T, D)


def _rope_table(n_ctx, n_lat):
    rows = n_lat // GRID_W
    row = jnp.repeat(jnp.arange(rows, dtype=F32), GRID_W)
    col = jnp.tile(jnp.arange(GRID_W, dtype=F32), rows)
    n = HEAD_DIM // 4
    inv = ROPE_BASE ** (-jnp.arange(n, dtype=F32) / n)
    ang = jnp.concatenate([row[:, None] * inv, col[:, None] * inv], axis=-1)
    cos, sin = jnp.cos(ang), jnp.sin(ang)
    zero = jnp.zeros_like(sin)
    reps = LANE // HEAD_DIM
    cos_t = jnp.tile(jnp.concatenate([cos, cos], axis=-1), (1, reps))
    sin_lo = jnp.tile(jnp.concatenate([-sin, zero], axis=-1), (1, reps))
    sin_hi = jnp.tile(jnp.concatenate([zero, sin], axis=-1), (1, reps))
    lat = jnp.concatenate([cos_t, sin_lo, sin_hi], axis=-1)
    ctx = jnp.concatenate([jnp.ones((n_ctx, LANE), F32), jnp.zeros((n_ctx, 2 * LANE), F32)], axis=-1)
    return jnp.concatenate([ctx, lat], axis=0)


def kernel(x, c, ctx, c_ctx, w_ada, b_ada, g_pre_mix, g_post_mix, g_pre_ffn, g_post_ffn, w_in, att_sink, pool_w, pool_scale, gla_wa2, gla_ba, gla_norm, w_br_att, w_br_pool, w_br_gla, w_o, w_up, conv_w, conv_b, w_down):
    B, L, D = x.shape
    C = ctx.shape[1]
    depth = w_in.shape[0]
    R = ROW_TILE
    assert D == D_MODEL and C % R == 0 and L % R == 0 and L % GRID_W == 0
    n_ctx_tiles = C // R

    mod_rows = -(-(B + 1) // SUBLANE) * SUBLANE
    cc = jnp.zeros((mod_rows, D), F32).at[:B].set(c).at[mod_rows - 1].set(c_ctx)
    mods = _ada_table(cc, w_ada, b_ada).reshape(depth, mod_rows, 1, 6 * D)

    rope_tab = _rope_table(C, L)
    w_in_p, w_mg = _split_w_in(w_in)
    wa = jnp.zeros((depth, LANE, 2 * GLA_QK_DIM), F32)
    wa = wa.at[:, 0:GLA_GATE_RANK, 0:GLA_QK_DIM].set(gla_wa2[:, 0])
    wa = wa.at[:, GLA_GATE_RANK:2 * GLA_GATE_RANK, GLA_QK_DIM:].set(gla_wa2[:, 1])
    wa_hi = wa.astype(BF16)
    wa = jnp.concatenate([wa_hi, wa_hi, (wa - wa_hi.astype(F32)).astype(BF16)], axis=1)
    ba = gla_ba.reshape(depth, 1, 2 * GLA_QK_DIM)
    pbd = jnp.zeros((depth, POOL_DIM, POOL_DIM), F32)
    for g in range(len(POOL_WINDOWS)):
        sl = slice(g * POOL_GROUP_DIM, (g + 1) * POOL_GROUP_DIM)
        pbd = pbd.at[:, sl, sl].set(pool_w[:, g])
    pbd = pbd.astype(BF16)
    wba, wbp, wbg, wo = (w.astype(BF16) for w in (w_br_att, w_br_pool, w_br_gla, w_o))
    wup, wdn = w_up.astype(BF16), w_down.astype(BF16)

    def rows(t):
        return t.reshape(depth, 1, -1)

    g_pre_mix, g_post_mix, g_pre_ffn, g_post_ffn, pool_scale, gla_norm, conv_b = (
        rows(t) for t in (g_pre_mix, g_post_mix, g_pre_ffn, g_post_ffn, pool_scale, gla_norm, conv_b))

    xa = (ctx, x) if depth > 1 else jnp.concatenate([ctx, x], axis=1)
    for l in range(depth):
        last = l == depth - 1
        off = n_ctx_tiles if last else 0
        q, kv, pu, gqkv, sgr, la, h = _in_proj(xa, mods, g_pre_mix, w_in_p, rope_tab, wa, ba, n_ctx_tiles, l)
        y_att = _attention(q, kv, att_sink, C, off, l)
        on = _gla(gqkv, la, gla_norm, C, l)
        xa = _merge(xa, y_att, pu, on, sgr, h, mods, g_post_mix, wba, pbd, pool_scale, wbp, wbg, w_mg, wo,
                    n_ctx_tiles, off, l)
        xa = _ffn(xa, mods, g_pre_ffn, wup, conv_w, conv_b, wdn, g_post_ffn, n_ctx_tiles - off, l)
    return xa
```

```python
import functools

import jax
import jax.numpy as jnp
import numpy as np
from jax import lax
from jax.experimental import pallas as pl
from jax.experimental.pallas import tpu as pltpu

F32 = jnp.float32
BF16 = jnp.bfloat16
HIGHEST = lax.Precision.HIGHEST

D_MODEL = 1024
GRID_W = 64
EPS = 1e-6

HEAD_DIM = 64
ATT_HEADS = 8
ATT_KV_HEADS = 2
ATT_GROUP = ATT_HEADS // ATT_KV_HEADS
WINDOW = 128
ROPE_BASE = 10000.0
ATT_DIM = ATT_HEADS * HEAD_DIM
KV_DIM = ATT_KV_HEADS * HEAD_DIM

POOL_WINDOWS = (2, 4, 8, 16)
POOL_GROUP_DIM = 64
POOL_DIM = len(POOL_WINDOWS) * POOL_GROUP_DIM

GLA_HEADS = 4
GLA_DK = 32
GLA_DV = 64
GLA_GATE_RANK = 16
GLA_TAU = 16.0
N_BRANCH = 3
GLA_QK_DIM = GLA_HEADS * GLA_DK
GLA_V_DIM = GLA_HEADS * GLA_DV
GLA_CHUNK = 64
GLA_TILES_PER_STEP = 4

D_FF = 2816
FF_CHUNK = 256
FFN_TILES = 2
FF_GROUP = 4
PAIR = 2

LANE = 128
SUBLANE = 8
SUBLANE_LOG2 = SUBLANE.bit_length() - 1
ROW_TILE = 256

_O_AQ = 0
_O_AK = _O_AQ + ATT_DIM
_O_AV = _O_AK + KV_DIM
_O_PU = _O_AV + KV_DIM
_O_GQ = _O_PU + POOL_DIM
_O_GK = _O_GQ + GLA_QK_DIM
_O_GV = _O_GK + GLA_QK_DIM
_O_GR = _O_GV + GLA_V_DIM
_O_GLR = _O_GR + GLA_V_DIM
IN_DIM_PAD = _O_GLR + LANE
_O_MG = _O_GLR + 2 * GLA_GATE_RANK
W_SPLIT_ROWS = 256

NEG_BIG = -1e30
LOG2E = float(np.log2(np.e))


def _params(vmem_mb, n_axes):
    return pltpu.CompilerParams(
        dimension_semantics=("arbitrary",) * n_axes,
        vmem_limit_bytes=vmem_mb * 1024 * 1024,
    )


def _layer_spec(shape, layer):
    nd = len(shape)
    return pl.BlockSpec((None,) + tuple(shape), lambda *_: (layer,) + (0,) * nd,
                        pipeline_mode=pl.Buffered(1))


def _sigmoid(v):
    return 1.0 / (1.0 + jnp.exp(-v))


def _iota_div(shape, axis, divisor):
    shift = int(np.log2(divisor))
    assert 1 << shift == divisor
    return lax.shift_right_logical(lax.broadcasted_iota(jnp.int32, shape, axis), shift)


def _modulated_norm(x, gain, scale, shift):
    ms = jnp.mean(x * x, axis=-1, keepdims=True)
    return (x * lax.rsqrt(ms + EPS) * gain) * (1.0 + scale) + shift


def _ada_kernel(c_ref, w_ref, b_ref, o_ref):
    c = c_ref[...]
    a = c * _sigmoid(c)
    o_ref[0] = jnp.dot(a, w_ref[0], precision=HIGHEST, preferred_element_type=F32) + b_ref[0]


def _ada_table(cc, w_ada, b_ada):
    depth = w_ada.shape[0]
    rows = cc.shape[0]
    width = 2 * D_MODEL
    n_col = w_ada.shape[2] // width
    return pl.pallas_call(
        _ada_kernel,
        grid=(depth, n_col),
        in_specs=[
            pl.BlockSpec((rows, D_MODEL), lambda l, j: (0, 0)),
            pl.BlockSpec((1, D_MODEL, width), lambda l, j: (l, 0, j)),
            pl.BlockSpec((1, 1, width), lambda l, j: (l, 0, j)),
        ],
        out_specs=pl.BlockSpec((1, rows, width), lambda l, j: (l, 0, j)),
        out_shape=jax.ShapeDtypeStruct((depth, rows, w_ada.shape[2]), F32),
        compiler_params=_params(32, 2),
        name="ada_table",
    )(cc, w_ada, b_ada.reshape(depth, 1, -1))


def _split_w_in_kernel(w_ref, wp_ref, wg_ref):
    wp_ref[0] = w_ref[0, :, 0:IN_DIM_PAD]
    wg_ref[0] = w_ref[0, :, _O_MG:_O_MG + N_BRANCH * D_MODEL]


def _split_w_in(w_in):
    depth, D, n_in = w_in.shape
    assert n_in == _O_MG + N_BRANCH * D and D % W_SPLIT_ROWS == 0
    rows = W_SPLIT_ROWS
    return pl.pallas_call(
        _split_w_in_kernel,
        grid=(depth, D // rows),
        in_specs=[pl.BlockSpec((1, rows, n_in), lambda l, r: (l, r, 0))],
        out_specs=[pl.BlockSpec((1, rows, IN_DIM_PAD), lambda l, r: (l, r, 0)),
                   pl.BlockSpec((1, rows, N_BRANCH * D), lambda l, r: (l, r, 0))],
        out_shape=[jax.ShapeDtypeStruct((depth, D, IN_DIM_PAD), BF16),
                   jax.ShapeDtypeStruct((depth, D, N_BRANCH * D), BF16)],
        compiler_params=_params(48, 2),
        name="split_w_in",
    )(w_in)


def _stream_tile(x_refs, t, tiles_per_seq, n_ctx_tiles):
    if len(x_refs) == 1:
        return x_refs[0][t * ROW_TILE:(t + 1) * ROW_TILE, :]
    if len(x_refs) == PAIR:
        return x_refs[t][...]
    i = lax.rem(pl.program_id(0) * PAIR + t, tiles_per_seq)
    return jnp.where(i < n_ctx_tiles, x_refs[t][...], x_refs[PAIR + t][...])


def _stream_specs(x_src, tile_coords, tiles_per_seq, n_ctx_tiles):
    R = ROW_TILE
    if not isinstance(x_src, tuple):
        B, T, D = x_src.shape
        return [x_src.reshape(B * T, D)], [pl.BlockSpec((PAIR * R, D), lambda s: (s, 0))]
    ctx, lat = x_src
    B, _, D = ctx.shape
    n_lat_tiles = tiles_per_seq - n_ctx_tiles

    def spec(t, is_ctx):
        def index(s):
            b, i = tile_coords(s, t)
            if is_ctx:
                return (b * n_ctx_tiles + jnp.minimum(i, n_ctx_tiles - 1), 0)
            return (b * n_lat_tiles + jnp.maximum(i - n_ctx_tiles, 0), 0)
        return pl.BlockSpec((R, D), index)

    arrays = [ctx.reshape(-1, D)] * PAIR + [lat.reshape(-1, D)] * PAIR
    return arrays, [spec(t, True) for t in range(PAIR)] + [spec(t, False) for t in range(PAIR)]


def _in_proj_kernel(*refs, n_x, tiles_per_seq, n_ctx_tiles):
    x_refs = refs[:n_x]
    (moda_ref, modb_ref, g_ref, w_ref, ropea_ref, ropeb_ref, wa_ref, ba_ref,
     q_ref, kv_ref, pu_ref, gqkv_ref, sgr_ref, la_ref, h_ref) = refs[n_x:]
    D = D_MODEL
    R = ROW_TILE
    mods = (moda_ref[0], modb_ref[0])
    ropes = (ropea_ref, ropeb_ref)

    def norm(t):
        rows = pl.ds(t * R, R)
        h_ref[rows, :] = _modulated_norm(_stream_tile(x_refs, t, tiles_per_seq, n_ctx_tiles), g_ref[...],
                                         mods[t][:, D:2 * D], mods[t][:, 0:D]).astype(BF16)

    def project(t, after_first):
        rows = pl.ds(t * R, R)
        h = h_ref[rows, :]
        rope_ref = ropes[t]

        def proj(lo, width):
            return jnp.dot(h, w_ref[:, lo:lo + width], preferred_element_type=F32)

        cosf = rope_ref[:, 0:LANE]
        sin_lo = rope_ref[:, LANE:2 * LANE]
        sin_hi = rope_ref[:, 2 * LANE:3 * LANE]

        def rope(v):
            return (v * cosf + pltpu.roll(v, LANE - HEAD_DIM // 2, 1) * sin_lo
                    + pltpu.roll(v, HEAD_DIM // 2, 1) * sin_hi)

        aq = proj(_O_AQ, ATT_DIM)
        after_first()
        for j in range(ATT_DIM // LANE):
            q_ref[rows, j * LANE:(j + 1) * LANE] = (
                rope(aq[:, j * LANE:(j + 1) * LANE]) * (HEAD_DIM ** -0.5 * LOG2E)).astype(BF16)
        kv_ref[rows, 0:KV_DIM] = rope(proj(_O_AK, KV_DIM)).astype(BF16)
        kv_ref[rows, KV_DIM:2 * KV_DIM] = proj(_O_AV, KV_DIM).astype(BF16)
        pu_ref[rows, :] = proj(_O_PU, POOL_DIM)
        gqkv_ref[rows, 0:GLA_QK_DIM] = (proj(_O_GQ, GLA_QK_DIM) * (GLA_DK ** -0.5)).astype(BF16)
        gqkv_ref[rows, GLA_QK_DIM:] = proj(_O_GK, GLA_QK_DIM + GLA_V_DIM).astype(BF16)
        gr = proj(_O_GR, GLA_V_DIM)
        sgr_ref[rows, :] = (gr * _sigmoid(gr)).astype(BF16)

        glr = proj(_O_GLR, LANE)
        glr_hi = glr.astype(BF16)
        glr_lo = (glr - glr_hi.astype(F32)).astype(BF16)
        z = jnp.dot(jnp.concatenate([glr_hi, glr_lo, glr_hi], axis=1), wa_ref[...],
                    preferred_element_type=F32) + ba_ref[...]
        la_ref[rows, :] = (jnp.minimum(z, 0.0) - jnp.log(1.0 + jnp.exp(-jnp.abs(z)))) * (1.0 / GLA_TAU)

    norm(0)
    project(0, lambda: norm(1))
    project(1, lambda: None)


def _tile_of_pair(t, tiles_per_seq, n_ctx_tiles, ctx_row, layer):
    def seq_tile(s):
        return lax.rem(s * PAIR + t, tiles_per_seq)

    def mod_index(s):
        tile = s * PAIR + t
        return (layer, jnp.where(lax.rem(tile, tiles_per_seq) < n_ctx_tiles, ctx_row,
                                 lax.div(tile, tiles_per_seq)), 0, 0)

    return seq_tile, pl.BlockSpec((None, 1, 1, 6 * D_MODEL), mod_index)


def _in_proj(x_src, mods, g_pre, w_in, rope_tab, wa, ba, n_ctx_tiles, layer):
    if isinstance(x_src, tuple):
        B, D = x_src[0].shape[0], x_src[0].shape[2]
        T = x_src[0].shape[1] + x_src[1].shape[1]
    else:
        B, T, D = x_src.shape
    R = ROW_TILE
    nt = T // R
    assert (B * nt) % PAIR == 0
    ctx_row = mods.shape[1] - 1
    x_arrays, x_specs = _stream_specs(
        x_src, lambda s, t: (lax.div(s * PAIR + t, nt), lax.rem(s * PAIR + t, nt)), nt, n_ctx_tiles)

    def pair(width):
        return pl.BlockSpec((PAIR * R, width), lambda s: (s, 0))

    seq_tiles, mod_specs = zip(*(_tile_of_pair(t, nt, n_ctx_tiles, ctx_row, layer) for t in range(PAIR)))
    rope_specs = [pl.BlockSpec((R, 3 * LANE), lambda s, f=f: (f(s), 0)) for f in seq_tiles]
    widths = (ATT_DIM, 2 * KV_DIM, POOL_DIM, 2 * GLA_QK_DIM + GLA_V_DIM, GLA_V_DIM,
              2 * GLA_QK_DIM, D)
    dtypes = (BF16, BF16, F32, BF16, BF16, F32, BF16)
    kern = functools.partial(_in_proj_kernel, n_x=len(x_arrays), tiles_per_seq=nt, n_ctx_tiles=n_ctx_tiles)
    outs = pl.pallas_call(
        kern,
        grid=(B * nt // PAIR,),
        in_specs=[
            *x_specs, *mod_specs,
            _layer_spec((1, D), layer),
            _layer_spec((D, IN_DIM_PAD), layer),
            *rope_specs,
            _layer_spec((3 * LANE, 2 * GLA_QK_DIM), layer),
            _layer_spec((1, 2 * GLA_QK_DIM), layer),
        ],
        out_specs=[pair(w) for w in widths],
        out_shape=[jax.ShapeDtypeStruct((B * T, w), dt) for w, dt in zip(widths, dtypes)],
        compiler_params=_params(48, 1),
        name="in_proj",
    )(*x_arrays, mods, mods, g_pre, w_in, rope_tab, rope_tab, wa, ba)
    return [o.reshape(B, T, -1) for o in outs]


def _attn_kernel(sink_ref, q_ref, kv_ref, o_ref, s_s, *, layer, tile_off, n_ctx_tiles, n_ctx, n_lat):
    R = ROW_TILE
    W = WINDOW
    assert R == 2 * W == 2 * LANE
    n_win = R // W + 2
    i = pl.program_id(1) + tile_off
    n_blk = (n_ctx + n_lat) // W
    nt_dims = (((1,), (1,)), ((), ()))
    q = q_ref[0]

    def heads(kv, split, merge):
        n_keys = kv.shape[0]
        ks, v_ext_ts = [], []
        for g in range(ATT_KV_HEADS):
            ks.append(kv[:, g * HEAD_DIM:(g + 1) * HEAD_DIM])
            v = kv[:, KV_DIM + g * HEAD_DIM:KV_DIM + (g + 1) * HEAD_DIM].astype(F32)
            v_ext_ts.append(
                jnp.concatenate([v, jnp.ones((n_keys, LANE - HEAD_DIM), F32)], axis=1).T.astype(BF16))

        n_slots = s_s.shape[0]

        def scores(h):
            qh = q[:, h * HEAD_DIM:(h + 1) * HEAD_DIM]
            s_s[h % n_slots, 0:n_keys, :] = lax.dot_general(ks[h // ATT_GROUP], qh, nt_dims,
                                                             preferred_element_type=F32)

        n_at_once = n_slots // 2
        for h in range(n_at_once):
            scores(h)
        for h0 in range(0, ATT_HEADS, n_at_once):
            batch = tuple(range(h0, h0 + n_at_once))
            for h in batch:
                if h + n_at_once < ATT_HEADS:
                    scores(h + n_at_once)
            sinks = [sink_ref[layer, h] * LOG2E for h in batch]
            halves = [split(s_s.at[h % n_slots]) for h in batch]
            ms = [[jnp.maximum(jnp.max(s_half, axis=0, keepdims=True), sk) for s_half in hs]
                  for hs, sk in zip(halves, sinks)]
            ps = [[jnp.exp2(s_half - m).astype(BF16) for s_half, m in zip(hs, mh)]
                  for hs, mh in zip(halves, ms)]
            o_exts = [jnp.dot(v_ext_ts[h // ATT_GROUP], merge(p), preferred_element_type=F32)
                      for h, p in zip(batch, ps)]
            out_t = [o_ext[0:HEAD_DIM] / (o_ext[HEAD_DIM:HEAD_DIM + 1]
                                          + jnp.exp2(sk - jnp.concatenate(mh, axis=1)))
                     for o_ext, sk, mh in zip(o_exts, sinks, ms)]
            for j in range(0, n_at_once, 2):
                o_ref[0, :, (h0 + j) * HEAD_DIM:(h0 + j + 2) * HEAD_DIM] = (
                    jnp.concatenate(out_t[j:j + 2], axis=0).T.astype(BF16))

    @pl.when(i < n_ctx_tiles)
    def _context_queries():
        heads(kv_ref[0, 0:n_ctx, :],
              lambda ref: [ref[0:n_ctx, a * LANE:(a + 1) * LANE] for a in range(R // LANE)],
              lambda ps: jnp.concatenate(ps, axis=1))

    @pl.when(i >= n_ctx_tiles)
    def _latent_queries():
        first_blk = i * (R // W) - 1
        parts = []
        for blk in range(n_win):
            idx = jnp.clip(first_blk + blk, 0, n_blk - 1)
            parts.append(kv_ref[0, pl.ds(pl.multiple_of(idx * W, W), W), :])
        parts.append(kv_ref[0, 0:n_ctx, :])
        n_loc = n_win * W
        kj = lax.broadcasted_iota(jnp.int32, (W, LANE), 0)
        qi = lax.broadcasted_iota(jnp.int32, (W, LANE), 1)
        below = jnp.where(kj >= qi, 0.0, NEG_BIG).astype(F32)
        above = jnp.where(kj <= qi, 0.0, NEG_BIG).astype(F32)
        below_first = below + jnp.where(i == n_ctx_tiles, NEG_BIG, 0.0)
        above_last = above + jnp.where(i == n_blk // (R // W) - 1, NEG_BIG, 0.0)

        def split(ref):
            def blk(b, a):
                return ref[b * W:(b + 1) * W, a * LANE:(a + 1) * LANE]
            return [jnp.concatenate([blk(0, 0) + below_first, blk(1, 0), blk(2, 0) + above,
                                     ref[n_loc:n_loc + n_ctx, 0:LANE]], axis=0),
                    jnp.concatenate([blk(1, 1) + below, blk(2, 1), blk(3, 1) + above_last,
                                     ref[n_loc:n_loc + n_ctx, LANE:2 * LANE]], axis=0)]

        def merge(ps):
            zero = jnp.zeros((W, LANE), BF16)
            live = (n_win - 1) * W
            return jnp.concatenate(
                [jnp.concatenate([ps[0][0:live], zero, ps[0][live:]], axis=0),
                 jnp.concatenate([zero, ps[1]], axis=0)], axis=1)

        heads(jnp.concatenate(parts, axis=0), split, merge)


def _attention(q, kv, sink, n_ctx, tile_off, layer):
    B, T, _ = q.shape
    R = ROW_TILE
    nt = T // R
    kern = functools.partial(_attn_kernel, layer=layer, tile_off=tile_off, n_ctx_tiles=n_ctx // R,
                             n_ctx=n_ctx, n_lat=T - n_ctx)
    return pl.pallas_call(
        kern,
        grid=(B, nt - tile_off),
        in_specs=[
            pl.BlockSpec(memory_space=pltpu.SMEM),
            pl.BlockSpec((1, R, ATT_DIM), lambda b, i: (b, i + tile_off, 0)),
            pl.BlockSpec((1, T, 2 * KV_DIM), lambda b, i: (b, 0, 0)),
        ],
        out_specs=pl.BlockSpec((1, R, ATT_DIM), lambda b, i: (b, i + tile_off, 0)),
        out_shape=jax.ShapeDtypeStruct((B, T, ATT_DIM), BF16),
        scratch_shapes=[pltpu.VMEM((2 * ATT_GROUP, R + 2 * WINDOW + n_ctx, R), F32)],
        compiler_params=_params(48, 2),
        name="attention",
    )(sink, q, kv)


def _gla_kernel(qkv_ref, la_ref, gn_ref, o_ref, of_s, ob_s, stf, stb, *, n_ctx):
    CH = GLA_CHUNK
    R = ROW_TILE
    NC = R // CH
    T = of_s.shape[0]
    n_tiles = T // R
    n_ctx_tiles = n_ctx // R
    QK = GLA_QK_DIM
    DV = GLA_V_DIM
    H = GLA_HEADS

    stf[...] = jnp.zeros(stf.shape, F32)
    stb[...] = jnp.zeros(stb.shape, F32)

    same_chunk = _iota_div((R, R), 0, CH) == _iota_div((R, R), 1, CH)
    ri = lax.broadcasted_iota(jnp.int32, (R, R), 0)
    ci = lax.broadcasted_iota(jnp.int32, (R, R), 1)
    tri_lo = (same_chunk & (ci <= ri)).astype(BF16)
    tri_up = (same_chunk & (ci >= ri)).astype(BF16)
    rs = jnp.bitwise_and(lax.broadcasted_iota(jnp.int32, (H * R, R), 0), R - 1)
    cs = lax.broadcasted_iota(jnp.int32, (H * R, R), 1)
    same_chunk_h = (lax.shift_right_logical(rs, CH.bit_length() - 1)
                    == lax.shift_right_logical(cs, CH.bit_length() - 1))
    keep_lo = same_chunk_h & (cs <= rs)
    keep_up = same_chunk_h & (cs >= rs)
    qk_head = _iota_div((H * R, QK), 0, R) == _iota_div((H * R, QK), 1, GLA_DK)
    chunk_of_row = _iota_div((R, QK), 0, CH)
    st_keep = _iota_div((DV, QK), 0, GLA_DV) == _iota_div((DV, QK), 1, GLA_DK)
    nt_dims = (((1,), (1,)), ((), ()))

    def split3(a):
        a1 = a.astype(BF16)
        r1 = a - a1.astype(F32)
        a2 = r1.astype(BF16)
        a3 = (r1 - a2.astype(F32)).astype(BF16)
        return jnp.concatenate([a1, a2, a3], axis=1)

    def rows_of_chunks(cum, offset):
        picks = [cum[c * CH + offset:c * CH + offset + 1, :] for c in range(NC)]
        full = jnp.concatenate([jnp.broadcast_to(p, (CH, QK)) for p in picks], axis=0)
        return picks, full

    def load(t, col0, tri):
        r0 = t * R if isinstance(t, int) else pl.multiple_of(t * R, R)
        a = la_ref[0, pl.ds(r0, R), col0:col0 + QK]
        c3 = jnp.dot(tri, split3(a), preferred_element_type=F32)
        return dict(r0=r0, cum=c3[:, 0:QK] + c3[:, QK:2 * QK] + c3[:, 2 * QK:3 * QK],
                    q=qkv_ref[0, pl.ds(r0, R), 0:QK].astype(F32),
                    k=qkv_ref[0, pl.ds(r0, R), QK:2 * QK].astype(F32),
                    v=qkv_ref[0, pl.ds(r0, R), 2 * QK:])

    def scale(d, last):
        cum, q, k = d["cum"], d["q"], d["k"]
        tots, tot_b = rows_of_chunks(cum, last)
        _, mid_b = rows_of_chunks(cum, CH // 2)
        d["dec"] = [jnp.exp(t) for t in tots]
        d["qe"] = (q * jnp.exp(cum)).astype(BF16)
        d["qm"] = (q * jnp.exp(cum - mid_b)).astype(BF16)
        km = k * jnp.exp(mid_b - cum)
        d["ks"] = jnp.where(qk_head, jnp.concatenate([km] * H, axis=0), 0.0).astype(BF16)
        kl = (k * jnp.exp(tot_b - cum)).astype(BF16)
        d["kl_blocks"] = jnp.concatenate(
            [jnp.where(chunk_of_row == c, kl, jnp.zeros_like(kl)) for c in range(NC)], axis=1)
        d["v_t"] = d["v"].astype(F32).T.astype(BF16)

    def scores(d, keep_t):
        att_t = lax.dot_general(d["ks"], d["qm"], nt_dims, preferred_element_type=F32)
        d["att_t"] = jnp.where(keep_t, att_t, 0.0).astype(BF16)
        kv = jnp.dot(d["v_t"], d["kl_blocks"], preferred_element_type=F32)
        d["kv"] = [kv[:, c * QK:(c + 1) * QK] for c in range(NC)]

    def intra(d):
        o_t = jnp.concatenate(
            [jnp.dot(d["v_t"][hd * GLA_DV:(hd + 1) * GLA_DV], d["att_t"][hd * R:(hd + 1) * R],
                     preferred_element_type=F32) for hd in range(H)], axis=0)
        d["o"] = o_t.T

    def inter(tiles, order, st_ref, o_scr):
        state = st_ref[...]
        for d in tiles:
            parts = [None] * NC
            for c in order:
                parts[c] = lax.dot_general(d["qe"][c * CH:(c + 1) * CH], state.astype(BF16), nt_dims,
                                           preferred_element_type=F32)
                state = state * d["dec"][c] + jnp.where(st_keep, d["kv"][c], 0.0)
            o_scr[pl.ds(d["r0"], R), :] = d["o"] + jnp.concatenate(parts, axis=0)
        st_ref[...] = state

    def advance(f_tiles, b_tiles):
        fs = [load(t, 0, tri_lo) for t in f_tiles]
        bs = [load(t, QK, tri_up) for t in b_tiles]
        for d in fs:
            scale(d, CH - 1)
        for d in bs:
            scale(d, 0)
        for d in fs:
            scores(d, keep_up)
        for d in bs:
            scores(d, keep_lo)
        for d in fs + bs:
            intra(d)
        inter(fs, range(NC), stf, of_s)
        inter(bs, range(NC - 1, -1, -1), stb, ob_s)

    for s in range(n_ctx_tiles):
        advance([s], [n_ctx_tiles - 1 - s])
    n_lat_tiles = n_tiles - n_ctx_tiles
    per_step = GLA_TILES_PER_STEP if n_lat_tiles % GLA_TILES_PER_STEP == 0 else 1

    def step(s, carry):
        first = n_ctx_tiles + s * per_step
        last = n_tiles - 1 - s * per_step
        advance([first + j for j in range(per_step)], [last - j for j in range(per_step)])
        return carry

    lax.fori_loop(0, n_lat_tiles // per_step, step, 0)

    gmean = (_iota_div((DV, DV), 0, GLA_DV) == _iota_div((DV, DV), 1, GLA_DV)).astype(BF16) * (1.0 / GLA_DV)
    gn = gn_ref[...]

    def finish(i, carry):
        r0 = pl.multiple_of(i * R, R)
        o = of_s[pl.ds(r0, R), :] + ob_s[pl.ds(r0, R), :]
        oo = o * o
        hi = oo.astype(BF16)
        lo = (oo - hi.astype(F32)).astype(BF16)
        ms = jnp.dot(hi, gmean, preferred_element_type=F32) + jnp.dot(lo, gmean, preferred_element_type=F32)
        o_ref[0, pl.ds(r0, R), :] = (o * lax.rsqrt(ms + EPS) * gn).astype(BF16)
        return carry

    lax.fori_loop(0, n_tiles, finish, 0, unroll=True)


def _gla(gqkv, la, gla_norm, n_ctx, layer):
    B, T, _ = gqkv.shape
    kern = functools.partial(_gla_kernel, n_ctx=n_ctx)
    return pl.pallas_call(
        kern,
        grid=(B,),
        in_specs=[
            pl.BlockSpec((1, T, 2 * GLA_QK_DIM + GLA_V_DIM), lambda b: (b, 0, 0)),
            pl.BlockSpec((1, T, 2 * GLA_QK_DIM), lambda b: (b, 0, 0)),
            _layer_spec((1, GLA_V_DIM), layer),
        ],
        out_specs=pl.BlockSpec((1, T, GLA_V_DIM), lambda b: (b, 0, 0)),
        out_shape=jax.ShapeDtypeStruct((B, T, GLA_V_DIM), BF16),
        scratch_shapes=[pltpu.VMEM((T, GLA_V_DIM), F32), pltpu.VMEM((T, GLA_V_DIM), F32),
                        pltpu.VMEM((GLA_V_DIM, GLA_QK_DIM), F32),
                        pltpu.VMEM((GLA_V_DIM, GLA_QK_DIM), F32)],
        compiler_params=_params(48, 1),
        name="gla",
    )(gqkv, la, gla_norm)


N_MERGE_TILED = 5


def _pool_tile(u, before, after, pos0, seq_len):
    R = u.shape[0]
    ext = jnp.concatenate([before, u, after], axis=0)
    n = ext.shape[0]

    def shifted(a, k):
        return pltpu.roll(a, (-k) % n, 0)

    w2 = ext + shifted(ext, -1)
    w4 = shifted(w2, 1) + shifted(w2, -1)
    w8 = shifted(w4, 2) + shifted(w4, -2)
    w16 = shifted(w8, 4) + shifted(w8, -4)
    sums = [a[SUBLANE:SUBLANE + R] for a in (w2, w4, w8, w16)]
    lane_group = _iota_div((R, POOL_DIM), 1, POOL_GROUP_DIM)
    pos = lax.broadcasted_iota(jnp.int32, (R, POOL_DIM), 0) + pos0
    tot = sums[0]
    lo = jnp.full((R, POOL_DIM), POOL_WINDOWS[0] // 2, jnp.int32)
    hi = jnp.full((R, POOL_DIM), POOL_WINDOWS[0] - POOL_WINDOWS[0] // 2 - 1, jnp.int32)
    for g in range(1, len(POOL_WINDOWS)):
        w = POOL_WINDOWS[g]
        tot = jnp.where(lane_group == g, sums[g], tot)
        lo = jnp.where(lane_group == g, w // 2, lo)
        hi = jnp.where(lane_group == g, w - w // 2 - 1, hi)
    cnt = (jnp.minimum(pos + hi + 1, seq_len) - jnp.maximum(pos - lo, 0)).astype(F32)
    return tot / cnt - u


def _merge_kernel(*refs, n_x, n_ctx_tiles, tile_off, kept, n_ctx, n_lat):
    D = D_MODEL
    R = ROW_TILE
    x_refs, refs = refs[:n_x], refs[n_x:]
    n_tiled = PAIR * N_MERGE_TILED
    tiled = refs[:n_tiled]
    halos = refs[n_tiled:n_tiled + 2 * PAIR]
    mod_refs = refs[n_tiled + 2 * PAIR:n_tiled + 3 * PAIR]
    (gpost_ref, wba_ref, pbd_ref, ps_ref, wbp_ref, wbg_ref, wmg0_ref, wmg1_ref, wmg2_ref, wo_ref,
     o_ref) = refs[n_tiled + 3 * PAIR:]
    wmg_refs = (wmg0_ref, wmg1_ref, wmg2_ref)
    tiles_per_seq = n_ctx_tiles + n_lat // R

    for t in range(PAIR):
        att_ref, pu_ref, on_ref, sgr_ref, h_ref = tiled[t::PAIR]
        h = h_ref[...]

        def gate(j):
            return _sigmoid(jnp.dot(h, wmg_refs[j][...], preferred_element_type=F32))

        m = gate(0) * jnp.dot(att_ref[...], wba_ref[...], preferred_element_type=F32)

        i = lax.rem(pl.program_id(0) * PAIR + t, kept) + tile_off
        is_ctx = i < n_ctx_tiles
        seq_start = (i == 0) | (i == n_ctx_tiles)
        seq_end = (i == n_ctx_tiles - 1) | (i == tiles_per_seq - 1)
        before = jnp.where(seq_start, 0.0, halos[2 * t][...])
        after = jnp.where(seq_end, 0.0, halos[2 * t + 1][...])
        d_pool = _pool_tile(pu_ref[...], before, after, jnp.where(is_ctx, i, i - n_ctx_tiles) * R,
                            jnp.where(is_ctx, n_ctx, n_lat))
        yp = jnp.dot(d_pool.astype(BF16), pbd_ref[...], preferred_element_type=F32) * ps_ref[...]
        m = m + gate(1) * jnp.dot(yp.astype(BF16), wbp_ref[...], preferred_element_type=F32)
        m = m + gate(2) * jnp.dot(on_ref[...] * sgr_ref[...], wbg_ref[...], preferred_element_type=F32)
        y = jnp.dot(m.astype(BF16), wo_ref[...], preferred_element_type=F32)
        ms = jnp.mean(y * y, axis=-1, keepdims=True)
        o_ref[t * R:(t + 1) * R, :] = (
            _stream_tile(x_refs, t, tiles_per_seq, n_ctx_tiles)
            + mod_refs[t][0][:, 2 * D:3 * D] * (y * lax.rsqrt(ms + EPS) * gpost_ref[...]))


def _merge(x_src, y_att, pu, on, sgr, h, mods, g_post, wba, pbd, ps, wbp, wbg, wmg, wo,
           n_ctx_tiles, tile_off, layer):
    B, T, _ = y_att.shape
    D = D_MODEL
    R = ROW_TILE
    H = SUBLANE
    nt = T // R
    kept = nt - tile_off
    assert (B * kept) % PAIR == 0
    ctx_row = mods.shape[1] - 1
    rpb = R // H
    last_blk = B * T // H - 1

    def halo_spec(t, after):
        def index(s):
            b, i = coords(s, t)
            tile = b * nt + i
            return ((jnp.minimum((tile + 1) * rpb, last_blk) if after else jnp.maximum(tile * rpb - 1, 0)), 0)
        return pl.BlockSpec((H, POOL_DIM), index)

    def coords(s, t):
        tile = s * PAIR + t
        return lax.div(tile, kept), lax.rem(tile, kept) + tile_off

    def tile_specs(width):
        def spec(t):
            def index(s):
                b, i = coords(s, t)
                return (b * nt + i, 0)
            return pl.BlockSpec((R, width), index)
        return [spec(t) for t in range(PAIR)]

    def mod_spec(t):
        def index(s):
            b, i = coords(s, t)
            return (layer, jnp.where(i < n_ctx_tiles, ctx_row, b), 0, 0)
        return pl.BlockSpec((None, 1, 1, 6 * D), index)

    tiled = (y_att, pu, on, sgr, h)
    assert len(tiled) == N_MERGE_TILED
    flat = [a.reshape(B * T, a.shape[-1]) for a in tiled]
    pu_flat = flat[1]
    if isinstance(x_src, tuple):
        assert tile_off == 0
        x_arrays, x_specs = _stream_specs(x_src, coords, nt, n_ctx_tiles)
    else:
        x_arrays, x_specs = [x_src.reshape(B * T, D)] * PAIR, tile_specs(D)
    kern = functools.partial(_merge_kernel, n_x=len(x_arrays), n_ctx_tiles=n_ctx_tiles, tile_off=tile_off,
                             kept=kept, n_ctx=n_ctx_tiles * R, n_lat=T - n_ctx_tiles * R)
    out = pl.pallas_call(
        kern,
        grid=(B * kept // PAIR,),
        in_specs=[
            *x_specs,
            *(spec for a in flat for spec in tile_specs(a.shape[-1])),
            *(halo_spec(t, after) for t in range(PAIR) for after in (False, True)),
            *(mod_spec(t) for t in range(PAIR)),
            _layer_spec((1, D), layer),
            _layer_spec((ATT_DIM, D), layer), _layer_spec((POOL_DIM, POOL_DIM), layer),
            _layer_spec((1, POOL_DIM), layer), _layer_spec((POOL_DIM, D), layer),
            _layer_spec((GLA_V_DIM, D), layer),
            *(pl.BlockSpec((None, D, D), lambda s, j=j: (layer, 0, j),
                           pipeline_mode=pl.Buffered(1)) for j in range(3)),
            _layer_spec((D, D), layer),
        ],
        out_specs=pl.BlockSpec((PAIR * R, D), lambda s: (s, 0)),
        out_shape=jax.ShapeDtypeStruct((B * kept * R, D), F32),
        compiler_params=_params(48, 1),
        name="merge",
    )(*x_arrays, *(a for a in flat for _ in range(PAIR)), *([pu_flat] * (2 * PAIR)), *([mods] * PAIR),
      g_post, wba, pbd, ps, wbp, wbg, wmg, wmg, wmg, wo)
    return out.reshape(B, kept * R, D)


def _ffn_kernel(*refs, n_ctx_tiles, tiles_per_seq):
    D = D_MODEL
    R = ROW_TILE
    S = SUBLANE
    G = R // S
    CK = FF_CHUNK
    n_chunks = D_FF // CK
    NT = FFN_TILES
    x_ref = refs[0]
    halo_refs = [(refs[1 + 2 * t], refs[2 + 2 * t]) for t in range(NT)]
    mods = [refs[1 + 2 * NT + t][0] for t in range(NT)]
    (gpre_ref, wup_ref, cw_ref, cb_ref, wdn_ref, gpost_ref, o_ref,
     he_s, u_s, act_s, acc_s) = refs[1 + 3 * NT:]
    r_idx = lax.broadcasted_iota(jnp.int32, (R, R), 0)
    c_idx = lax.broadcasted_iota(jnp.int32, (R, R), 1)

    def seq_of(p):
        return jnp.bitwise_and(p, S - 1) * G + lax.shift_right_logical(p, SUBLANE_LOG2)

    to_perm = (c_idx == seq_of(r_idx)).astype(BF16)
    to_seq = (r_idx == seq_of(c_idx)).astype(BF16)
    sub = lax.broadcasted_iota(jnp.int32, (S, 2 * CK), 0)
    halo_row = lax.broadcasted_iota(jnp.int32, (S, 1), 0)

    def chunk_cols(ref, j):
        return jnp.concatenate([ref[:, j * CK:(j + 1) * CK], ref[:, D_FF + j * CK:D_FF + (j + 1) * CK]],
                               axis=1)

    def prologue(t):
        i = lax.rem(pl.program_id(0) * NT + t, tiles_per_seq)
        scale, shift = mods[t][:, 4 * D:5 * D], mods[t][:, 3 * D:4 * D]
        h = _modulated_norm(x_ref[t * R:(t + 1) * R, :], gpre_ref[...], scale, shift).astype(BF16)
        he_s[t, 0:R, :] = jnp.dot(to_perm, h, preferred_element_type=F32).astype(BF16)
        xp_ref, xn_ref = halo_refs[t]
        halo = jnp.concatenate([xp_ref[S - 1:S, :], xn_ref[0:1, :], jnp.zeros((S - 2, D), F32)], axis=0)
        seq_start = (i == 0) | (i == n_ctx_tiles)
        seq_end = (i == n_ctx_tiles - 1) | (i == tiles_per_seq - 1)
        outside = ((halo_row == 0) & seq_start) | ((halo_row == 1) & seq_end) | (halo_row >= 2)
        he_s[t, R:R + S, :] = jnp.where(
            outside, 0.0, _modulated_norm(halo, gpre_ref[...], scale, shift)).astype(BF16)

    def up(t, j):
        he = he_s[t]
        u_s[t, j % 2, :, 0:CK] = jnp.dot(he, wup_ref[:, j * CK:(j + 1) * CK], preferred_element_type=F32)
        u_s[t, j % 2, :, CK:] = jnp.dot(he, wup_ref[:, D_FF + j * CK:D_FF + (j + 1) * CK],
                                        preferred_element_type=F32)

    def finish(t, j):
        u = u_s[t, j % 2, 0:R, :]
        edge = u_s[t, j % 2, R:R + S, :]
        prev0 = jnp.where(sub == 0, edge[0:1], pltpu.roll(u[R - S:R], 1, 0))
        next_last = jnp.where(sub == S - 1, edge[1:2], pltpu.roll(u[0:S], S - 1, 0))
        prev = jnp.concatenate([prev0, u[0:R - S]], axis=0)
        nxt = jnp.concatenate([u[S:R], next_last], axis=0)
        cw = chunk_cols(cw_ref, j)
        c = prev * cw[0:1] + u * cw[1:2] + nxt * cw[2:3] + chunk_cols(cb_ref, j)
        g = c[:, CK:]
        act_s[t, :, j * CK:(j + 1) * CK] = (c[:, :CK] * (g * _sigmoid(g))).astype(BF16)

    def down(t, j0, j1):
        y = jnp.dot(act_s[t, :, j0 * CK:j1 * CK], wdn_ref[j0 * CK:j1 * CK, :], preferred_element_type=F32)
        if j0 == 0:
            acc_s[t] = y
        else:
            acc_s[t] += y

    def epilogue(t):
        y = jnp.dot(to_seq, acc_s[t].astype(BF16), preferred_element_type=F32)
        ms = jnp.mean(y * y, axis=-1, keepdims=True)
        o_ref[t * R:(t + 1) * R, :] = (x_ref[t * R:(t + 1) * R, :]
                                       + mods[t][:, 5 * D:6 * D] * (y * lax.rsqrt(ms + EPS) * gpost_ref[...]))

    def body(t, after_first_up):
        up(t, 0)
        after_first_up()
        group_start = 0
        for j in range(n_chunks):
            if j + 1 < n_chunks:
                up(t, j + 1)
            finish(t, j)
            if j + 1 - group_start == FF_GROUP or j + 1 == n_chunks:
                down(t, group_start, j + 1)
                group_start = j + 1

    def between(t):
        if t > 0:
            epilogue(t - 1)
        if t + 1 < NT:
            prologue(t + 1)

    prologue(0)
    for t in range(NT):
        body(t, functools.partial(between, t))
    epilogue(NT - 1)


def _ffn(xa, mods, g_pre, w_up, conv_w, conv_b, w_down, g_post, n_ctx_tiles, layer):
    B, T, D = xa.shape
    R = ROW_TILE
    H = SUBLANE
    nt = T // R
    NT = FFN_TILES
    assert (B * nt) % NT == 0
    ctx_row = mods.shape[1] - 1
    rpb = R // H
    RE = R + H
    last_blk = B * T // H - 1

    def halo(t, after):
        def index(s):
            tile = s * NT + t
            return ((jnp.minimum((tile + 1) * rpb, last_blk) if after else jnp.maximum(tile * rpb - 1, 0)), 0)
        return pl.BlockSpec((H, D), index)

    def mod_spec(t):
        def index(s):
            tile = s * NT + t
            return (layer, jnp.where(lax.rem(tile, nt) < n_ctx_tiles, ctx_row, lax.div(tile, nt)), 0, 0)
        return pl.BlockSpec((None, 1, 1, 6 * D), index)

    x2 = xa.reshape(B * T, D)
    kern = functools.partial(_ffn_kernel, n_ctx_tiles=n_ctx_tiles, tiles_per_seq=nt)
    out = pl.pallas_call(
        kern,
        grid=(B * nt // NT,),
        in_specs=[
            pl.BlockSpec((NT * R, D), lambda s: (s, 0)),
            *(halo(t, after) for t in range(NT) for after in (False, True)),
            *(mod_spec(t) for t in range(NT)),
            _layer_spec((1, D), layer),
            _layer_spec((D, 2 * D_FF), layer), _layer_spec((3, 2 * D_FF), layer),
            _layer_spec((1, 2 * D_FF), layer), _layer_spec((D_FF, D), layer),
            _layer_spec((1, D), layer),
        ],
        out_specs=pl.BlockSpec((NT * R, D), lambda s: (s, 0)),
        out_shape=jax.ShapeDtypeStruct((B * T, D), F32),
        scratch_shapes=[pltpu.VMEM((NT, RE, D), BF16),
                        pltpu.VMEM((NT, 2, RE, 2 * FF_CHUNK), F32),
                        pltpu.VMEM((NT, R, D_FF), BF16),
                        pltpu.VMEM((NT, R, D), F32)],
        compiler_params=_params(56, 1),
        name="ffn",
    )(*([x2] * (1 + 2 * NT)), *([mods] * NT), g_pre, w_up, conv_w, conv_b, w_down, g_post)
    return out.reshape(B, T, D)


def _rope_table(n_ctx, n_lat):
    rows = n_lat // GRID_W
    row = jnp.repeat(jnp.arange(rows, dtype=F32), GRID_W)
    col = jnp.tile(jnp.arange(GRID_W, dtype=F32), rows)
    n = HEAD_DIM // 4
    inv = ROPE_BASE ** (-jnp.arange(n, dtype=F32) / n)
    ang = jnp.concatenate([row[:, None] * inv, col[:, None] * inv], axis=-1)
    cos, sin = jnp.cos(ang), jnp.sin(ang)
    zero = jnp.zeros_like(sin)
    reps = LANE // HEAD_DIM
    cos_t = jnp.tile(jnp.concatenate([cos, cos], axis=-1), (1, reps))
    sin_lo = jnp.tile(jnp.concatenate([-sin, zero], axis=-1), (1, reps))
    sin_hi = jnp.tile(jnp.concatenate([zero, sin], axis=-1), (1, reps))
    lat = jnp.concatenate([cos_t, sin_lo, sin_hi], axis=-1)
    ctx = jnp.concatenate([jnp.ones((n_ctx, LANE), F32), jnp.zeros((n_ctx, 2 * LANE), F32)], axis=-1)
    return jnp.concatenate([ctx, lat], axis=0)


def kernel(x, c, ctx, c_ctx, w_ada, b_ada, g_pre_mix, g_post_mix, g_pre_ffn, g_post_ffn, w_in, att_sink, pool_w, pool_scale, gla_wa2, gla_ba, gla_norm, w_br_att, w_br_pool, w_br_gla, w_o, w_up, conv_w, conv_b, w_down):
    B, L, D = x.shape
    C = ctx.shape[1]
    depth = w_in.shape[0]
    R = ROW_TILE
    assert D == D_MODEL and C % R == 0 and L % R == 0 and L % GRID_W == 0
    n_ctx_tiles = C // R

    mod_rows = -(-(B + 1) // SUBLANE) * SUBLANE
    cc = jnp.zeros((mod_rows, D), F32).at[:B].set(c).at[mod_rows - 1].set(c_ctx)
    mods = _ada_table(cc, w_ada, b_ada).reshape(depth, mod_rows, 1, 6 * D)

    rope_tab = _rope_table(C, L)
    w_in_p, w_mg = _split_w_in(w_in.astype(BF16))
    wa = jnp.zeros((depth, LANE, 2 * GLA_QK_DIM), F32)
    wa = wa.at[:, 0:GLA_GATE_RANK, 0:GLA_QK_DIM].set(gla_wa2[:, 0])
    wa = wa.at[:, GLA_GATE_RANK:2 * GLA_GATE_RANK, GLA_QK_DIM:].set(gla_wa2[:, 1])
    wa_hi = wa.astype(BF16)
    wa = jnp.concatenate([wa_hi, wa_hi, (wa - wa_hi.astype(F32)).astype(BF16)], axis=1)
    ba = gla_ba.reshape(depth, 1, 2 * GLA_QK_DIM)
    pbd = jnp.zeros((depth, POOL_DIM, POOL_DIM), F32)
    for g in range(len(POOL_WINDOWS)):
        sl = slice(g * POOL_GROUP_DIM, (g + 1) * POOL_GROUP_DIM)
        pbd = pbd.at[:, sl, sl].set(pool_w[:, g])
    pbd = pbd.astype(BF16)
    wba, wbp, wbg, wo = (w.astype(BF16) for w in (w_br_att, w_br_pool, w_br_gla, w_o))
    wup, wdn = w_up.astype(BF16), w_down.astype(BF16)

    def rows(t):
        return t.reshape(depth, 1, -1)

    g_pre_mix, g_post_mix, g_pre_ffn, g_post_ffn, pool_scale, gla_norm, conv_b = (
        rows(t) for t in (g_pre_mix, g_post_mix, g_pre_ffn, g_post_ffn, pool_scale, gla_norm, conv_b))

    xa = (ctx, x) if depth > 1 else jnp.concatenate([ctx, x], axis=1)
    for l in range(depth):
        last = l == depth - 1
        off = n_ctx_tiles if last else 0
        q, kv, pu, gqkv, sgr, la, h = _in_proj(xa, mods, g_pre_mix, w_in_p, rope_tab, wa, ba, n_ctx_tiles, l)
        y_att = _attention(q, kv, att_sink, C, off, l)
        on = _gla(gqkv, la, gla_norm, C, l)
        xa = _merge(xa, y_att, pu, on, sgr, h, mods, g_post_mix, wba, pbd, pool_scale, wbp, wbg, w_mg, wo,
                    n_ctx_tiles, off, l)
        xa = _ffn(xa, mods, g_pre_ffn, wup, conv_w, conv_b, wdn, g_post_ffn, n_ctx_tiles - off, l)
    return xa
```

```python
import functools

import jax
import jax.numpy as jnp
import numpy as np
from jax import lax
from jax.experimental import pallas as pl
from jax.experimental.pallas import tpu as pltpu

F32 = jnp.float32
BF16 = jnp.bfloat16
HIGHEST = lax.Precision.HIGHEST

D_MODEL = 1024
GRID_W = 64
EPS = 1e-6

HEAD_DIM = 64
ATT_HEADS = 8
ATT_KV_HEADS = 2
ATT_GROUP = ATT_HEADS // ATT_KV_HEADS
WINDOW = 128
ROPE_BASE = 10000.0
ATT_DIM = ATT_HEADS * HEAD_DIM
KV_DIM = ATT_KV_HEADS * HEAD_DIM

POOL_WINDOWS = (2, 4, 8, 16)
POOL_GROUP_DIM = 64
POOL_DIM = len(POOL_WINDOWS) * POOL_GROUP_DIM

GLA_HEADS = 4
GLA_DK = 32
GLA_DV = 64
GLA_GATE_RANK = 16
GLA_TAU = 16.0
N_BRANCH = 3
GLA_QK_DIM = GLA_HEADS * GLA_DK
GLA_V_DIM = GLA_HEADS * GLA_DV
GLA_CHUNK = 64
GLA_TILES_PER_STEP = 4

D_FF = 2816
FF_CHUNK = 256
FFN_TILES = 2
FF_GROUP = 4
PAIR = 2

LANE = 128
SUBLANE = 8
SUBLANE_LOG2 = SUBLANE.bit_length() - 1
ROW_TILE = 256

_O_AQ = 0
_O_AK = _O_AQ + ATT_DIM
_O_AV = _O_AK + KV_DIM
_O_PU = _O_AV + KV_DIM
_O_GQ = _O_PU + POOL_DIM
_O_GK = _O_GQ + GLA_QK_DIM
_O_GV = _O_GK + GLA_QK_DIM
_O_GR = _O_GV + GLA_V_DIM
_O_GLR = _O_GR + GLA_V_DIM
IN_DIM_PAD = _O_GLR + LANE
_O_MG = _O_GLR + 2 * GLA_GATE_RANK
W_SPLIT_ROWS = 256

NEG_BIG = -1e30
LOG2E = float(np.log2(np.e))


def _params(vmem_mb, n_axes):
    return pltpu.CompilerParams(
        dimension_semantics=("arbitrary",) * n_axes,
        vmem_limit_bytes=vmem_mb * 1024 * 1024,
    )


def _layer_spec(shape, layer):
    nd = len(shape)
    return pl.BlockSpec((None,) + tuple(shape), lambda *_: (layer,) + (0,) * nd,
                        pipeline_mode=pl.Buffered(1))


def _sigmoid(v):
    return 1.0 / (1.0 + jnp.exp(-v))


def _iota_div(shape, axis, divisor):
    shift = int(np.log2(divisor))
    assert 1 << shift == divisor
    return lax.shift_right_logical(lax.broadcasted_iota(jnp.int32, shape, axis), shift)


def _modulated_norm(x, gain, scale, shift):
    ms = jnp.mean(x * x, axis=-1, keepdims=True)
    return (x * lax.rsqrt(ms + EPS) * gain) * (1.0 + scale) + shift


def _ada_kernel(c_ref, w_ref, b_ref, o_ref):
    c = c_ref[...]
    a = c * _sigmoid(c)
    o_ref[0] = jnp.dot(a, w_ref[0], precision=HIGHEST, preferred_element_type=F32) + b_ref[0]


def _ada_table(cc, w_ada, b_ada):
    depth = w_ada.shape[0]
    rows = cc.shape[0]
    width = 2 * D_MODEL
    n_col = w_ada.shape[2] // width
    return pl.pallas_call(
        _ada_kernel,
        grid=(depth, n_col),
        in_specs=[
            pl.BlockSpec((rows, D_MODEL), lambda l, j: (0, 0)),
            pl.BlockSpec((1, D_MODEL, width), lambda l, j: (l, 0, j)),
            pl.BlockSpec((1, 1, width), lambda l, j: (l, 0, j)),
        ],
        out_specs=pl.BlockSpec((1, rows, width), lambda l, j: (l, 0, j)),
        out_shape=jax.ShapeDtypeStruct((depth, rows, w_ada.shape[2]), F32),
        compiler_params=_params(32, 2),
        name="ada_table",
    )(cc, w_ada, b_ada.reshape(depth, 1, -1))


def _split_w_in_kernel(w_ref, wp_ref, wg_ref):
    wp_ref[0] = w_ref[0, :, 0:IN_DIM_PAD]
    wg_ref[0] = w_ref[0, :, _O_MG:_O_MG + N_BRANCH * D_MODEL]


def _split_w_in(w_in):
    depth, D, n_in = w_in.shape
    assert n_in == _O_MG + N_BRANCH * D and D % W_SPLIT_ROWS == 0
    rows = W_SPLIT_ROWS
    return pl.pallas_call(
        _split_w_in_kernel,
        grid=(depth, D // rows),
        in_specs=[pl.BlockSpec((1, rows, n_in), lambda l, r: (l, r, 0))],
        out_specs=[pl.BlockSpec((1, rows, IN_DIM_PAD), lambda l, r: (l, r, 0)),
                   pl.BlockSpec((1, rows, N_BRANCH * D), lambda l, r: (l, r, 0))],
        out_shape=[jax.ShapeDtypeStruct((depth, D, IN_DIM_PAD), BF16),
                   jax.ShapeDtypeStruct((depth, D, N_BRANCH * D), BF16)],
        compiler_params=_params(48, 2),
        name="split_w_in",
    )(w_in)


def _stream_tile(x_refs, t, tiles_per_seq, n_ctx_tiles):
    if len(x_refs) == 1:
        return x_refs[0][t * ROW_TILE:(t + 1) * ROW_TILE, :]
    if len(x_refs) == PAIR:
        return x_refs[t][...]
    i = lax.rem(pl.program_id(0) * PAIR + t, tiles_per_seq)
    return jnp.where(i < n_ctx_tiles, x_refs[t][...], x_refs[PAIR + t][...])


def _stream_specs(x_src, tile_coords, tiles_per_seq, n_ctx_tiles):
    R = ROW_TILE
    if not isinstance(x_src, tuple):
        B, T, D = x_src.shape
        return [x_src.reshape(B * T, D)], [pl.BlockSpec((PAIR * R, D), lambda s: (s, 0))]
    ctx, lat = x_src
    B, _, D = ctx.shape
    n_lat_tiles = tiles_per_seq - n_ctx_tiles

    def spec(t, is_ctx):
        def index(s):
            b, i = tile_coords(s, t)
            if is_ctx:
                return (b * n_ctx_tiles + jnp.minimum(i, n_ctx_tiles - 1), 0)
            return (b * n_lat_tiles + jnp.maximum(i - n_ctx_tiles, 0), 0)
        return pl.BlockSpec((R, D), index)

    arrays = [ctx.reshape(-1, D)] * PAIR + [lat.reshape(-1, D)] * PAIR
    return arrays, [spec(t, True) for t in range(PAIR)] + [spec(t, False) for t in range(PAIR)]


def _in_proj_kernel(*refs, n_x, tiles_per_seq, n_ctx_tiles):
    x_refs = refs[:n_x]
    (moda_ref, modb_ref, g_ref, w_ref, ropea_ref, ropeb_ref, wa_ref, ba_ref,
     q_ref, kv_ref, pu_ref, gqkv_ref, sgr_ref, la_ref, h_ref) = refs[n_x:]
    D = D_MODEL
    R = ROW_TILE
    mods = (moda_ref[0], modb_ref[0])
    ropes = (ropea_ref, ropeb_ref)

    def norm(t):
        rows = pl.ds(t * R, R)
        h_ref[rows, :] = _modulated_norm(_stream_tile(x_refs, t, tiles_per_seq, n_ctx_tiles), g_ref[...],
                                         mods[t][:, D:2 * D], mods[t][:, 0:D]).astype(BF16)

    def project(t, after_first):
        rows = pl.ds(t * R, R)
        h = h_ref[rows, :]
        rope_ref = ropes[t]

        def proj(lo, width):
            return jnp.dot(h, w_ref[:, lo:lo + width], preferred_element_type=F32)

        cosf = rope_ref[:, 0:LANE]
        sin_lo = rope_ref[:, LANE:2 * LANE]
        sin_hi = rope_ref[:, 2 * LANE:3 * LANE]

        def rope(v):
            return (v * cosf + pltpu.roll(v, LANE - HEAD_DIM // 2, 1) * sin_lo
                    + pltpu.roll(v, HEAD_DIM // 2, 1) * sin_hi)

        aq = proj(_O_AQ, ATT_DIM)
        after_first()
        for j in range(ATT_DIM // LANE):
            q_ref[rows, j * LANE:(j + 1) * LANE] = (
                rope(aq[:, j * LANE:(j + 1) * LANE]) * (HEAD_DIM ** -0.5 * LOG2E)).astype(BF16)
        kv_ref[rows, 0:KV_DIM] = rope(proj(_O_AK, KV_DIM)).astype(BF16)
        kv_ref[rows, KV_DIM:2 * KV_DIM] = proj(_O_AV, KV_DIM).astype(BF16)
        pu_ref[rows, :] = proj(_O_PU, POOL_DIM)
        gqkv_ref[rows, 0:GLA_QK_DIM] = (proj(_O_GQ, GLA_QK_DIM) * (GLA_DK ** -0.5)).astype(BF16)
        gqkv_ref[rows, GLA_QK_DIM:] = proj(_O_GK, GLA_QK_DIM + GLA_V_DIM).astype(BF16)
        gr = proj(_O_GR, GLA_V_DIM)
        sgr_ref[rows, :] = (gr * _sigmoid(gr)).astype(BF16)

        glr = proj(_O_GLR, LANE)
        glr_hi = glr.astype(BF16)
        glr_lo = (glr - glr_hi.astype(F32)).astype(BF16)
        z = jnp.dot(jnp.concatenate([glr_hi, glr_lo, glr_hi], axis=1), wa_ref[...],
                    preferred_element_type=F32) + ba_ref[...]
        la_ref[rows, :] = (jnp.minimum(z, 0.0) - jnp.log(1.0 + jnp.exp(-jnp.abs(z)))) * (1.0 / GLA_TAU)

    norm(0)
    project(0, lambda: norm(1))
    project(1, lambda: None)


def _tile_of_pair(t, tiles_per_seq, n_ctx_tiles, ctx_row, layer):
    def seq_tile(s):
        return lax.rem(s * PAIR + t, tiles_per_seq)

    def mod_index(s):
        tile = s * PAIR + t
        return (layer, jnp.where(lax.rem(tile, tiles_per_seq) < n_ctx_tiles, ctx_row,
                                 lax.div(tile, tiles_per_seq)), 0, 0)

    return seq_tile, pl.BlockSpec((None, 1, 1, 6 * D_MODEL), mod_index)


def _in_proj(x_src, mods, g_pre, w_in, rope_tab, wa, ba, n_ctx_tiles, layer):
    if isinstance(x_src, tuple):
        B, D = x_src[0].shape[0], x_src[0].shape[2]
        T = x_src[0].shape[1] + x_src[1].shape[1]
    else:
        B, T, D = x_src.shape
    R = ROW_TILE
    nt = T // R
    assert (B * nt) % PAIR == 0
    ctx_row = mods.shape[1] - 1
    x_arrays, x_specs = _stream_specs(
        x_src, lambda s, t: (lax.div(s * PAIR + t, nt), lax.rem(s * PAIR + t, nt)), nt, n_ctx_tiles)

    def pair(width):
        return pl.BlockSpec((PAIR * R, width), lambda s: (s, 0))

    seq_tiles, mod_specs = zip(*(_tile_of_pair(t, nt, n_ctx_tiles, ctx_row, layer) for t in range(PAIR)))
    rope_specs = [pl.BlockSpec((R, 3 * LANE), lambda s, f=f: (f(s), 0)) for f in seq_tiles]
    widths = (ATT_DIM, 2 * KV_DIM, POOL_DIM, 2 * GLA_QK_DIM + GLA_V_DIM, GLA_V_DIM,
              2 * GLA_QK_DIM, D)
    dtypes = (BF16, BF16, F32, BF16, BF16, F32, BF16)
    kern = functools.partial(_in_proj_kernel, n_x=len(x_arrays), tiles_per_seq=nt, n_ctx_tiles=n_ctx_tiles)
    outs = pl.pallas_call(
        kern,
        grid=(B * nt // PAIR,),
        in_specs=[
            *x_specs, *mod_specs,
            _layer_spec((1, D), layer),
            _layer_spec((D, IN_DIM_PAD), layer),
            *rope_specs,
            _layer_spec((3 * LANE, 2 * GLA_QK_DIM), layer),
            _layer_spec((1, 2 * GLA_QK_DIM), layer),
        ],
        out_specs=[pair(w) for w in widths],
        out_shape=[jax.ShapeDtypeStruct((B * T, w), dt) for w, dt in zip(widths, dtypes)],
        compiler_params=_params(48, 1),
        name="in_proj",
    )(*x_arrays, mods, mods, g_pre, w_in, rope_tab, rope_tab, wa, ba)
    return [o.reshape(B, T, -1) for o in outs]


def _attn_kernel(sink_ref, q_ref, kv_ref, o_ref, s_s, *, layer, tile_off, n_ctx_tiles, n_ctx, n_lat):
    R = ROW_TILE
    W = WINDOW
    assert R == 2 * W == 2 * LANE
    n_win = R // W + 2
    i = pl.program_id(1) + tile_off
    n_blk = (n_ctx + n_lat) // W
    nt_dims = (((1,), (1,)), ((), ()))
    q = q_ref[0]

    def heads(kv, split, merge):
        n_keys = kv.shape[0]
        ks, v_ext_ts = [], []
        for g in range(ATT_KV_HEADS):
            ks.append(kv[:, g * HEAD_DIM:(g + 1) * HEAD_DIM])
            v = kv[:, KV_DIM + g * HEAD_DIM:KV_DIM + (g + 1) * HEAD_DIM].astype(F32)
            v_ext_ts.append(
                jnp.concatenate([v, jnp.ones((n_keys, LANE - HEAD_DIM), F32)], axis=1).T.astype(BF16))

        n_slots = s_s.shape[0]

        def scores(h):
            qh = q[:, h * HEAD_DIM:(h + 1) * HEAD_DIM]
            s_s[h % n_slots, 0:n_keys, :] = lax.dot_general(ks[h // ATT_GROUP], qh, nt_dims,
                                                             preferred_element_type=F32)

        n_at_once = n_slots // 2
        for h in range(n_at_once):
            scores(h)
        for h0 in range(0, ATT_HEADS, n_at_once):
            batch = tuple(range(h0, h0 + n_at_once))
            for h in batch:
                if h + n_at_once < ATT_HEADS:
                    scores(h + n_at_once)
            sinks = [sink_ref[layer, h] * LOG2E for h in batch]
            halves = [split(s_s.at[h % n_slots]) for h in batch]
            ms = [[jnp.maximum(jnp.max(s_half, axis=0, keepdims=True), sk) for s_half in hs]
                  for hs, sk in zip(halves, sinks)]
            ps = [[jnp.exp2(s_half - m).astype(BF16) for s_half, m in zip(hs, mh)]
                  for hs, mh in zip(halves, ms)]
            o_exts = [jnp.dot(v_ext_ts[h // ATT_GROUP], merge(p), preferred_element_type=F32)
                      for h, p in zip(batch, ps)]
            out_t = [o_ext[0:HEAD_DIM] / (o_ext[HEAD_DIM:HEAD_DIM + 1]
                                          + jnp.exp2(sk - jnp.concatenate(mh, axis=1)))
                     for o_ext, sk, mh in zip(o_exts, sinks, ms)]
            for j in range(0, n_at_once, 2):
                o_ref[0, :, (h0 + j) * HEAD_DIM:(h0 + j + 2) * HEAD_DIM] = (
                    jnp.concatenate(out_t[j:j + 2], axis=0).T.astype(BF16))

    @pl.when(i < n_ctx_tiles)
    def _context_queries():
        heads(kv_ref[0, 0:n_ctx, :],
              lambda ref: [ref[0:n_ctx, a * LANE:(a + 1) * LANE] for a in range(R // LANE)],
              lambda ps: jnp.concatenate(ps, axis=1))

    @pl.when(i >= n_ctx_tiles)
    def _latent_queries():
        first_blk = i * (R // W) - 1
        parts = []
        for blk in range(n_win):
            idx = jnp.clip(first_blk + blk, 0, n_blk - 1)
            parts.append(kv_ref[0, pl.ds(pl.multiple_of(idx * W, W), W), :])
        parts.append(kv_ref[0, 0:n_ctx, :])
        n_loc = n_win * W
        kj = lax.broadcasted_iota(jnp.int32, (W, LANE), 0)
        qi = lax.broadcasted_iota(jnp.int32, (W, LANE), 1)
        below = jnp.where(kj >= qi, 0.0, NEG_BIG).astype(F32)
        above = jnp.where(kj <= qi, 0.0, NEG_BIG).astype(F32)
        below_first = below + jnp.where(i == n_ctx_tiles, NEG_BIG, 0.0)
        above_last = above + jnp.where(i == n_blk // (R // W) - 1, NEG_BIG, 0.0)

        def split(ref):
            def blk(b, a):
                return ref[b * W:(b + 1) * W, a * LANE:(a + 1) * LANE]
            return [jnp.concatenate([blk(0, 0) + below_first, blk(1, 0), blk(2, 0) + above,
                                     ref[n_loc:n_loc + n_ctx, 0:LANE]], axis=0),
                    jnp.concatenate([blk(1, 1) + below, blk(2, 1), blk(3, 1) + above_last,
                                     ref[n_loc:n_loc + n_ctx, LANE:2 * LANE]], axis=0)]

        def merge(ps):
            zero = jnp.zeros((W, LANE), BF16)
            live = (n_win - 1) * W
            return jnp.concatenate(
                [jnp.concatenate([ps[0][0:live], zero, ps[0][live:]], axis=0),
                 jnp.concatenate([zero, ps[1]], axis=0)], axis=1)

        heads(jnp.concatenate(parts, axis=0), split, merge)


def _attention(q, kv, sink, n_ctx, tile_off, layer):
    B, T, _ = q.shape
    R = ROW_TILE
    nt = T // R
    kern = functools.partial(_attn_kernel, layer=layer, tile_off=tile_off, n_ctx_tiles=n_ctx // R,
                             n_ctx=n_ctx, n_lat=T - n_ctx)
    return pl.pallas_call(
        kern,
        grid=(B, nt - tile_off),
        in_specs=[
            pl.BlockSpec(memory_space=pltpu.SMEM),
            pl.BlockSpec((1, R, ATT_DIM), lambda b, i: (b, i + tile_off, 0)),
            pl.BlockSpec((1, T, 2 * KV_DIM), lambda b, i: (b, 0, 0)),
        ],
        out_specs=pl.BlockSpec((1, R, ATT_DIM), lambda b, i: (b, i, 0)),
        out_shape=jax.ShapeDtypeStruct((B, T - tile_off * R, ATT_DIM), BF16),
        scratch_shapes=[pltpu.VMEM((2 * ATT_GROUP, R + 2 * WINDOW + n_ctx, R), F32)],
        compiler_params=_params(48, 2),
        name="attention",
    )(sink, q, kv)


def _gla_kernel(qkv_ref, la_ref, gn_ref, o_ref, of_s, ob_s, stf, stb, *, n_ctx):
    CH = GLA_CHUNK
    R = ROW_TILE
    NC = R // CH
    T = of_s.shape[0]
    n_tiles = T // R
    n_ctx_tiles = n_ctx // R
    QK = GLA_QK_DIM
    DV = GLA_V_DIM
    H = GLA_HEADS

    stf[...] = jnp.zeros(stf.shape, F32)
    stb[...] = jnp.zeros(stb.shape, F32)

    same_chunk = _iota_div((R, R), 0, CH) == _iota_div((R, R), 1, CH)
    ri = lax.broadcasted_iota(jnp.int32, (R, R), 0)
    ci = lax.broadcasted_iota(jnp.int32, (R, R), 1)
    tri_lo = (same_chunk & (ci <= ri)).astype(BF16)
    tri_up = (same_chunk & (ci >= ri)).astype(BF16)
    rs = jnp.bitwise_and(lax.broadcasted_iota(jnp.int32, (H * R, R), 0), R - 1)
    cs = lax.broadcasted_iota(jnp.int32, (H * R, R), 1)
    same_chunk_h = (lax.shift_right_logical(rs, CH.bit_length() - 1)
                    == lax.shift_right_logical(cs, CH.bit_length() - 1))
    keep_lo = same_chunk_h & (cs <= rs)
    keep_up = same_chunk_h & (cs >= rs)
    qk_head = _iota_div((H * R, QK), 0, R) == _iota_div((H * R, QK), 1, GLA_DK)
    chunk_of_row = _iota_div((R, QK), 0, CH)
    st_keep = _iota_div((DV, QK), 0, GLA_DV) == _iota_div((DV, QK), 1, GLA_DK)
    nt_dims = (((1,), (1,)), ((), ()))

    def split3(a):
        a1 = a.astype(BF16)
        r1 = a - a1.astype(F32)
        a2 = r1.astype(BF16)
        a3 = (r1 - a2.astype(F32)).astype(BF16)
        return jnp.concatenate([a1, a2, a3], axis=1)

    def rows_of_chunks(cum, offset):
        picks = [cum[c * CH + offset:c * CH + offset + 1, :] for c in range(NC)]
        full = jnp.concatenate([jnp.broadcast_to(p, (CH, QK)) for p in picks], axis=0)
        return picks, full

    def load(t, col0, tri):
        r0 = t * R if isinstance(t, int) else pl.multiple_of(t * R, R)
        a = la_ref[0, pl.ds(r0, R), col0:col0 + QK]
        c3 = jnp.dot(tri, split3(a), preferred_element_type=F32)
        return dict(r0=r0, cum=c3[:, 0:QK] + c3[:, QK:2 * QK] + c3[:, 2 * QK:3 * QK],
                    q=qkv_ref[0, pl.ds(r0, R), 0:QK].astype(F32),
                    k=qkv_ref[0, pl.ds(r0, R), QK:2 * QK].astype(F32),
                    v=qkv_ref[0, pl.ds(r0, R), 2 * QK:])

    def scale(d, last):
        cum, q, k = d["cum"], d["q"], d["k"]
        tots, tot_b = rows_of_chunks(cum, last)
        _, mid_b = rows_of_chunks(cum, CH // 2)
        d["dec"] = [jnp.exp(t) for t in tots]
        d["qe"] = (q * jnp.exp(cum)).astype(BF16)
        d["qm"] = (q * jnp.exp(cum - mid_b)).astype(BF16)
        km = k * jnp.exp(mid_b - cum)
        d["ks"] = jnp.where(qk_head, jnp.concatenate([km] * H, axis=0), 0.0).astype(BF16)
        kl = (k * jnp.exp(tot_b - cum)).astype(BF16)
        d["kl_blocks"] = jnp.concatenate(
            [jnp.where(chunk_of_row == c, kl, jnp.zeros_like(kl)) for c in range(NC)], axis=1)
        d["v_t"] = d["v"].astype(F32).T.astype(BF16)

    def scores(d, keep_t):
        att_t = lax.dot_general(d["ks"], d["qm"], nt_dims, preferred_element_type=F32)
        d["att_t"] = jnp.where(keep_t, att_t, 0.0).astype(BF16)
        kv = jnp.dot(d["v_t"], d["kl_blocks"], preferred_element_type=F32)
        d["kv"] = [kv[:, c * QK:(c + 1) * QK] for c in range(NC)]

    def intra(d):
        o_t = jnp.concatenate(
            [jnp.dot(d["v_t"][hd * GLA_DV:(hd + 1) * GLA_DV], d["att_t"][hd * R:(hd + 1) * R],
                     preferred_element_type=F32) for hd in range(H)], axis=0)
        d["o"] = o_t.T

    def inter(tiles, order, st_ref, o_scr):
        state = st_ref[...]
        for d in tiles:
            parts = [None] * NC
            for c in order:
                parts[c] = lax.dot_general(d["qe"][c * CH:(c + 1) * CH], state.astype(BF16), nt_dims,
                                           preferred_element_type=F32)
                state = state * d["dec"][c] + jnp.where(st_keep, d["kv"][c], 0.0)
            o_scr[pl.ds(d["r0"], R), :] = d["o"] + jnp.concatenate(parts, axis=0)
        st_ref[...] = state

    def advance(f_tiles, b_tiles):
        fs = [load(t, 0, tri_lo) for t in f_tiles]
        bs = [load(t, QK, tri_up) for t in b_tiles]
        for d in fs:
            scale(d, CH - 1)
        for d in bs:
            scale(d, 0)
        for d in fs:
            scores(d, keep_up)
        for d in bs:
            scores(d, keep_lo)
        for d in fs + bs:
            intra(d)
        inter(fs, range(NC), stf, of_s)
        inter(bs, range(NC - 1, -1, -1), stb, ob_s)

    for s in range(n_ctx_tiles):
        advance([s], [n_ctx_tiles - 1 - s])
    n_lat_tiles = n_tiles - n_ctx_tiles
    per_step = GLA_TILES_PER_STEP if n_lat_tiles % GLA_TILES_PER_STEP == 0 else 1

    def step(s, carry):
        first = n_ctx_tiles + s * per_step
        last = n_tiles - 1 - s * per_step
        advance([first + j for j in range(per_step)], [last - j for j in range(per_step)])
        return carry

    lax.fori_loop(0, n_lat_tiles // per_step, step, 0)

    gmean = (_iota_div((DV, DV), 0, GLA_DV) == _iota_div((DV, DV), 1, GLA_DV)).astype(BF16) * (1.0 / GLA_DV)
    gn = gn_ref[...]

    def finish(i, carry):
        r0 = pl.multiple_of(i * R, R)
        o = of_s[pl.ds(r0, R), :] + ob_s[pl.ds(r0, R), :]
        oo = o * o
        hi = oo.astype(BF16)
        lo = (oo - hi.astype(F32)).astype(BF16)
        ms = jnp.dot(hi, gmean, preferred_element_type=F32) + jnp.dot(lo, gmean, preferred_element_type=F32)
        o_ref[0, pl.ds(r0, R), :] = (o * lax.rsqrt(ms + EPS) * gn).astype(BF16)
        return carry

    lax.fori_loop(0, n_tiles, finish, 0, unroll=True)


def _gla(gqkv, la, gla_norm, n_ctx, layer):
    B, T, _ = gqkv.shape
    kern = functools.partial(_gla_kernel, n_ctx=n_ctx)
    return pl.pallas_call(
        kern,
        grid=(B,),
        in_specs=[
            pl.BlockSpec((1, T, 2 * GLA_QK_DIM + GLA_V_DIM), lambda b: (b, 0, 0)),
            pl.BlockSpec((1, T, 2 * GLA_QK_DIM), lambda b: (b, 0, 0)),
            _layer_spec((1, GLA_V_DIM), layer),
        ],
        out_specs=pl.BlockSpec((1, T, GLA_V_DIM), lambda b: (b, 0, 0)),
        out_shape=jax.ShapeDtypeStruct((B, T, GLA_V_DIM), BF16),
        scratch_shapes=[pltpu.VMEM((T, GLA_V_DIM), F32), pltpu.VMEM((T, GLA_V_DIM), F32),
                        pltpu.VMEM((GLA_V_DIM, GLA_QK_DIM), F32),
                        pltpu.VMEM((GLA_V_DIM, GLA_QK_DIM), F32)],
        compiler_params=_params(48, 1),
        name="gla",
    )(gqkv, la, gla_norm)


N_MERGE_TILED = 5


def _pool_tile(u, before, after, pos0, seq_len):
    R = u.shape[0]
    ext = jnp.concatenate([before, u, after], axis=0)
    n = ext.shape[0]

    def shifted(a, k):
        return pltpu.roll(a, (-k) % n, 0)

    w2 = ext + shifted(ext, -1)
    w4 = shifted(w2, 1) + shifted(w2, -1)
    w8 = shifted(w4, 2) + shifted(w4, -2)
    w16 = shifted(w8, 4) + shifted(w8, -4)
    sums = [a[SUBLANE:SUBLANE + R] for a in (w2, w4, w8, w16)]
    lane_group = _iota_div((R, POOL_DIM), 1, POOL_GROUP_DIM)
    pos = lax.broadcasted_iota(jnp.int32, (R, POOL_DIM), 0) + pos0
    tot = sums[0]
    lo = jnp.full((R, POOL_DIM), POOL_WINDOWS[0] // 2, jnp.int32)
    hi = jnp.full((R, POOL_DIM), POOL_WINDOWS[0] - POOL_WINDOWS[0] // 2 - 1, jnp.int32)
    for g in range(1, len(POOL_WINDOWS)):
        w = POOL_WINDOWS[g]
        tot = jnp.where(lane_group == g, sums[g], tot)
        lo = jnp.where(lane_group == g, w // 2, lo)
        hi = jnp.where(lane_group == g, w - w // 2 - 1, hi)
    cnt = (jnp.minimum(pos + hi + 1, seq_len) - jnp.maximum(pos - lo, 0)).astype(F32)
    return tot / cnt - u


def _merge_kernel(*refs, n_x, n_ctx_tiles, tile_off, kept, n_ctx, n_lat):
    D = D_MODEL
    R = ROW_TILE
    x_refs, refs = refs[:n_x], refs[n_x:]
    n_tiled = PAIR * N_MERGE_TILED
    tiled = refs[:n_tiled]
    halos = refs[n_tiled:n_tiled + 2 * PAIR]
    mod_refs = refs[n_tiled + 2 * PAIR:n_tiled + 3 * PAIR]
    (gpost_ref, wba_ref, pbd_ref, ps_ref, wbp_ref, wbg_ref, wmg0_ref, wmg1_ref, wmg2_ref, wo_ref,
     o_ref) = refs[n_tiled + 3 * PAIR:]
    wmg_refs = (wmg0_ref, wmg1_ref, wmg2_ref)
    tiles_per_seq = n_ctx_tiles + n_lat // R

    for t in range(PAIR):
        att_ref, pu_ref, on_ref, sgr_ref, h_ref = tiled[t::PAIR]
        h = h_ref[...]

        def gate(j):
            return _sigmoid(jnp.dot(h, wmg_refs[j][...], preferred_element_type=F32))

        m = gate(0) * jnp.dot(att_ref[...], wba_ref[...], preferred_element_type=F32)

        i = lax.rem(pl.program_id(0) * PAIR + t, kept) + tile_off
        is_ctx = i < n_ctx_tiles
        seq_start = (i == 0) | (i == n_ctx_tiles)
        seq_end = (i == n_ctx_tiles - 1) | (i == tiles_per_seq - 1)
        before = jnp.where(seq_start, 0.0, halos[2 * t][...])
        after = jnp.where(seq_end, 0.0, halos[2 * t + 1][...])
        d_pool = _pool_tile(pu_ref[...], before, after, jnp.where(is_ctx, i, i - n_ctx_tiles) * R,
                            jnp.where(is_ctx, n_ctx, n_lat))
        yp = jnp.dot(d_pool.astype(BF16), pbd_ref[...], preferred_element_type=F32) * ps_ref[...]
        m = m + gate(1) * jnp.dot(yp.astype(BF16), wbp_ref[...], preferred_element_type=F32)
        m = m + gate(2) * jnp.dot(on_ref[...] * sgr_ref[...], wbg_ref[...], preferred_element_type=F32)
        y = jnp.dot(m.astype(BF16), wo_ref[...], preferred_element_type=F32)
        ms = jnp.mean(y * y, axis=-1, keepdims=True)
        o_ref[t * R:(t + 1) * R, :] = (
            _stream_tile(x_refs, t, tiles_per_seq, n_ctx_tiles)
            + mod_refs[t][0][:, 2 * D:3 * D] * (y * lax.rsqrt(ms + EPS) * gpost_ref[...]))


def _merge(x_src, y_att, pu, on, sgr, h, mods, g_post, wba, pbd, ps, wbp, wbg, wmg, wo,
           n_ctx_tiles, tile_off, layer):
    B, T, _ = pu.shape
    D = D_MODEL
    R = ROW_TILE
    H = SUBLANE
    nt = T // R
    kept = nt - tile_off
    assert (B * kept) % PAIR == 0
    ctx_row = mods.shape[1] - 1
    rpb = R // H
    last_blk = B * T // H - 1

    def halo_spec(t, after):
        def index(s):
            b, i = coords(s, t)
            tile = b * nt + i
            return ((jnp.minimum((tile + 1) * rpb, last_blk) if after else jnp.maximum(tile * rpb - 1, 0)), 0)
        return pl.BlockSpec((H, POOL_DIM), index)

    def coords(s, t):
        tile = s * PAIR + t
        return lax.div(tile, kept), lax.rem(tile, kept) + tile_off

    def tile_specs(width, kept_rows_only=False):
        def spec(t):
            def index(s):
                b, i = coords(s, t)
                return ((b * kept + i - tile_off) if kept_rows_only else (b * nt + i), 0)
            return pl.BlockSpec((R, width), index)
        return [spec(t) for t in range(PAIR)]

    def mod_spec(t):
        def index(s):
            b, i = coords(s, t)
            return (layer, jnp.where(i < n_ctx_tiles, ctx_row, b), 0, 0)
        return pl.BlockSpec((None, 1, 1, 6 * D), index)

    tiled = (y_att, pu, on, sgr, h)
    assert len(tiled) == N_MERGE_TILED
    assert y_att.shape[1] == kept * R and all(a.shape[1] == T for a in tiled[1:])
    flat = [a.reshape(-1, a.shape[-1]) for a in tiled]
    pu_flat = flat[1]
    if isinstance(x_src, tuple):
        assert tile_off == 0
        x_arrays, x_specs = _stream_specs(x_src, coords, nt, n_ctx_tiles)
    else:
        x_arrays, x_specs = [x_src.reshape(B * T, D)] * PAIR, tile_specs(D)
    kern = functools.partial(_merge_kernel, n_x=len(x_arrays), n_ctx_tiles=n_ctx_tiles, tile_off=tile_off,
                             kept=kept, n_ctx=n_ctx_tiles * R, n_lat=T - n_ctx_tiles * R)
    out = pl.pallas_call(
        kern,
        grid=(B * kept // PAIR,),
        in_specs=[
            *x_specs,
            *(spec for n, a in enumerate(flat) for spec in tile_specs(a.shape[-1], kept_rows_only=(n == 0))),
            *(halo_spec(t, after) for t in range(PAIR) for after in (False, True)),
            *(mod_spec(t) for t in range(PAIR)),
            _layer_spec((1, D), layer),
            _layer_spec((ATT_DIM, D), layer), _layer_spec((POOL_DIM, POOL_DIM), layer),
            _layer_spec((1, POOL_DIM), layer), _layer_spec((POOL_DIM, D), layer),
            _layer_spec((GLA_V_DIM, D), layer),
            *(pl.BlockSpec((None, D, D), lambda s, j=j: (layer, 0, j),
                           pipeline_mode=pl.Buffered(1)) for j in range(3)),
            _layer_spec((D, D), layer),
        ],
        out_specs=pl.BlockSpec((PAIR * R, D), lambda s: (s, 0)),
        out_shape=jax.ShapeDtypeStruct((B * kept * R, D), F32),
        compiler_params=_params(48, 1),
        name="merge",
    )(*x_arrays, *(a for a in flat for _ in range(PAIR)), *([pu_flat] * (2 * PAIR)), *([mods] * PAIR),
      g_post, wba, pbd, ps, wbp, wbg, wmg, wmg, wmg, wo)
    return out.reshape(B, kept * R, D)


def _ffn_kernel(*refs, n_ctx_tiles, tiles_per_seq):
    D = D_MODEL
    R = ROW_TILE
    S = SUBLANE
    G = R // S
    CK = FF_CHUNK
    n_chunks = D_FF // CK
    NT = FFN_TILES
    x_ref = refs[0]
    halo_refs = [(refs[1 + 2 * t], refs[2 + 2 * t]) for t in range(NT)]
    mods = [refs[1 + 2 * NT + t][0] for t in range(NT)]
    (gpre_ref, wup_ref, cw_ref, cb_ref, wdn_ref, gpost_ref, o_ref,
     he_s, u_s, act_s, acc_s) = refs[1 + 3 * NT:]
    r_idx = lax.broadcasted_iota(jnp.int32, (R, R), 0)
    c_idx = lax.broadcasted_iota(jnp.int32, (R, R), 1)

    def seq_of(p):
        return jnp.bitwise_and(p, S - 1) * G + lax.shift_right_logical(p, SUBLANE_LOG2)

    to_perm = (c_idx == seq_of(r_idx)).astype(BF16)
    to_seq = (r_idx == seq_of(c_idx)).astype(BF16)
    sub = lax.broadcasted_iota(jnp.int32, (S, 2 * CK), 0)
    halo_row = lax.broadcasted_iota(jnp.int32, (S, 1), 0)

    def chunk_cols(ref, j):
        return jnp.concatenate([ref[:, j * CK:(j + 1) * CK], ref[:, D_FF + j * CK:D_FF + (j + 1) * CK]],
                               axis=1)

    def prologue(t):
        i = lax.rem(pl.program_id(0) * NT + t, tiles_per_seq)
        scale, shift = mods[t][:, 4 * D:5 * D], mods[t][:, 3 * D:4 * D]
        h = _modulated_norm(x_ref[t * R:(t + 1) * R, :], gpre_ref[...], scale, shift).astype(BF16)
        he_s[t, 0:R, :] = jnp.dot(to_perm, h, preferred_element_type=F32).astype(BF16)
        xp_ref, xn_ref = halo_refs[t]
        halo = jnp.concatenate([xp_ref[S - 1:S, :], xn_ref[0:1, :], jnp.zeros((S - 2, D), F32)], axis=0)
        seq_start = (i == 0) | (i == n_ctx_tiles)
        seq_end = (i == n_ctx_tiles - 1) | (i == tiles_per_seq - 1)
        outside = ((halo_row == 0) & seq_start) | ((halo_row == 1) & seq_end) | (halo_row >= 2)
        he_s[t, R:R + S, :] = jnp.where(
            outside, 0.0, _modulated_norm(halo, gpre_ref[...], scale, shift)).astype(BF16)

    def up(t, j):
        he = he_s[t]
        u_s[t, j % 2, :, 0:CK] = jnp.dot(he, wup_ref[:, j * CK:(j + 1) * CK], preferred_element_type=F32)
        u_s[t, j % 2, :, CK:] = jnp.dot(he, wup_ref[:, D_FF + j * CK:D_FF + (j + 1) * CK],
                                        preferred_element_type=F32)

    def finish(t, j):
        u = u_s[t, j % 2, 0:R, :]
        edge = u_s[t, j % 2, R:R + S, :]
        prev0 = jnp.where(sub == 0, edge[0:1], pltpu.roll(u[R - S:R], 1, 0))
        next_last = jnp.where(sub == S - 1, edge[1:2], pltpu.roll(u[0:S], S - 1, 0))
        prev = jnp.concatenate([prev0, u[0:R - S]], axis=0)
        nxt = jnp.concatenate([u[S:R], next_last], axis=0)
        cw = chunk_cols(cw_ref, j)
        c = prev * cw[0:1] + u * cw[1:2] + nxt * cw[2:3] + chunk_cols(cb_ref, j)
        g = c[:, CK:]
        act_s[t, :, j * CK:(j + 1) * CK] = (c[:, :CK] * (g * _sigmoid(g))).astype(BF16)

    def down(t, j0, j1):
        y = jnp.dot(act_s[t, :, j0 * CK:j1 * CK], wdn_ref[j0 * CK:j1 * CK, :], preferred_element_type=F32)
        if j0 == 0:
            acc_s[t] = y
        else:
            acc_s[t] += y

    def epilogue(t):
        y = jnp.dot(to_seq, acc_s[t].astype(BF16), preferred_element_type=F32)
        ms = jnp.mean(y * y, axis=-1, keepdims=True)
        o_ref[t * R:(t + 1) * R, :] = (x_ref[t * R:(t + 1) * R, :]
                                       + mods[t][:, 5 * D:6 * D] * (y * lax.rsqrt(ms + EPS) * gpost_ref[...]))

    def body(t, after_first_up):
        up(t, 0)
        after_first_up()
        group_start = 0
        for j in range(n_chunks):
            if j + 1 < n_chunks:
                up(t, j + 1)
            finish(t, j)
            if j + 1 - group_start == FF_GROUP or j + 1 == n_chunks:
                down(t, group_start, j + 1)
                group_start = j + 1

    def between(t):
        if t > 0:
            epilogue(t - 1)
        if t + 1 < NT:
            prologue(t + 1)

    prologue(0)
    for t in range(NT):
        body(t, functools.partial(between, t))
    epilogue(NT - 1)


def _ffn(xa, mods, g_pre, w_up, conv_w, conv_b, w_down, g_post, n_ctx_tiles, layer):
    B, T, D = xa.shape
    R = ROW_TILE
    H = SUBLANE
    nt = T // R
    NT = FFN_TILES
    assert (B * nt) % NT == 0
    ctx_row = mods.shape[1] - 1
    rpb = R // H
    RE = R + H
    last_blk = B * T // H - 1

    def halo(t, after):
        def index(s):
            tile = s * NT + t
            return ((jnp.minimum((tile + 1) * rpb, last_blk) if after else jnp.maximum(tile * rpb - 1, 0)), 0)
        return pl.BlockSpec((H, D), index)

    def mod_spec(t):
        def index(s):
            tile = s * NT + t
            return (layer, jnp.where(lax.rem(tile, nt) < n_ctx_tiles, ctx_row, lax.div(tile, nt)), 0, 0)
        return pl.BlockSpec((None, 1, 1, 6 * D), index)

    x2 = xa.reshape(B * T, D)
    kern = functools.partial(_ffn_kernel, n_ctx_tiles=n_ctx_tiles, tiles_per_seq=nt)
    out = pl.pallas_call(
        kern,
        grid=(B * nt // NT,),
        in_specs=[
            pl.BlockSpec((NT * R, D), lambda s: (s, 0)),
            *(halo(t, after) for t in range(NT) for after in (False, True)),
            *(mod_spec(t) for t in range(NT)),
            _layer_spec((1, D), layer),
            _layer_spec((D, 2 * D_FF), layer), _layer_spec((3, 2 * D_FF), layer),
            _layer_spec((1, 2 * D_FF), layer), _layer_spec((D_FF, D), layer),
            _layer_spec((1, D), layer),
        ],
        out_specs=pl.BlockSpec((NT * R, D), lambda s: (s, 0)),
        out_shape=jax.ShapeDtypeStruct((B * T, D), F32),
        scratch_shapes=[pltpu.VMEM((NT, RE, D), BF16),
                        pltpu.VMEM((NT, 2, RE, 2 * FF_CHUNK), F32),
                        pltpu.VMEM((NT, R, D_FF), BF16),
                        pltpu.VMEM((NT, R, D), F32)],
        compiler_params=_params(56, 1),
        name="ffn",
    )(*([x2] * (1 + 2 * NT)), *([mods] * NT), g_pre, w_up, conv_w, conv_b, w_down, g_post)
    return out.reshape(B, T, D)


def _rope_table(n_ctx, n_lat):
    rows = n_lat // GRID_W
    row = jnp.repeat(jnp.arange(rows, dtype=F32), GRID_W)
    col = jnp.tile(jnp.arange(GRID_W, dtype=F32), rows)
    n = HEAD_DIM // 4
    inv = ROPE_BASE ** (-jnp.arange(n, dtype=F32) / n)
    ang = jnp.concatenate([row[:, None] * inv, col[:, None] * inv], axis=-1)
    cos, sin = jnp.cos(ang), jnp.sin(ang)
    zero = jnp.zeros_like(sin)
    reps = LANE // HEAD_DIM
    cos_t = jnp.tile(jnp.concatenate([cos, cos], axis=-1), (1, reps))
    sin_lo = jnp.tile(jnp.concatenate([-sin, zero], axis=-1), (1, reps))
    sin_hi = jnp.tile(jnp.concatenate([zero, sin], axis=-1), (1, reps))
    lat = jnp.concatenate([cos_t, sin_lo, sin_hi], axis=-1)
    ctx = jnp.concatenate([jnp.ones((n_ctx, LANE), F32), jnp.zeros((n_ctx, 2 * LANE), F32)], axis=-1)
    return jnp.concatenate([ctx, lat], axis=0)


def kernel(x, c, ctx, c_ctx, w_ada, b_ada, g_pre_mix, g_post_mix, g_pre_ffn, g_post_ffn, w_in, att_sink, pool_w, pool_scale, gla_wa2, gla_ba, gla_norm, w_br_att, w_br_pool, w_br_gla, w_o, w_up, conv_w, conv_b, w_down):
    B, L, D = x.shape
    C = ctx.shape[1]
    depth = w_in.shape[0]
    R = ROW_TILE
    assert D == D_MODEL and C % R == 0 and L % R == 0 and L % GRID_W == 0
    n_ctx_tiles = C // R

    mod_rows = -(-(B + 1) // SUBLANE) * SUBLANE
    cc = jnp.zeros((mod_rows, D), F32).at[:B].set(c).at[mod_rows - 1].set(c_ctx)
    mods = _ada_table(cc, w_ada, b_ada).reshape(depth, mod_rows, 1, 6 * D)

    rope_tab = _rope_table(C, L)
    w_in_p, w_mg = _split_w_in(w_in.astype(BF16))
    wa = jnp.zeros((depth, LANE, 2 * GLA_QK_DIM), F32)
    wa = wa.at[:, 0:GLA_GATE_RANK, 0:GLA_QK_DIM].set(gla_wa2[:, 0])
    wa = wa.at[:, GLA_GATE_RANK:2 * GLA_GATE_RANK, GLA_QK_DIM:].set(gla_wa2[:, 1])
    wa_hi = wa.astype(BF16)
    wa = jnp.concatenate([wa_hi, wa_hi, (wa - wa_hi.astype(F32)).astype(BF16)], axis=1)
    ba = gla_ba.reshape(depth, 1, 2 * GLA_QK_DIM)
    pbd = jnp.zeros((depth, POOL_DIM, POOL_DIM), F32)
    for g in range(len(POOL_WINDOWS)):
        sl = slice(g * POOL_GROUP_DIM, (g + 1) * POOL_GROUP_DIM)
        pbd = pbd.at[:, sl, sl].set(pool_w[:, g])
    pbd = pbd.astype(BF16)
    wba, wbp, wbg, wo = (w.astype(BF16) for w in (w_br_att, w_br_pool, w_br_gla, w_o))
    wup, wdn = w_up.astype(BF16), w_down.astype(BF16)

    def rows(t):
        return t.reshape(depth, 1, -1)

    g_pre_mix, g_post_mix, g_pre_ffn, g_post_ffn, pool_scale, gla_norm, conv_b = (
        rows(t) for t in (g_pre_mix, g_post_mix, g_pre_ffn, g_post_ffn, pool_scale, gla_norm, conv_b))

    xa = (ctx, x) if depth > 1 else jnp.concatenate([ctx, x], axis=1)
    for l in range(depth):
        last = l == depth - 1
        off = n_ctx_tiles if last else 0
        q, kv, pu, gqkv, sgr, la, h = _in_proj(xa, mods, g_pre_mix, w_in_p, rope_tab, wa, ba, n_ctx_tiles, l)
        y_att = _attention(q, kv, att_sink, C, off, l)
        on = _gla(gqkv, la, gla_norm, C, l)
        xa = _merge(xa, y_att, pu, on, sgr, h, mods, g_post_mix, wba, pbd, pool_scale, wbp, wbg, w_mg, wo,
                    n_ctx_tiles, off, l)
        xa = _ffn(xa, mods, g_pre_ffn, wup, conv_w, conv_b, wdn, g_post_ffn, n_ctx_tiles - off, l)
    return xa
```
